```python
import math
import jax, jax.numpy as jnp
from jax import lax
import numpy as np

D_MODEL = 1024
BATCH = 2
SEQ = 16384
DEPTH = 4
DEC_BATCH = 16
DEC_SEQ = 64
PAST_LEN = 1024

CHUNK = 64
N_GROUPS_MIX = 4
GROUP_W = D_MODEL // N_GROUPS_MIX
D_MIX = N_GROUPS_MIX * GROUP_W
D_FF = 4 * D_MODEL
EPS = 1e-6
NEG_INF = -1e30

A_HEADS = 4
A_HD = GROUP_W // A_HEADS
A_BAND_CHUNKS = 8
A_SPAN = A_BAND_CHUNKS * CHUNK
A_REL_CLIP = 128

B_HEADS = 4
B_HD = GROUP_W // B_HEADS
B_INNER = B_HEADS * B_HD
B_GROUPS = 2
B_STATE = 64
B_CONV = 4
B_CONV_DIM = B_INNER + 2 * B_GROUPS * B_STATE

C_HEADS = 4
C_NOPE = 64
C_ROPE = 32
C_VD = GROUP_W // C_HEADS
C_Q_LORA = 256
C_KV_LORA = 128
ROPE_BASE = 10000.0
Q_BLOCK = 128

D_HEADS = 4
D_HD = GROUP_W // D_HEADS

IN_SIZES = (GROUP_W, GROUP_W, GROUP_W,
            B_INNER, B_CONV_DIM, B_HEADS,
            C_Q_LORA, C_KV_LORA, C_ROPE,
            GROUP_W, GROUP_W, GROUP_W, D_HEADS, D_HEADS, GROUP_W)
N_IN = sum(IN_SIZES)

kernel_name = 'hybrid_streaming_encoder_step'


def _rmsnorm(x, g):
    xf = x.astype(jnp.float32)
    y = xf * lax.rsqrt(jnp.mean(xf * xf, axis=-1, keepdims=True) + EPS)
    return (y * g.astype(jnp.float32)).astype(x.dtype)


def _rope(x, pos):
    half = x.shape[-1] // 2
    inv = jnp.exp(-math.log(ROPE_BASE) * jnp.arange(half, dtype=jnp.float32) / half)
    ang = pos.astype(jnp.float32)[:, None] * inv[None, :]
    cos = jnp.cos(ang)[None, :, None, :]
    sin = jnp.sin(ang)[None, :, None, :]
    xf = x.astype(jnp.float32)
    x1, x2 = xf[..., :half], xf[..., half:]
    return jnp.concatenate([x1 * cos - x2 * sin, x1 * sin + x2 * cos], axis=-1).astype(x.dtype)


def _band_mask(qpos, kpos):
    qc = qpos[..., :, None] // CHUNK
    kc = kpos[..., None, :] // CHUNK
    return (kpos[..., None, :] >= 0) & (kc <= qc) & (kc >= qc - A_BAND_CHUNKS)


def _rel_bias(table, qpos, kpos):
    rel = jnp.clip(qpos[:, None] - kpos[None, :], -A_REL_CLIP, A_REL_CLIP) + A_REL_CLIP
    return table[:, rel].astype(jnp.float32)


def _chunk_attn_prompt(q, k, v, rel_table):
    b, s, h, d = q.shape
    nc = s // CHUNK
    nb = A_BAND_CHUNKS + 1

    def band(t):
        tp = jnp.concatenate([jnp.zeros((b, A_SPAN, h, d), t.dtype), t], axis=1)
        tp = tp.reshape(b, nc + A_BAND_CHUNKS, CHUNK, h, d)
        return jnp.concatenate([tp[:, j:j + nc] for j in range(nb)], axis=2)

    kb, vb = band(k), band(v)
    qc = q.reshape(b, nc, CHUNK, h, d)
    q_local = jnp.arange(CHUNK)
    k_local = jnp.arange(nb * CHUNK) - A_SPAN
    bias = _rel_bias(rel_table, q_local, k_local)
    start = jnp.arange(nc) * CHUNK
    mask = _band_mask(start[:, None] + q_local[None], start[:, None] + k_local[None])
    logits = jnp.einsum('bclhd,bcshd->bchls', qc, kb).astype(jnp.float32) * (d ** -0.5) + bias[None, None]
    logits = jnp.where(mask[None, :, None], logits, NEG_INF)
    p = jax.nn.softmax(logits, axis=-1).astype(v.dtype)
    o = jnp.einsum('bchls,bcshd->bclhd', p, vb)
    return o.reshape(b, s, h * d)


def _chunk_attn_cached(q, k, v, qpos, kpos, rel_table):
    b, L, h, d = q.shape
    logits = jnp.einsum('blhd,bshd->bhls', q, k).astype(jnp.float32) * (d ** -0.5)
    logits = logits + _rel_bias(rel_table, qpos, kpos)[None]
    logits = jnp.where(_band_mask(qpos, kpos)[None, None], logits, NEG_INF)
    p = jax.nn.softmax(logits, axis=-1).astype(v.dtype)
    return jnp.einsum('bhls,bshd->blhd', p, v).reshape(b, L, h * d)


def _ssd(x, dt, a, bm, cm, h0, chunk):
    b, L, h, p = x.shape
    nc = L // chunk
    rep = h // B_GROUPS
    bm = jnp.repeat(bm, rep, axis=2).reshape(b, nc, chunk, h, B_STATE)
    cm = jnp.repeat(cm, rep, axis=2).reshape(b, nc, chunk, h, B_STATE)
    xdt = (x * dt[..., None]).reshape(b, nc, chunk, h, p)
    da = (dt * a).reshape(b, nc, chunk, h).transpose(0, 1, 3, 2)
    da_cs = jnp.cumsum(da, axis=-1)
    tri = jnp.tril(jnp.ones((chunk, chunk), bool))
    decay_in = jnp.exp(jnp.where(tri, da_cs[..., :, None] - da_cs[..., None, :], -jnp.inf))
    y_diag = jnp.einsum('bclhn,bcshn,bchls,bcshp->bclhp', cm, bm, decay_in, xdt)
    decay_end = jnp.exp(da_cs[..., -1:] - da_cs)
    chunk_states = jnp.einsum('bclhn,bchl,bclhp->bchpn', bm, decay_end, xdt)
    chunk_decay = jnp.exp(da_cs[..., -1])

    def step(s, inp):
        st_c, dec_c = inp
        return dec_c[..., None, None] * s + st_c, s

    h_final, h_prev = lax.scan(step, h0, (jnp.moveaxis(chunk_states, 1, 0), jnp.moveaxis(chunk_decay, 1, 0)))
    h_prev = jnp.moveaxis(h_prev, 0, 1)
    y_off = jnp.einsum('bclhn,bchpn,bchl->bclhp', cm, h_prev, jnp.exp(da_cs))
    return (y_diag + y_off).reshape(b, L, h, p), h_final


def _mamba2(z, xbc, dt_raw, conv_state, ssm_state, conv_w, conv_b, dt_bias, a_log, d_skip, norm_g, chunk):
    b, L, _ = xbc.shape
    xpad = jnp.concatenate([conv_state.astype(xbc.dtype), xbc], axis=1)
    conv = conv_b + sum(xpad[:, j:j + L] * conv_w[j] for j in range(B_CONV))
    new_conv = xpad[:, L:]
    u = jax.nn.silu(conv.astype(jnp.float32))
    xs = u[..., :B_INNER].reshape(b, L, B_HEADS, B_HD)
    bm = u[..., B_INNER:B_INNER + B_GROUPS * B_STATE].reshape(b, L, B_GROUPS, B_STATE)
    cm = u[..., B_INNER + B_GROUPS * B_STATE:].reshape(b, L, B_GROUPS, B_STATE)
    dt = jax.nn.softplus(dt_raw.astype(jnp.float32) + dt_bias.astype(jnp.float32))
    a = -jnp.exp(a_log.astype(jnp.float32))
    y, new_ssm = _ssd(xs, dt, a, bm, cm, ssm_state.astype(jnp.float32), chunk)
    y = y + d_skip.astype(jnp.float32)[:, None] * xs
    y = y.reshape(b, L, B_INNER) * jax.nn.silu(z.astype(jnp.float32))
    return _rmsnorm(y, norm_g).astype(z.dtype), new_conv, new_ssm


def _mla_project(c_q, c_kv, c_kr, pos, q_norm_g, w_uq, kv_norm_g):
    b, L = c_q.shape[:2]
    q = (_rmsnorm(c_q, q_norm_g) @ w_uq).reshape(b, L, C_HEADS, C_NOPE + C_ROPE)
    q_nope, q_rope = q[..., :C_NOPE], _rope(q[..., C_NOPE:], pos)
    ckv = _rmsnorm(c_kv, kv_norm_g)
    kr = _rope(c_kr[:, :, None, :], pos)[:, :, 0]
    return q_nope, q_rope, ckv, kr


def _mla_attend(q_nope, q_rope, ckv, kr, qpos, kpos, w_ukv):
    b, Lk = ckv.shape[:2]
    kv = (ckv @ w_ukv).reshape(b, Lk, C_HEADS, C_NOPE + C_VD)
    k_nope, v = kv[..., :C_NOPE], kv[..., C_NOPE:]
    Lq = q_nope.shape[1]
    qb = min(Q_BLOCK, Lq)
    nb = Lq // qb
    scale = (C_NOPE + C_ROPE) ** -0.5

    def block(args):
        qn, qr, qp = args
        logits = (jnp.einsum('bqhd,bkhd->bhqk', qn, k_nope)
                  + jnp.einsum('bqhr,bkr->bhqk', qr, kr)).astype(jnp.float32) * scale
        mask = (kpos[None, :] // CHUNK) <= (qp[:, None] // CHUNK)
        logits = jnp.where(mask[None, None], logits, NEG_INF)
        p = jax.nn.softmax(logits, axis=-1).astype(v.dtype)
        return jnp.einsum('bhqk,bkhd->bqhd', p, v)

    qn_b = q_nope.reshape(b, nb, qb, C_HEADS, C_NOPE).transpose(1, 0, 2, 3, 4)
    qr_b = q_rope.reshape(b, nb, qb, C_HEADS, C_ROPE).transpose(1, 0, 2, 3, 4)
    out = lax.map(block, (qn_b, qr_b, qpos.reshape(nb, qb)))
    return out.transpose(1, 0, 2, 3, 4).reshape(b, Lq, C_HEADS * C_VD)


def _mlstm(q, k, v, i_pre, f_pre, c0, n0, m0, chunk):
    b, L, h, d = q.shape
    nc = L // chunk

    def to_chunks(t):
        return t.astype(jnp.float32).reshape(b, nc, chunk, h, -1).transpose(1, 0, 3, 2, 4)

    qc, kc, vc = to_chunks(q), to_chunks(k * (d ** -0.5)), to_chunks(v)
    lf = to_chunks(jax.nn.log_sigmoid(f_pre.astype(jnp.float32))[..., None])[..., 0]
    ig = to_chunks(i_pre[..., None])[..., 0]
    tri = jnp.tril(jnp.ones((chunk, chunk), bool))

    def step(carry, inp):
        cmat, nvec, m = carry
        qq, kk, vv, lff, ii = inp
        bcum = jnp.cumsum(lff, axis=-1)
        dmat = jnp.where(tri, bcum[..., :, None] - bcum[..., None, :] + ii[..., None, :], -jnp.inf)
        inter = bcum + m[..., None]
        m_t = jnp.maximum(inter, jnp.max(dmat, axis=-1))
        w_intra = jnp.exp(dmat - m_t[..., None])
        w_inter = jnp.exp(inter - m_t)
        s = jnp.einsum('bhld,bhsd->bhls', qq, kk) * w_intra
        num = w_inter[..., None] * jnp.einsum('bhld,bhdv->bhlv', qq, cmat) + jnp.einsum('bhls,bhsv->bhlv', s, vv)
        den = w_inter * jnp.einsum('bhld,bhd->bhl', qq, nvec) + jnp.sum(s, axis=-1)
        hout = num / jnp.maximum(jnp.abs(den), jnp.exp(-m_t))[..., None]
        g_end = bcum[..., -1:] - bcum + ii
        m_new = jnp.maximum(bcum[..., -1] + m, jnp.max(g_end, axis=-1))
        w_end = jnp.exp(g_end - m_new[..., None])
        decay = jnp.exp(bcum[..., -1] + m - m_new)
        c_new = decay[..., None, None] * cmat + jnp.einsum('bhs,bhsd,bhsv->bhdv', w_end, kk, vv)
        n_new = decay[..., None] * nvec + jnp.einsum('bhs,bhsd->bhd', w_end, kk)
        return (c_new, n_new, m_new), hout

    init = (c0.astype(jnp.float32), n0.astype(jnp.float32), m0.astype(jnp.float32))
    (c_f, n_f, m_f), hs = lax.scan(step, init, (qc, kc, vc, lf, ig))
    hs = hs.transpose(1, 0, 3, 2, 4).reshape(b, L, h, d)
    return hs, c_f, n_f, m_f


def _layer(x, lp, st):
    b, L, _ = x.shape
    past = 0 if st is None else st['ckv'].shape[1]
    pos = past + jnp.arange(L)
    chunk = min(CHUNK, L)
    hn = _rmsnorm(x, lp['norm1_g'])
    proj = hn @ lp['w_in']
    cuts = [int(c) for c in np.cumsum(IN_SIZES)[:-1]]
    (a_q, a_k, a_v, b_z, b_xbc, b_dt, c_q, c_kv, c_kr,
     m_q, m_k, m_v, m_i, m_f, m_o) = jnp.split(proj, cuts, axis=-1)

    a_q = a_q.reshape(b, L, A_HEADS, A_HD)
    a_k = a_k.reshape(b, L, A_HEADS, A_HD)
    a_v = a_v.reshape(b, L, A_HEADS, A_HD)
    if st is None:
        ya = _chunk_attn_prompt(a_q, a_k, a_v, lp['attn_rel'])
        keep = min(A_SPAN, L)
        new_ak, new_av = a_k[:, L - keep:], a_v[:, L - keep:]
    else:
        n_cache = st['ak'].shape[1]
        k_all = jnp.concatenate([st['ak'].astype(a_k.dtype), a_k], axis=1)
        v_all = jnp.concatenate([st['av'].astype(a_v.dtype), a_v], axis=1)
        kpos = past - n_cache + jnp.arange(n_cache + L)
        ya = _chunk_attn_cached(a_q, k_all, v_all, pos, kpos, lp['attn_rel'])
        new_ak, new_av = k_all[:, L:], v_all[:, L:]

    if st is None:
        conv0 = jnp.zeros((b, B_CONV - 1, B_CONV_DIM), x.dtype)
        ssm0 = jnp.zeros((b, B_HEADS, B_HD, B_STATE), jnp.float32)
    else:
        conv0, ssm0 = st['conv'], st['ssm']
    yb, new_conv, new_ssm = _mamba2(b_z, b_xbc, b_dt, conv0, ssm0, lp['conv_w'], lp['conv_b'],
                                    lp['dt_bias'], lp['a_log'], lp['d_skip'], lp['ssm_norm_g'], chunk)

    q_nope, q_rope, ckv, kr = _mla_project(c_q, c_kv, c_kr, pos, lp['q_norm_g'], lp['w_uq'], lp['kv_norm_g'])
    if st is None:
        ckv_all, kr_all, kpos_c = ckv, kr, pos
    else:
        ckv_all = jnp.concatenate([st['ckv'].astype(ckv.dtype), ckv], axis=1)
        kr_all = jnp.concatenate([st['kr'].astype(kr.dtype), kr], axis=1)
        kpos_c = jnp.arange(past + L)
    yc = _mla_attend(q_nope, q_rope, ckv_all, kr_all, pos, kpos_c, lp['w_ukv'])

    if st is None:
        c0 = jnp.zeros((b, D_HEADS, D_HD, D_HD), jnp.float32)
        n0 = jnp.zeros((b, D_HEADS, D_HD), jnp.float32)
        m0 = jnp.zeros((b, D_HEADS), jnp.float32)
    else:
        c0, n0, m0 = st['c'], st['n'], st['m']
    hd, new_c, new_n, new_m = _mlstm(m_q.reshape(b, L, D_HEADS, D_HD), m_k.reshape(b, L, D_HEADS, D_HD),
                                     m_v.reshape(b, L, D_HEADS, D_HD), m_i + lp['b_i'], m_f + lp['b_f'],
                                     c0, n0, m0, chunk)
    yd = _rmsnorm(hd, lp['mlstm_norm_g'].reshape(D_HEADS, D_HD)).reshape(b, L, GROUP_W)
    yd = (yd * jax.nn.sigmoid(m_o.astype(jnp.float32))).astype(x.dtype)

    y = jnp.concatenate([ya.astype(x.dtype), yb.astype(x.dtype), yc.astype(x.dtype), yd], axis=-1) @ lp['w_out']
    x = x + y
    hm = _rmsnorm(x, lp['norm2_g'])
    x = x + jnp.square(jax.nn.relu(hm @ lp['w1'])) @ lp['w2']
    return x, (new_ak, new_av, new_conv, new_ssm, ckv, kr, new_c, new_n, new_m)


def _stack_states(states):
    return tuple(jnp.stack([s[i] for s in states]) for i in range(len(states[0])))


def setup_inputs(seed: int = 0) -> dict:
    key = jax.random.key(seed)
    keys = jax.random.split(key, 48)
    counter = [0]

    def nxt():
        k = keys[counter[0]]
        counter[0] += 1
        return k

    def nrm(shape, scale=1.0):
        return scale * jax.random.normal(nxt(), shape, jnp.float32)

    a_cache = min(A_SPAN, PAST_LEN)
    x_prompt = nrm((BATCH, SEQ, D_MODEL))
    x_sample = nrm((DEC_BATCH, DEC_SEQ, D_MODEL))
    cache_attn_k = nrm((DEPTH, DEC_BATCH, a_cache, A_HEADS, A_HD))
    cache_attn_v = nrm((DEPTH, DEC_BATCH, a_cache, A_HEADS, A_HD))
    state_ssm_conv = nrm((DEPTH, DEC_BATCH, B_CONV - 1, B_CONV_DIM))
    state_ssm = nrm((DEPTH, DEC_BATCH, B_HEADS, B_HD, B_STATE), 0.1)
    cache_mla_ckv = nrm((DEPTH, DEC_BATCH, PAST_LEN, C_KV_LORA))
    cache_mla_kr = nrm((DEPTH, DEC_BATCH, PAST_LEN, C_ROPE))
    state_mlstm_c = nrm((DEPTH, DEC_BATCH, D_HEADS, D_HD, D_HD), 0.1)
    state_mlstm_n = nrm((DEPTH, DEC_BATCH, D_HEADS, D_HD), 0.1)
    state_mlstm_m = nrm((DEPTH, DEC_BATCH, D_HEADS), 0.5)
    norm1_g = 1.0 + nrm((DEPTH, D_MODEL), 0.02)
    w_in = nrm((DEPTH, D_MODEL, N_IN), D_MODEL ** -0.5)
    attn_rel_bias = nrm((DEPTH, A_HEADS, 2 * A_REL_CLIP + 1), 0.2)
    ssm_conv_w = nrm((DEPTH, B_CONV, B_CONV_DIM), B_CONV ** -0.5)
    ssm_conv_b = nrm((DEPTH, B_CONV_DIM), 0.02)
    dt0 = jnp.exp(jax.random.uniform(nxt(), (DEPTH, B_HEADS), jnp.float32, math.log(1e-3), math.log(1e-1)))
    ssm_dt_bias = dt0 + jnp.log(-jnp.expm1(-dt0))
    ssm_a_log = jnp.log(jax.random.uniform(nxt(), (DEPTH, B_HEADS), jnp.float32, 1.0, 16.0))
    ssm_d = 1.0 + nrm((DEPTH, B_HEADS), 0.1)
    ssm_norm_g = 1.0 + nrm((DEPTH, B_INNER), 0.02)
    mla_q_norm_g = 1.0 + nrm((DEPTH, C_Q_LORA), 0.02)
    mla_w_uq = nrm((DEPTH, C_Q_LORA, C_HEADS * (C_NOPE + C_ROPE)), C_Q_LORA ** -0.5)
    mla_kv_norm_g = 1.0 + nrm((DEPTH, C_KV_LORA), 0.02)
    mla_w_ukv = nrm((DEPTH, C_KV_LORA, C_HEADS * (C_NOPE + C_VD)), C_KV_LORA ** -0.5)
    mlstm_b_i = nrm((DEPTH, D_HEADS), 0.1)
    mlstm_b_f = jnp.linspace(3.0, 6.0, D_HEADS)[None, :] + nrm((DEPTH, D_HEADS), 0.1)
    mlstm_norm_g = 1.0 + nrm((DEPTH, GROUP_W), 0.02)
    w_out = nrm((DEPTH, D_MIX, D_MODEL), D_MIX ** -0.5)
    norm2_g = 1.0 + nrm((DEPTH, D_MODEL), 0.02)
    mlp_w1 = nrm((DEPTH, D_MODEL, D_FF), D_MODEL ** -0.5)
    mlp_w2 = nrm((DEPTH, D_FF, D_MODEL), D_FF ** -0.5)
    final_g = 1.0 + nrm((D_MODEL,), 0.02)
    return {'x_prompt': x_prompt, 'x_sample': x_sample,
            'cache_attn_k': cache_attn_k, 'cache_attn_v': cache_attn_v,
            'state_ssm_conv': state_ssm_conv, 'state_ssm': state_ssm,
            'cache_mla_ckv': cache_mla_ckv, 'cache_mla_kr': cache_mla_kr,
            'state_mlstm_c': state_mlstm_c, 'state_mlstm_n': state_mlstm_n, 'state_mlstm_m': state_mlstm_m,
            'norm1_g': norm1_g, 'w_in': w_in, 'attn_rel_bias': attn_rel_bias,
            'ssm_conv_w': ssm_conv_w, 'ssm_conv_b': ssm_conv_b, 'ssm_dt_bias': ssm_dt_bias,
            'ssm_a_log': ssm_a_log, 'ssm_d': ssm_d, 'ssm_norm_g': ssm_norm_g,
            'mla_q_norm_g': mla_q_norm_g, 'mla_w_uq': mla_w_uq, 'mla_kv_norm_g': mla_kv_norm_g,
            'mla_w_ukv': mla_w_ukv, 'mlstm_b_i': mlstm_b_i, 'mlstm_b_f': mlstm_b_f,
            'mlstm_norm_g': mlstm_norm_g, 'w_out': w_out, 'norm2_g': norm2_g,
            'mlp_w1': mlp_w1, 'mlp_w2': mlp_w2, 'final_g': final_g}


def reference(x_prompt, x_sample, cache_attn_k, cache_attn_v, state_ssm_conv, state_ssm,
              cache_mla_ckv, cache_mla_kr, state_mlstm_c, state_mlstm_n, state_mlstm_m,
              norm1_g, w_in, attn_rel_bias, ssm_conv_w, ssm_conv_b, ssm_dt_bias, ssm_a_log, ssm_d,
              ssm_norm_g, mla_q_norm_g, mla_w_uq, mla_kv_norm_g, mla_w_ukv, mlstm_b_i, mlstm_b_f,
              mlstm_norm_g, w_out, norm2_g, mlp_w1, mlp_w2, final_g):
    xp, xs = x_prompt, x_sample
    pro_states, smp_states = [], []
    for l in range(DEPTH):
        lp = {'norm1_g': norm1_g[l], 'w_in': w_in[l], 'attn_rel': attn_rel_bias[l],
              'conv_w': ssm_conv_w[l], 'conv_b': ssm_conv_b[l], 'dt_bias': ssm_dt_bias[l],
              'a_log': ssm_a_log[l], 'd_skip': ssm_d[l], 'ssm_norm_g': ssm_norm_g[l],
              'q_norm_g': mla_q_norm_g[l], 'w_uq': mla_w_uq[l], 'kv_norm_g': mla_kv_norm_g[l],
              'w_ukv': mla_w_ukv[l], 'b_i': mlstm_b_i[l], 'b_f': mlstm_b_f[l],
              'mlstm_norm_g': mlstm_norm_g[l], 'w_out': w_out[l], 'norm2_g': norm2_g[l],
              'w1': mlp_w1[l], 'w2': mlp_w2[l]}
        st = {'ak': cache_attn_k[l], 'av': cache_attn_v[l], 'conv': state_ssm_conv[l], 'ssm': state_ssm[l],
              'ckv': cache_mla_ckv[l], 'kr': cache_mla_kr[l], 'c': state_mlstm_c[l], 'n': state_mlstm_n[l],
              'm': state_mlstm_m[l]}
        xp, sp = _layer(xp, lp, None)
        xs, ss = _layer(xs, lp, st)
        pro_states.append(sp)
        smp_states.append(ss)
    y_prompt = _rmsnorm(xp, final_g)
    y_sample = _rmsnorm(xs, final_g)
    (p_attn_k, p_attn_v, p_ssm_conv, p_ssm, p_mla_ckv, p_mla_kr,
     p_mlstm_c, p_mlstm_n, p_mlstm_m) = _stack_states(pro_states)
    (s_attn_k, s_attn_v, s_ssm_conv, s_ssm, s_mla_ckv, s_mla_kr,
     s_mlstm_c, s_mlstm_n, s_mlstm_m) = _stack_states(smp_states)
    return (y_prompt, y_sample,
            p_attn_k, p_attn_v, p_ssm_conv, p_ssm, p_mla_ckv, p_mla_kr, p_mlstm_c, p_mlstm_n, p_mlstm_m,
            s_attn_k, s_attn_v, s_ssm_conv, s_ssm, s_mla_ckv, s_mla_kr, s_mlstm_c, s_mlstm_n, s_mlstm_m)
```

```python
import functools
import math

import jax
import jax.numpy as jnp
from jax import lax
from jax.experimental import pallas as pl
from jax.experimental.pallas import tpu as pltpu

F32 = jnp.float32
BF16 = jnp.bfloat16
EPS = 1e-6
NEG = -1e30
ROPE_BASE = 10000.0

CHUNK = 64
D_MODEL = 1024
GROUP_W = 256
D_FF = 4096
N_HEADS = 4
HD = 64
A_BAND_CHUNKS = 8
A_SPAN = A_BAND_CHUNKS * CHUNK
A_REL_CLIP = 128
B_GROUPS = 2
B_STATE = 64
B_CONV = 4
B_CONV_DIM = 512
C_NOPE = 64
C_ROPE = 32
C_KV_LORA = 128
C_SCALE = (C_NOPE + C_ROPE) ** -0.5
LANE = 128
VMEM_LIMIT = 56 * 1024 * 1024

OFF_A, OFF_BZ, OFF_XBC, OFF_CQ, OFF_CKV, OFF_KR, OFF_KRR, OFF_MQKV, OFF_MO, OFF_G, N_PROJ = (
    0, 768, 1024, 1536, 1792, 1920, 2048, 2176, 2944, 3200, 3328)
G_DT, G_I, G_F = 0, 4, 8

ROW_TILE = 512
A_QBLOCK = 256
C_TILE = 256
REC_CHUNK_PROMPT = 256


def _cparams(sem):
    return pltpu.CompilerParams(dimension_semantics=sem, vmem_limit_bytes=VMEM_LIMIT)


def _rms(x, g):
    return x * lax.rsqrt(jnp.mean(x * x, axis=-1, keepdims=True) + EPS) * g


def _dot(a, b):
    return jnp.dot(a, b, preferred_element_type=F32)


def _dot_nt(a, b):
    return lax.dot_general(a, b, (((1,), (1,)), ((), ())), preferred_element_type=F32)


def _dot_tn(a, b):
    return lax.dot_general(a, b, (((0,), (0,)), ((), ())), preferred_element_type=F32)


def _cumsum_rows(x, tri_bf16):
    hi = x.astype(BF16)
    r1 = x - hi.astype(F32)
    mid = r1.astype(BF16)
    lo = (r1 - mid.astype(F32)).astype(BF16)
    return _dot(tri_bf16, hi) + _dot(tri_bf16, mid) + _dot(tri_bf16, lo)


def _log1p_exp_neg_abs(x):
    return jnp.log1p(jnp.exp(-jnp.abs(x)))


def _full_spec(shape):
    n = len(shape)
    return pl.BlockSpec(shape, lambda *_: (0,) * n)


def _const_spec(shape):
    n = len(shape)
    return pl.BlockSpec(shape, lambda *_: (0,) * n, pipeline_mode=pl.Buffered(1))


def _inproj_kernel(x_ref, g1_ref, w_ref, qg_ref, kvg_ref, wuq_ref, wabs_ref, cos_ref, sin_ref,
                   aq_ref, akv_ref, bz_ref, bxbc_ref, qabs_ref, ckv_ref, kr_ref, kc_ref, vt_ref,
                   mqkv_ref, mo_ref, gates_ref):
    tm = x_ref.shape[0]
    hn = _rms(x_ref[...], g1_ref[...]).astype(BF16)

    def seg(a, b):
        return _dot(hn, w_ref[:, a:b])

    a = seg(OFF_A, OFF_BZ)
    aq_ref[...] = (a[:, :GROUP_W] * HD ** -0.5).astype(BF16)
    akv_ref[...] = a[:, GROUP_W:]
    bz_ref[...] = seg(OFF_BZ, OFF_XBC)
    bxbc_ref[...] = seg(OFF_XBC, OFF_CQ)

    cos = cos_ref[...]
    sin = sin_ref[...]
    cqn = _rms(seg(OFF_CQ, OFF_CKV), qg_ref[...]).astype(BF16)
    qf = _dot(cqn, wuq_ref[...])
    qrope = (qf[:, 256:384] * cos + qf[:, 384:512] * sin) * C_SCALE
    qlat = _dot(qf[:, :256].astype(BF16), wabs_ref[...]) * C_SCALE
    lane_head = lax.broadcasted_iota(jnp.int32, (tm, LANE), 1) // C_ROPE
    for h in range(N_HEADS):
        qabs_ref[h, :, 0:LANE] = qlat[:, h * LANE:(h + 1) * LANE].astype(BF16)
        qabs_ref[h, :, LANE:2 * LANE] = jnp.where(lane_head == h, qrope, 0.0).astype(BF16)

    ckv = _rms(seg(OFF_CKV, OFF_KR), kvg_ref[...])
    kr4 = seg(OFF_KR, OFF_KRR) * cos + seg(OFF_KRR, OFF_MQKV) * sin
    ckv_ref[...] = ckv
    kr_ref[...] = kr4[:, :C_ROPE]
    kc_ref[...] = jnp.concatenate([ckv, kr4], axis=-1).astype(BF16)
    for s in range(tm // C_TILE):
        vt_ref[s] = ckv[s * C_TILE:(s + 1) * C_TILE, :].T.astype(BF16)

    m = seg(OFF_MQKV, OFF_MO)
    mqkv_ref[:, 0:256] = m[:, 0:256].astype(BF16)
    mqkv_ref[:, 256:512] = (m[:, 256:512] * HD ** -0.5).astype(BF16)
    mqkv_ref[:, 512:768] = m[:, 512:768].astype(BF16)
    mo_ref[...] = seg(OFF_MO, OFF_G)
    gates_ref[...] = seg(OFF_G, N_PROJ)


def _inproj(x, g1, w, qg, kvg, wuq, wabs, cos, sin):
    t = x.shape[0]
    tm = ROW_TILE
    row = lambda n: pl.BlockSpec((tm, n), lambda i: (i, 0))
    out_shape = (
        jax.ShapeDtypeStruct((t, 256), BF16),
        jax.ShapeDtypeStruct((t, 512), F32),
        jax.ShapeDtypeStruct((t, 256), F32),
        jax.ShapeDtypeStruct((t, 512), F32),
        jax.ShapeDtypeStruct((N_HEADS, t, 256), BF16),
        jax.ShapeDtypeStruct((t, 128), F32),
        jax.ShapeDtypeStruct((t, C_ROPE), F32),
        jax.ShapeDtypeStruct((t, 256), BF16),
        jax.ShapeDtypeStruct((t // C_TILE, 128, C_TILE), BF16),
        jax.ShapeDtypeStruct((t, 768), BF16),
        jax.ShapeDtypeStruct((t, 256), F32),
        jax.ShapeDtypeStruct((t, LANE), F32),
    )
    out_specs = (
        row(256), row(512), row(256), row(512),
        pl.BlockSpec((N_HEADS, tm, 256), lambda i: (0, i, 0)),
        row(128), row(C_ROPE), row(256),
        pl.BlockSpec((tm // C_TILE, 128, C_TILE), lambda i: (i, 0, 0)),
        row(768), row(256), row(LANE),
    )
    in_specs = [row(D_MODEL), _full_spec((1, D_MODEL)), _const_spec((D_MODEL, N_PROJ)),
                _full_spec((1, 256)), _full_spec((1, 128)), _const_spec((256, 512)), _const_spec((256, 512)),
                row(LANE), row(LANE)]
    return pl.pallas_call(
        _inproj_kernel, grid=(t // tm,), in_specs=in_specs, out_specs=out_specs, out_shape=out_shape,
        compiler_params=_cparams(("parallel",)), name="inproj",
    )(x, g1, w, qg, kvg, wuq, wabs, cos, sin)


def _band_heads(q, kb, vb, bias_ref, valid, o_ref):
    for h in range(N_HEADS):
        sl = slice(h * HD, (h + 1) * HD)
        s = _dot_nt(q[:, sl], kb[:, sl]) + bias_ref[h]
        if valid is not None:
            s = jnp.where(valid, s, NEG)
        m = jnp.max(s, axis=-1, keepdims=True)
        p = jnp.exp(s - m)
        l = jnp.sum(p, axis=-1, keepdims=True)
        o = _dot(p.astype(BF16), vb[:, sl]) / l
        o_ref[:, sl] = o.astype(o_ref.dtype)


def _attn_a_prompt_kernel(q_ref, k0_ref, k1_ref, k2_ref, bias_ref, o_ref):
    i = pl.program_id(1)
    kv = jnp.concatenate([k0_ref[...], k1_ref[...], k2_ref[...]], axis=0)
    kb = kv[:, :GROUP_W].astype(BF16)
    vb = kv[:, GROUP_W:].astype(BF16)
    col = lax.broadcasted_iota(jnp.int32, (A_QBLOCK, 3 * A_QBLOCK), 1)
    valid = col >= (2 - i) * A_QBLOCK
    _band_heads(q_ref[...], kb, vb, bias_ref, valid, o_ref)


def _attn_a_prompt(aq, akv, biasm, batch, seq):
    nb = seq // A_QBLOCK
    qb = A_QBLOCK

    def kspec(back):
        return pl.BlockSpec((qb, 512), lambda b, i: (b * nb + jnp.maximum(i - back, 0), 0))

    return pl.pallas_call(
        _attn_a_prompt_kernel, grid=(batch, nb),
        in_specs=[pl.BlockSpec((qb, 256), lambda b, i: (b * nb + i, 0)), kspec(2), kspec(1), kspec(0),
                  _full_spec(biasm.shape)],
        out_specs=pl.BlockSpec((qb, 256), lambda b, i: (b * nb + i, 0)),
        out_shape=jax.ShapeDtypeStruct((batch * seq, 256), BF16),
        compiler_params=_cparams(("parallel", "parallel")), name="attn_a_prompt",
    )(aq, akv, akv, akv, biasm)


def _attn_a_sample_kernel(q_ref, ck_ref, cv_ref, kvn_ref, bias_ref, o_ref):
    kvn = kvn_ref[...]
    kb = jnp.concatenate([ck_ref[...], kvn[:, :GROUP_W]], axis=0).astype(BF16)
    vb = jnp.concatenate([cv_ref[...], kvn[:, GROUP_W:]], axis=0).astype(BF16)
    _band_heads(q_ref[...], kb, vb, bias_ref, None, o_ref)


def _attn_a_sample(aq, akv, cache_k, cache_v, bias, layer, row0, n_seq, seq):
    blk0 = row0 // seq
    cspec = pl.BlockSpec((None, None, A_SPAN, 256), lambda b: (layer, b, 0, 0))
    return pl.pallas_call(
        _attn_a_sample_kernel, grid=(n_seq,),
        in_specs=[pl.BlockSpec((seq, 256), lambda b: (blk0 + b, 0)), cspec, cspec,
                  pl.BlockSpec((seq, 512), lambda b: (blk0 + b, 0)), _full_spec(bias.shape)],
        out_specs=pl.BlockSpec((seq, 256), lambda b: (b, 0)),
        out_shape=jax.ShapeDtypeStruct((n_seq * seq, 256), BF16),
        compiler_params=_cparams(("parallel",)), name="attn_a_sample",
    )(aq, cache_k, cache_v, akv, bias)


def _ssd_kernel(z_ref, xbc_ref, g_ref, conv0_ref, h0_ref, cw_ref, cb_ref, dtb_ref, alog_ref, dsk_ref, ng_ref,
                y_ref, hout_ref, xpad_ref, h_ref):
    L = z_ref.shape[0]
    c = pl.program_id(1)

    @pl.when(c == 0)
    def _():
        xpad_ref[0:8, :] = conv0_ref[...]
        h_ref[...] = h0_ref[...]

    xpad_ref[8:8 + L, :] = xbc_ref[...]
    conv = cb_ref[...]
    for j in range(B_CONV):
        conv = conv + xpad_ref[5 + j:5 + j + L, :] * cw_ref[j:j + 1, :]
    xpad_ref[0:8, :] = xpad_ref[L:L + 8, :]
    u = conv * jax.nn.sigmoid(conv)

    g = g_ref[...] + dtb_ref[...]
    dt = jnp.maximum(g, 0.0) + _log1p_exp_neg_abs(g)
    da = dt * (-jnp.exp(alog_ref[...]))
    row = lax.broadcasted_iota(jnp.int32, (L, L), 0)
    colm = lax.broadcasted_iota(jnp.int32, (L, L), 1)
    tri = row >= colm
    cs = _cumsum_rows(da, tri.astype(BF16))
    cs_t = cs.T

    ys = []
    gmat = [None] * B_GROUPS
    for h in range(N_HEADS):
        grp = h // (N_HEADS // B_GROUPS)
        xs = u[:, h * HD:(h + 1) * HD]
        bm = u[:, 256 + grp * B_STATE:256 + (grp + 1) * B_STATE].astype(BF16)
        cm = u[:, 384 + grp * B_STATE:384 + (grp + 1) * B_STATE].astype(BF16)
        if gmat[grp] is None:
            gmat[grp] = _dot_nt(cm, bm)
        csc = cs[:, G_DT + h:G_DT + h + 1]
        csr = cs_t[G_DT + h:G_DT + h + 1, :]
        tot = cs[L - 1:L, G_DT + h:G_DT + h + 1]
        dec = jnp.exp(jnp.where(tri, csc - csr, NEG))
        xdt = xs * dt[:, G_DT + h:G_DT + h + 1]
        hprev = h_ref[h]
        y = _dot((gmat[grp] * dec).astype(BF16), xdt.astype(BF16))
        y = y + _dot_nt(cm, hprev.astype(BF16)) * jnp.exp(csc)
        y = y + dsk_ref[0:1, h:h + 1] * xs
        ys.append(y)
        wend = (xdt * jnp.exp(tot - csc)).astype(BF16)
        h_ref[h] = jnp.exp(tot) * hprev + _dot_tn(wend, bm)
    y = jnp.concatenate(ys, axis=-1)
    z = z_ref[...]
    y = y * (z * jax.nn.sigmoid(z))
    y_ref[...] = _rms(y, ng_ref[...]).astype(y_ref.dtype)

    @pl.when(c == pl.num_programs(1) - 1)
    def _():
        hout_ref[...] = h_ref[...]


def _ssd(bz, bxbc, gates, conv0, h0, cw, cb, dtb, alog, dsk, ng, row0, n_seq, seq, L):
    nc = seq // L
    blk0 = row0 // L
    rspec = lambda n: pl.BlockSpec((L, n), lambda b, c: (blk0 + b * nc + c, 0))
    return pl.pallas_call(
        _ssd_kernel, grid=(n_seq, nc),
        in_specs=[rspec(256), rspec(512), rspec(LANE),
                  pl.BlockSpec((None, 8, 512), lambda b, c: (b, 0, 0)),
                  pl.BlockSpec((None, N_HEADS, HD, B_STATE), lambda b, c: (b, 0, 0, 0)),
                  _full_spec((B_CONV, 512)), _full_spec((1, 512)), _full_spec((1, LANE)), _full_spec((1, LANE)),
                  _full_spec((1, LANE)), _full_spec((1, 256))],
        out_specs=(pl.BlockSpec((L, 256), lambda b, c: (b * nc + c, 0)),
                   pl.BlockSpec((None, N_HEADS, HD, B_STATE), lambda b, c: (b, 0, 0, 0))),
        out_shape=(jax.ShapeDtypeStruct((n_seq * seq, 256), BF16),
                   jax.ShapeDtypeStruct((n_seq, N_HEADS, HD, B_STATE), F32)),
        scratch_shapes=[pltpu.VMEM((L + 8, 512), F32), pltpu.VMEM((N_HEADS, HD, B_STATE), F32)],
        compiler_params=_cparams(("parallel", "arbitrary")), name="ssd",
    )(bz, bxbc, gates, conv0, h0, cw, cb, dtb, alog, dsk, ng)


def _mlstm_kernel(qkv_ref, o_ref, g_ref, c0_ref, m0_ref, bias_ref, ng_ref,
                  y_ref, cout_ref, mout_ref, c_ref, m_ref):
    L = qkv_ref.shape[0]
    c = pl.program_id(1)

    @pl.when(c == 0)
    def _():
        c_ref[...] = c0_ref[...]
        m_ref[...] = m0_ref[...]

    g = g_ref[...] + bias_ref[...]
    lf = jnp.minimum(g, 0.0) - _log1p_exp_neg_abs(g)
    row = lax.broadcasted_iota(jnp.int32, (L, L), 0)
    colm = lax.broadcasted_iota(jnp.int32, (L, L), 1)
    tri = row >= colm
    bcum = _cumsum_rows(lf, tri.astype(BF16))
    bcum_t = bcum.T
    g_t = g.T
    unit = (lax.broadcasted_iota(jnp.int32, (L, HD), 1) == 0).astype(BF16)
    ng = ng_ref[...]
    og = o_ref[...]

    for h in range(N_HEADS):
        sl = slice(h * HD, (h + 1) * HD)
        q = qkv_ref[:, h * HD:(h + 1) * HD]
        k = qkv_ref[:, 256 + h * HD:256 + (h + 1) * HD]
        v = qkv_ref[:, 512 + h * HD:512 + (h + 1) * HD]
        vaug = jnp.concatenate([v, unit], axis=-1)
        bc = bcum[:, G_F + h:G_F + h + 1]
        br = bcum_t[G_F + h:G_F + h + 1, :]
        ic = g[:, G_I + h:G_I + h + 1]
        ir = g_t[G_I + h:G_I + h + 1, :]
        tot = bcum[L - 1:L, G_F + h:G_F + h + 1]
        mprev = m_ref[0:1, h:h + 1]

        dm = jnp.where(tri, bc - br + ir, NEG)
        inter = bc + mprev
        mt = jnp.maximum(inter, jnp.max(dm, axis=-1, keepdims=True))
        w_intra = jnp.exp(dm - mt)
        w_inter = jnp.exp(inter - mt)
        caug = c_ref[h]
        s = _dot_nt(q, k) * w_intra
        r = w_inter * _dot(q, caug.astype(BF16)) + _dot(s.astype(BF16), vaug)
        num = r[:, :HD]
        den = r[:, HD:HD + 1]
        hout = num / jnp.maximum(jnp.abs(den), jnp.exp(-mt))

        gend = tot - bc + ic
        mnew = jnp.maximum(tot + mprev, jnp.max(gend, axis=0, keepdims=True))
        kw = (k.astype(F32) * jnp.exp(gend - mnew)).astype(BF16)
        c_ref[h] = jnp.exp(tot + mprev - mnew) * caug + _dot_tn(kw, vaug)
        m_ref[0:1, h:h + 1] = mnew

        yh = _rms(hout, ng[:, sl]) * jax.nn.sigmoid(og[:, sl])
        y_ref[:, sl] = yh.astype(y_ref.dtype)

    @pl.when(c == pl.num_programs(1) - 1)
    def _():
        cout_ref[...] = c_ref[...]
        mout_ref[...] = m_ref[...]


def _mlstm(mqkv, mo, gates, c0, m0, bias, ng, row0, n_seq, seq, L):
    nc = seq // L
    blk0 = row0 // L
    rspec = lambda n: pl.BlockSpec((L, n), lambda b, c: (blk0 + b * nc + c, 0))
    cspec = pl.BlockSpec((None, N_HEADS, HD, LANE), lambda b, c: (b, 0, 0, 0))
    mspec = pl.BlockSpec((None, 1, LANE), lambda b, c: (b, 0, 0))
    return pl.pallas_call(
        _mlstm_kernel, grid=(n_seq, nc),
        in_specs=[rspec(768), rspec(256), rspec(LANE), cspec, mspec, _full_spec((1, LANE)), _full_spec((1, 256))],
        out_specs=(pl.BlockSpec((L, 256), lambda b, c: (b * nc + c, 0)), cspec, mspec),
        out_shape=(jax.ShapeDtypeStruct((n_seq * seq, 256), BF16),
                   jax.ShapeDtypeStruct((n_seq, N_HEADS, HD, LANE), F32),
                   jax.ShapeDtypeStruct((n_seq, 1, LANE), F32)),
        scratch_shapes=[pltpu.VMEM((N_HEADS, HD, LANE), F32), pltpu.VMEM((1, LANE), F32)],
        compiler_params=_cparams(("parallel", "arbitrary")), name="mlstm",
    )(mqkv, mo, gates, c0, m0, bias, ng)


def _mla_prompt_kernel(q_ref, kc_ref, vt_ref, wuvt_ref, o_ref, acc_ref, m_ref, l_ref):
    tq = q_ref.shape[1]
    tk = C_TILE
    i = pl.program_id(1)
    qs = q_ref[...].reshape(N_HEADS * tq, 2 * LANE)
    m_ref[...] = jnp.full_like(m_ref, NEG)
    l_ref[...] = jnp.zeros_like(l_ref)
    acc_ref[...] = jnp.zeros_like(acc_ref)

    def tile(j, mask):
        start = pl.multiple_of(j * tk, tk)
        s = _dot_nt(kc_ref[pl.ds(start, tk), :], qs)
        if mask is not None:
            s = jnp.where(mask, s, NEG)
        m_old = m_ref[...]
        m_new = jnp.maximum(m_old, jnp.max(s, axis=0, keepdims=True))
        alpha = jnp.exp(m_old - m_new)
        p = jnp.exp(s - m_new)
        l_ref[...] = alpha * l_ref[...] + jnp.sum(p, axis=0, keepdims=True)
        acc_ref[...] = alpha * acc_ref[...] + _dot(vt_ref[j], p.astype(BF16))
        m_ref[...] = m_new

    def body(j, carry):
        tile(j, None)
        return carry

    lax.fori_loop(0, i, body, 0)
    krow = lax.broadcasted_iota(jnp.int32, (tk, N_HEADS * tq), 0) // CHUNK
    qcol = (lax.broadcasted_iota(jnp.int32, (tk, N_HEADS * tq), 1) % tq) // CHUNK
    tile(i, krow <= qcol)

    o = (acc_ref[...] / l_ref[...]).astype(BF16)
    yt = jnp.concatenate([_dot(wuvt_ref[h], o[:, h * tq:(h + 1) * tq]) for h in range(N_HEADS)], axis=0)
    o_ref[...] = yt.T.astype(o_ref.dtype)


def _mla_prompt(qabs, kc, vt, wuvt, batch, seq):
    tq = C_TILE
    nq = seq // tq
    return pl.pallas_call(
        _mla_prompt_kernel, grid=(batch, nq),
        in_specs=[pl.BlockSpec((N_HEADS, tq, 256), lambda b, i: (0, b * nq + i, 0)),
                  pl.BlockSpec((seq, 256), lambda b, i: (b, 0)),
                  pl.BlockSpec((seq // C_TILE, 128, C_TILE), lambda b, i: (b, 0, 0)),
                  _full_spec((N_HEADS, HD, 128))],
        out_specs=pl.BlockSpec((tq, 256), lambda b, i: (b * nq + i, 0)),
        out_shape=jax.ShapeDtypeStruct((batch * seq, 256), BF16),
        scratch_shapes=[pltpu.VMEM((128, N_HEADS * tq), F32), pltpu.VMEM((1, N_HEADS * tq), F32),
                        pltpu.VMEM((1, N_HEADS * tq), F32)],
        compiler_params=_cparams(("parallel", "arbitrary")), name="mla_prompt",
    )(qabs, kc, vt, wuvt)


def _mla_sample_kernel(q_ref, kcache_ref, knew_ref, wuv_ref, o_ref):
    seq = q_ref.shape[1]
    qs = q_ref[...].reshape(N_HEADS * seq, 2 * LANE)
    kall = jnp.concatenate([kcache_ref[...], knew_ref[...]], axis=0)
    s = _dot_nt(qs, kall)
    m = jnp.max(s, axis=-1, keepdims=True)
    p = jnp.exp(s - m)
    l = jnp.sum(p, axis=-1, keepdims=True)
    o = (_dot(p.astype(BF16), kall[:, :C_KV_LORA]) / l).astype(BF16)
    for h in range(N_HEADS):
        o_ref[:, h * HD:(h + 1) * HD] = _dot(o[h * seq:(h + 1) * seq], wuv_ref[h]).astype(o_ref.dtype)


def _mla_sample(qabs, kc, kcache, wuv, layer, row0, n_seq, seq):
    blk0 = row0 // seq
    past = kcache.shape[2]
    return pl.pallas_call(
        _mla_sample_kernel, grid=(n_seq,),
        in_specs=[pl.BlockSpec((N_HEADS, seq, 256), lambda b: (0, blk0 + b, 0)),
                  pl.BlockSpec((None, None, past, 256), lambda b: (layer, b, 0, 0)),
                  pl.BlockSpec((seq, 256), lambda b: (blk0 + b, 0)),
                  _full_spec((N_HEADS, 128, HD))],
        out_specs=pl.BlockSpec((seq, 256), lambda b: (b, 0)),
        out_shape=jax.ShapeDtypeStruct((n_seq * seq, 256), BF16),
        compiler_params=_cparams(("parallel",)), name="mla_sample",
    )(qabs, kcache, kc, wuv)


def _outmlp_kernel(x_ref, ya_ref, yb_ref, yc_ref, yd_ref, wout_ref, g2_ref, w1_ref, w2_ref, fg_ref, o_ref, *, final):
    y = jnp.concatenate([ya_ref[...], yb_ref[...], yc_ref[...], yd_ref[...]], axis=-1)
    x1 = x_ref[...] + _dot(y, wout_ref[...])
    hm = _rms(x1, g2_ref[...]).astype(BF16)
    acc = x1
    for c in range(D_FF // D_MODEL):
        sl = slice(c * D_MODEL, (c + 1) * D_MODEL)
        hc = jnp.square(jnp.maximum(_dot(hm, w1_ref[:, sl]), 0.0)).astype(BF16)
        acc = acc + _dot(hc, w2_ref[sl, :])
    if final:
        acc = _rms(acc, fg_ref[...])
    o_ref[...] = acc


def _outmlp(x, ya, yb, yc, yd, wout, g2, w1, w2, fg, final):
    t = x.shape[0]
    tm = ROW_TILE
    row = lambda n: pl.BlockSpec((tm, n), lambda i: (i, 0))
    return pl.pallas_call(
        functools.partial(_outmlp_kernel, final=final), grid=(t // tm,),
        in_specs=[row(D_MODEL), row(256), row(256), row(256), row(256), _const_spec((D_MODEL, D_MODEL)),
                  _full_spec((1, D_MODEL)), _const_spec((D_MODEL, D_FF)), _const_spec((D_FF, D_MODEL)),
                  _full_spec((1, D_MODEL))],
        out_specs=row(D_MODEL), out_shape=jax.ShapeDtypeStruct((t, D_MODEL), F32),
        input_output_aliases={0: 0},
        compiler_params=_cparams(("parallel",)), name="outmlp",
    )(x, ya, yb, yc, yd, wout, g2, w1, w2, fg)


def _rot_cols(w):
    half = C_ROPE // 2
    return jnp.concatenate([-w[..., half:], w[..., :half]], axis=-1)


def _pad_lanes(v, n=LANE, at=0):
    out = jnp.zeros(v.shape[:-1] + (n,), v.dtype)
    return out.at[..., at:at + v.shape[-1]].set(v)


def _prep_w_in(w):
    sizes = (256, 256, 256, 256, 512, 4, 256, 128, 32, 256, 256, 256, 4, 4, 256)
    cuts = [0]
    for s in sizes:
        cuts.append(cuts[-1] + s)
    (a_q, a_k, a_v, b_z, b_xbc, b_dt, c_q, c_kv, c_kr, m_q, m_k, m_v, m_i, m_f, m_o) = (
        w[..., cuts[n]:cuts[n + 1]] for n in range(len(sizes)))
    kr4 = jnp.concatenate([c_kr] * N_HEADS, axis=-1)
    krr4 = jnp.concatenate([_rot_cols(c_kr)] * N_HEADS, axis=-1)
    gates = _pad_lanes(jnp.concatenate([b_dt, m_i, m_f], axis=-1))
    out = jnp.concatenate([a_q, a_k, a_v, b_z, b_xbc, c_q, c_kv, kr4, krr4, m_q, m_k, m_v, m_o, gates], axis=-1)
    assert out.shape[-1] == N_PROJ
    return out.astype(BF16)


def _prep_w_uq(w):
    d = w.shape[0]
    w4 = w.reshape(d, 256, N_HEADS, C_NOPE + C_ROPE)
    nope = w4[..., :C_NOPE].reshape(d, 256, N_HEADS * C_NOPE)
    rope = w4[..., C_NOPE:]
    return jnp.concatenate([nope, rope.reshape(d, 256, N_HEADS * C_ROPE),
                            _rot_cols(rope).reshape(d, 256, N_HEADS * C_ROPE)], axis=-1).astype(BF16)


def _prep_w_ukv(w):
    d = w.shape[0]
    w4 = w.reshape(d, C_KV_LORA, N_HEADS, C_NOPE + HD)
    w_uk = w4[..., :C_NOPE]
    w_uv = w4[..., C_NOPE:]
    wabs = jnp.zeros((d, N_HEADS * C_NOPE, N_HEADS * C_KV_LORA), w.dtype)
    for h in range(N_HEADS):
        wabs = wabs.at[:, h * C_NOPE:(h + 1) * C_NOPE, h * C_KV_LORA:(h + 1) * C_KV_LORA].set(
            jnp.swapaxes(w_uk[:, :, h, :], 1, 2))
    wuv = jnp.transpose(w_uv, (0, 2, 1, 3))
    wuvt = jnp.transpose(w_uv, (0, 2, 3, 1))
    return wabs.astype(BF16), wuv.astype(BF16), wuvt.astype(BF16)


def _rope_tables(pos):
    half = C_ROPE // 2
    inv = jnp.exp(-math.log(ROPE_BASE) * jnp.arange(half, dtype=F32) / half)
    ang = pos.astype(F32)[:, None] * inv[None, :]
    reps = LANE // half
    return jnp.tile(jnp.cos(ang), (1, reps)), jnp.tile(jnp.sin(ang), (1, reps))


def _band_bias(table, q_rows):
    r = jnp.arange(q_rows)[:, None]
    c = jnp.arange(q_rows + A_SPAN)[None, :]
    s = c - (r // CHUNK) * CHUNK
    rel = jnp.clip((r % CHUNK) - (s - A_SPAN), -A_REL_CLIP, A_REL_CLIP) + A_REL_CLIP
    inband = (s >= 0) & (s < A_SPAN + CHUNK)
    return jnp.where(inband[None], table[:, rel].astype(F32), NEG)


def kernel(x_prompt, x_sample, cache_attn_k, cache_attn_v, state_ssm_conv, state_ssm, cache_mla_ckv, cache_mla_kr,
           state_mlstm_c, state_mlstm_n, state_mlstm_m, norm1_g, w_in, attn_rel_bias, ssm_conv_w, ssm_conv_b,
           ssm_dt_bias, ssm_a_log, ssm_d, ssm_norm_g, mla_q_norm_g, mla_w_uq, mla_kv_norm_g, mla_w_ukv,
           mlstm_b_i, mlstm_b_f, mlstm_norm_g, w_out, norm2_g, mlp_w1, mlp_w2, final_g):
    batch, seq, _ = x_prompt.shape
    dbatch, dseq, _ = x_sample.shape
    depth = w_in.shape[0]
    past = cache_mla_ckv.shape[2]
    tp, ts = batch * seq, dbatch * dseq
    lp = min(REC_CHUNK_PROMPT, seq)
    assert dseq == CHUNK and past % CHUNK == 0 and cache_attn_k.shape[2] == A_SPAN
    assert seq % C_TILE == 0 and tp % ROW_TILE == 0 and ts % ROW_TILE == 0

    w_in_r = _prep_w_in(w_in)
    wuq_r = _prep_w_uq(mla_w_uq)
    wabs, wuv, wuvt = _prep_w_ukv(mla_w_ukv)
    w_out_b, w1_b, w2_b = w_out.astype(BF16), mlp_w1.astype(BF16), mlp_w2.astype(BF16)
    pos = jnp.concatenate([jnp.tile(jnp.arange(seq), batch), jnp.tile(past + jnp.arange(dseq), dbatch)])
    cos, sin = _rope_tables(pos)
    dtb = _pad_lanes(ssm_dt_bias, at=G_DT)[:, None, :]
    alog = _pad_lanes(ssm_a_log, at=G_DT)[:, None, :]
    dsk = _pad_lanes(ssm_d)[:, None, :]
    gate_bias = (_pad_lanes(mlstm_b_i, at=G_I) + _pad_lanes(mlstm_b_f, at=G_F))[:, None, :]
    cache_k = cache_attn_k.reshape(depth, dbatch, A_SPAN, 256)
    cache_v = cache_attn_v.reshape(depth, dbatch, A_SPAN, 256)
    kcache = jnp.concatenate([cache_mla_ckv] + [cache_mla_kr] * N_HEADS, axis=-1).astype(BF16)
    conv0_s = jnp.pad(state_ssm_conv, ((0, 0), (0, 0), (8 - (B_CONV - 1), 0), (0, 0)))
    caug0_s = jnp.concatenate([state_mlstm_c, state_mlstm_n[..., None],
                               jnp.zeros(state_mlstm_c.shape[:-1] + (LANE - HD - 1,), F32)], axis=-1)
    m0_s = _pad_lanes(state_mlstm_m)[:, :, None, :]
    conv0_p = jnp.zeros((batch, 8, B_CONV_DIM), F32)
    h0_p = jnp.zeros((batch, N_HEADS, HD, B_STATE), F32)
    caug0_p = jnp.zeros((batch, N_HEADS, HD, LANE), F32)
    m0_p = jnp.zeros((batch, 1, LANE), F32)

    x = jnp.concatenate([x_prompt.reshape(tp, D_MODEL), x_sample.reshape(ts, D_MODEL)], axis=0)
    outs = [[] for _ in range(18)]
    fg = final_g[None, :]
    for l in range(depth):
        (aq, akv, bz, bxbc, qabs, ckv, kr, kc, vt, mqkv, mo, gates) = _inproj(
            x, norm1_g[l][None], w_in_r[l], mla_q_norm_g[l][None], mla_kv_norm_g[l][None], wuq_r[l], wabs[l],
            cos, sin)

        ya_p = _attn_a_prompt(aq, akv, _band_bias(attn_rel_bias[l], A_QBLOCK), batch, seq)
        ya_s = _attn_a_sample(aq, akv, cache_k, cache_v, _band_bias(attn_rel_bias[l], CHUNK), l, tp, dbatch, dseq)

        ssd_w = (ssm_conv_w[l], ssm_conv_b[l][None], dtb[l], alog[l], dsk[l], ssm_norm_g[l][None])
        yb_p, h_p = _ssd(bz, bxbc, gates, conv0_p, h0_p, *ssd_w, 0, batch, seq, lp)
        yb_s, h_s = _ssd(bz, bxbc, gates, conv0_s[l], state_ssm[l], *ssd_w, tp, dbatch, dseq, dseq)

        yc_p = _mla_prompt(qabs, kc, vt, wuvt[l], batch, seq)
        yc_s = _mla_sample(qabs, kc, kcache, wuv[l], l, tp, dbatch, dseq)

        ml_w = (gate_bias[l], mlstm_norm_g[l][None])
        yd_p, c_p, m_p = _mlstm(mqkv, mo, gates, caug0_p, m0_p, *ml_w, 0, batch, seq, lp)
        yd_s, c_s, m_s = _mlstm(mqkv, mo, gates, caug0_s[l], m0_s[l], *ml_w, tp, dbatch, dseq, dseq)

        cat = lambda a, b: jnp.concatenate([a, b], axis=0)
        x = _outmlp(x, cat(ya_p, ya_s), cat(yb_p, yb_s), cat(yc_p, yc_s), cat(yd_p, yd_s), w_out_b[l],
                    norm2_g[l][None], w1_b[l], w2_b[l], fg, l == depth - 1)

        akv_p = akv[:tp].reshape(batch, seq, 2, N_HEADS, HD)[:, seq - min(A_SPAN, seq):]
        akv_s = akv[tp:].reshape(dbatch, dseq, 2, N_HEADS, HD)
        new = (akv_p[:, :, 0], akv_p[:, :, 1],
               bxbc[:tp].reshape(batch, seq, B_CONV_DIM)[:, seq - (B_CONV - 1):], h_p,
               ckv[:tp].reshape(batch, seq, C_KV_LORA), kr[:tp].reshape(batch, seq, C_ROPE),
               c_p[..., :HD], c_p[..., HD], m_p[:, 0, :N_HEADS],
               jnp.concatenate([cache_attn_k[l][:, dseq:], akv_s[:, :, 0]], axis=1),
               jnp.concatenate([cache_attn_v[l][:, dseq:], akv_s[:, :, 1]], axis=1),
               bxbc[tp:].reshape(dbatch, dseq, B_CONV_DIM)[:, dseq - (B_CONV - 1):], h_s,
               ckv[tp:].reshape(dbatch, dseq, C_KV_LORA), kr[tp:].reshape(dbatch, dseq, C_ROPE),
               c_s[..., :HD], c_s[..., HD], m_s[:, 0, :N_HEADS])
        for dst, v in zip(outs, new):
            dst.append(v)

    y_prompt = x[:tp].reshape(batch, seq, D_MODEL)
    y_sample = x[tp:].reshape(dbatch, dseq, D_MODEL)
    return (y_prompt, y_sample) + tuple(jnp.stack(v) for v in outs)
```

```python
import functools
import math

import jax
import jax.numpy as jnp
import numpy as np
from jax import lax
from jax.experimental import pallas as pl
from jax.experimental.pallas import tpu as pltpu

F32 = jnp.float32
BF16 = jnp.bfloat16
EPS = 1e-6
NEG = -1e30
ROPE_BASE = 10000.0

CHUNK = 64
D_MODEL = 1024
GROUP_W = 256
D_FF = 4096
N_HEADS = 4
HD = 64
A_BAND_CHUNKS = 8
A_SPAN = A_BAND_CHUNKS * CHUNK
A_REL_CLIP = 128
B_GROUPS = 2
B_STATE = 64
B_CONV = 4
B_CONV_DIM = 512
C_NOPE = 64
C_ROPE = 32
C_KV_LORA = 128
C_SCALE = (C_NOPE + C_ROPE) ** -0.5
C_QSCALE = C_SCALE * math.log2(math.e)
VT_ROWS = 144
LANE = 128
VMEM_LIMIT = 56 * 1024 * 1024

OFF_A, OFF_BZ, OFF_XBC, OFF_CQ, OFF_CKV, OFF_KR, OFF_KRR, OFF_MQKV, OFF_MO, OFF_G, N_PROJ = (
    0, 768, 1024, 1536, 1792, 1920, 2048, 2176, 2944, 3200, 3328)
G_DT, G_I, G_F = 0, 4, 8

ROW_TILE = 512
A_QBLOCK = 256
C_TILE = 256
REC_CHUNK_PROMPT = 256


def _cparams(sem):
    return pltpu.CompilerParams(dimension_semantics=sem, vmem_limit_bytes=VMEM_LIMIT)


def _rms(x, g):
    return x * lax.rsqrt(jnp.mean(x * x, axis=-1, keepdims=True) + EPS) * g


def _dot(a, b):
    return jnp.dot(a, b, preferred_element_type=F32)


def _dot_nt(a, b):
    return lax.dot_general(a, b, (((1,), (1,)), ((), ())), preferred_element_type=F32)


def _dot_tn(a, b):
    return lax.dot_general(a, b, (((0,), (0,)), ((), ())), preferred_element_type=F32)


def _cumsum_rows(x, tri_bf16):
    hi = x.astype(BF16)
    r1 = x - hi.astype(F32)
    mid = r1.astype(BF16)
    lo = (r1 - mid.astype(F32)).astype(BF16)
    return _dot(tri_bf16, hi) + _dot(tri_bf16, mid) + _dot(tri_bf16, lo)


def _log1p_exp_neg_abs(x):
    return jnp.log1p(jnp.exp(-jnp.abs(x)))


def _full_spec(shape):
    n = len(shape)
    return pl.BlockSpec(shape, lambda *_: (0,) * n)


_DST_SPEC = pl.BlockSpec(memory_space=pl.ANY)


def _const_spec(shape):
    n = len(shape)
    return pl.BlockSpec(shape, lambda *_: (0,) * n, pipeline_mode=pl.Buffered(1))


def _inproj_kernel(x_ref, g1_ref, w_ref, qg_ref, kvg_ref, wuq_ref, wabs_ref, cos_ref, sin_ref,
                   aq_ref, akv_ref, bz_ref, bxbc_ref, qabs_ref, qt_ref, ckv_ref, kr_ref, kc_ref, vt_ref,
                   mqkv_ref, mo_ref, gates_ref):
    tm = x_ref.shape[0]
    hn = _rms(x_ref[...], g1_ref[...]).astype(BF16)

    def seg(a, b):
        return _dot(hn, w_ref[:, a:b])

    a = seg(OFF_A, OFF_BZ)
    aq_ref[...] = (a[:, :GROUP_W] * HD ** -0.5).astype(BF16)
    akv_ref[...] = a[:, GROUP_W:]
    bz_ref[...] = seg(OFF_BZ, OFF_XBC)
    bxbc_ref[...] = seg(OFF_XBC, OFF_CQ)

    cos = cos_ref[...]
    sin = sin_ref[...]
    cqn = _rms(seg(OFF_CQ, OFF_CKV), qg_ref[...]).astype(BF16)
    qf = _dot(cqn, wuq_ref[...])
    qrope = (qf[:, 256:384] * cos + qf[:, 384:512] * sin) * C_QSCALE
    qlat = _dot(qf[:, :256].astype(BF16), wabs_ref[...]) * C_QSCALE
    lane_head = lax.broadcasted_iota(jnp.int32, (tm, LANE), 1) // C_ROPE
    for h in range(N_HEADS):
        qh = jnp.concatenate([qlat[:, h * LANE:(h + 1) * LANE], jnp.where(lane_head == h, qrope, 0.0)], axis=-1)
        qabs_ref[h] = qh.astype(BF16)
        qt_ref[h] = qh.T.astype(BF16)

    ckv = _rms(seg(OFF_CKV, OFF_KR), kvg_ref[...])
    kr4 = seg(OFF_KR, OFF_KRR) * cos + seg(OFF_KRR, OFF_MQKV) * sin
    ckv_ref[...] = ckv
    kr_ref[...] = kr4[:, :C_ROPE]
    kc_ref[...] = jnp.concatenate([ckv, kr4], axis=-1).astype(BF16)
    ones_rows = (lax.broadcasted_iota(jnp.int32, (VT_ROWS - C_KV_LORA, C_TILE), 0) == 0).astype(BF16)
    for s in range(tm // C_TILE):
        vt_ref[s, 0:C_KV_LORA, :] = ckv[s * C_TILE:(s + 1) * C_TILE, :].T.astype(BF16)
        vt_ref[s, C_KV_LORA:VT_ROWS, :] = ones_rows

    m = seg(OFF_MQKV, OFF_MO)
    mqkv_ref[:, 0:256] = m[:, 0:256].astype(BF16)
    mqkv_ref[:, 256:512] = (m[:, 256:512] * HD ** -0.5).astype(BF16)
    mqkv_ref[:, 512:768] = m[:, 512:768].astype(BF16)
    mo_ref[...] = seg(OFF_MO, OFF_G)
    gates_ref[...] = seg(OFF_G, N_PROJ)


def _inproj(x, g1, w, qg, kvg, wuq, wabs, cos, sin):
    t = x.shape[0]
    tm = ROW_TILE
    row = lambda n: pl.BlockSpec((tm, n), lambda i: (i, 0))
    out_shape = (
        jax.ShapeDtypeStruct((t, 256), BF16),
        jax.ShapeDtypeStruct((t, 512), F32),
        jax.ShapeDtypeStruct((t, 256), F32),
        jax.ShapeDtypeStruct((t, 512), F32),
        jax.ShapeDtypeStruct((N_HEADS, t, 256), BF16),
        jax.ShapeDtypeStruct((N_HEADS, 256, t), BF16),
        jax.ShapeDtypeStruct((t, 128), F32),
        jax.ShapeDtypeStruct((t, C_ROPE), F32),
        jax.ShapeDtypeStruct((t, 256), BF16),
        jax.ShapeDtypeStruct((t // C_TILE, VT_ROWS, C_TILE), BF16),
        jax.ShapeDtypeStruct((t, 768), BF16),
        jax.ShapeDtypeStruct((t, 256), F32),
        jax.ShapeDtypeStruct((t, LANE), F32),
    )
    out_specs = (
        row(256), row(512), row(256), row(512),
        pl.BlockSpec((N_HEADS, tm, 256), lambda i: (0, i, 0)),
        pl.BlockSpec((N_HEADS, 256, tm), lambda i: (0, 0, i)),
        row(128), row(C_ROPE), row(256),
        pl.BlockSpec((tm // C_TILE, VT_ROWS, C_TILE), lambda i: (i, 0, 0)),
        row(768), row(256), row(LANE),
    )
    in_specs = [row(D_MODEL), _full_spec((1, D_MODEL)), _const_spec((D_MODEL, N_PROJ)),
                _full_spec((1, 256)), _full_spec((1, 128)), _const_spec((256, 512)), _const_spec((256, 512)),
                row(LANE), row(LANE)]
    return pl.pallas_call(
        _inproj_kernel, grid=(t // tm,), in_specs=in_specs, out_specs=out_specs, out_shape=out_shape,
        compiler_params=_cparams(("parallel",)), name="inproj",
    )(x, g1, w, qg, kvg, wuq, wabs, cos, sin)


def _band_heads(q, kb, vb, bias_ref, valid, o_ref):
    for h in range(N_HEADS):
        sl = slice(h * HD, (h + 1) * HD)
        s = _dot_nt(q[:, sl], kb[:, sl]) + bias_ref[h]
        if valid is not None:
            s = jnp.where(valid, s, NEG)
        m = jnp.max(s, axis=-1, keepdims=True)
        p = jnp.exp(s - m)
        l = jnp.sum(p, axis=-1, keepdims=True)
        o = _dot(p.astype(BF16), vb[:, sl]) / l
        o_ref[:, sl] = o.astype(o_ref.dtype)


def _attn_a_prompt_kernel(dst_ref, q_ref, k0_ref, k1_ref, k2_ref, bias_ref, o_ref):
    i = pl.program_id(1)
    kv = jnp.concatenate([k0_ref[...], k1_ref[...], k2_ref[...]], axis=0)
    kb = kv[:, :GROUP_W].astype(BF16)
    vb = kv[:, GROUP_W:].astype(BF16)
    col = lax.broadcasted_iota(jnp.int32, (A_QBLOCK, 3 * A_QBLOCK), 1)
    valid = col >= (2 - i) * A_QBLOCK
    _band_heads(q_ref[...], kb, vb, bias_ref, valid, o_ref)


def _attn_a_prompt(dst, aq, akv, biasm, batch, seq):
    nb = seq // A_QBLOCK
    qb = A_QBLOCK

    def kspec(back):
        return pl.BlockSpec((qb, 512), lambda b, i: (b * nb + jnp.maximum(i - back, 0), 0))

    return pl.pallas_call(
        _attn_a_prompt_kernel, grid=(batch, nb),
        in_specs=[_DST_SPEC, pl.BlockSpec((qb, 256), lambda b, i: (b * nb + i, 0)), kspec(2), kspec(1), kspec(0),
                  _full_spec(biasm.shape)],
        out_specs=pl.BlockSpec((qb, 256), lambda b, i: (b * nb + i, 0)),
        out_shape=jax.ShapeDtypeStruct(dst.shape, dst.dtype), input_output_aliases={0: 0},
        compiler_params=_cparams(("parallel", "parallel")), name="attn_a_prompt",
    )(dst, aq, akv, akv, akv, biasm)


def _attn_a_sample_kernel(dst_ref, q_ref, ck_ref, cv_ref, kvn_ref, bias_ref, o_ref):
    kvn = kvn_ref[...]
    kb = jnp.concatenate([ck_ref[...], kvn[:, :GROUP_W]], axis=0).astype(BF16)
    vb = jnp.concatenate([cv_ref[...], kvn[:, GROUP_W:]], axis=0).astype(BF16)
    _band_heads(q_ref[...], kb, vb, bias_ref, None, o_ref)


def _attn_a_sample(dst, aq, akv, cache_k, cache_v, bias, layer, row0, n_seq, seq):
    blk0 = row0 // seq
    cspec = pl.BlockSpec((None, None, A_SPAN, 256), lambda b: (layer, b, 0, 0))
    return pl.pallas_call(
        _attn_a_sample_kernel, grid=(n_seq,),
        in_specs=[_DST_SPEC, pl.BlockSpec((seq, 256), lambda b: (blk0 + b, 0)), cspec, cspec,
                  pl.BlockSpec((seq, 512), lambda b: (blk0 + b, 0)), _full_spec(bias.shape)],
        out_specs=pl.BlockSpec((seq, 256), lambda b: (blk0 + b, 0)),
        out_shape=jax.ShapeDtypeStruct(dst.shape, dst.dtype), input_output_aliases={0: 0},
        compiler_params=_cparams(("parallel",)), name="attn_a_sample",
    )(dst, aq, cache_k, cache_v, akv, bias)


def _ssd_kernel(dst_ref, z_ref, xbc_ref, g_ref, conv0_ref, h0_ref, cw_ref, cb_ref, dtb_ref, alog_ref, dsk_ref, ng_ref,
                y_ref, hout_ref, xpad_ref, h_ref):
    L = z_ref.shape[0]
    c = pl.program_id(1)

    @pl.when(c == 0)
    def _():
        xpad_ref[0:8, :] = conv0_ref[...]
        h_ref[...] = h0_ref[...]

    xpad_ref[8:8 + L, :] = xbc_ref[...]
    conv = cb_ref[...]
    for j in range(B_CONV):
        conv = conv + xpad_ref[5 + j:5 + j + L, :] * cw_ref[j:j + 1, :]
    xpad_ref[0:8, :] = xpad_ref[L:L + 8, :]
    u = conv * jax.nn.sigmoid(conv)

    g = g_ref[...] + dtb_ref[...]
    dt = jnp.maximum(g, 0.0) + _log1p_exp_neg_abs(g)
    da = dt * (-jnp.exp(alog_ref[...]))
    row = lax.broadcasted_iota(jnp.int32, (L, L), 0)
    colm = lax.broadcasted_iota(jnp.int32, (L, L), 1)
    tri = row >= colm
    cs = _cumsum_rows(da, tri.astype(BF16))
    cs_t = cs.T

    ys = []
    gmat = [None] * B_GROUPS
    for h in range(N_HEADS):
        grp = h // (N_HEADS // B_GROUPS)
        xs = u[:, h * HD:(h + 1) * HD]
        bm = u[:, 256 + grp * B_STATE:256 + (grp + 1) * B_STATE].astype(BF16)
        cm = u[:, 384 + grp * B_STATE:384 + (grp + 1) * B_STATE].astype(BF16)
        if gmat[grp] is None:
            gmat[grp] = _dot_nt(cm, bm)
        csc = cs[:, G_DT + h:G_DT + h + 1]
        csr = cs_t[G_DT + h:G_DT + h + 1, :]
        tot = cs[L - 1:L, G_DT + h:G_DT + h + 1]
        dec = jnp.exp(jnp.where(tri, csc - csr, NEG))
        xdt = xs * dt[:, G_DT + h:G_DT + h + 1]
        hprev = h_ref[h]
        y = _dot((gmat[grp] * dec).astype(BF16), xdt.astype(BF16))
        y = y + _dot_nt(cm, hprev.astype(BF16)) * jnp.exp(csc)
        y = y + dsk_ref[0:1, h:h + 1] * xs
        ys.append(y)
        wend = (xdt * jnp.exp(tot - csc)).astype(BF16)
        h_ref[h] = jnp.exp(tot) * hprev + _dot_tn(wend, bm)
    y = jnp.concatenate(ys, axis=-1)
    z = z_ref[...]
    y = y * (z * jax.nn.sigmoid(z))
    y_ref[...] = _rms(y, ng_ref[...]).astype(y_ref.dtype)

    @pl.when(c == pl.num_programs(1) - 1)
    def _():
        hout_ref[...] = h_ref[...]


def _ssd(dst, bz, bxbc, gates, conv0, h0, cw, cb, dtb, alog, dsk, ng, row0, n_seq, seq, L):
    nc = seq // L
    blk0 = row0 // L
    rspec = lambda n: pl.BlockSpec((L, n), lambda b, c: (blk0 + b * nc + c, 0))
    return pl.pallas_call(
        _ssd_kernel, grid=(n_seq, nc),
        in_specs=[_DST_SPEC, rspec(256), rspec(512), rspec(LANE),
                  pl.BlockSpec((None, 8, 512), lambda b, c: (b, 0, 0)),
                  pl.BlockSpec((None, N_HEADS, HD, B_STATE), lambda b, c: (b, 0, 0, 0)),
                  _full_spec((B_CONV, 512)), _full_spec((1, 512)), _full_spec((1, LANE)), _full_spec((1, LANE)),
                  _full_spec((1, LANE)), _full_spec((1, 256))],
        out_specs=(rspec(256), pl.BlockSpec((None, N_HEADS, HD, B_STATE), lambda b, c: (b, 0, 0, 0))),
        input_output_aliases={0: 0},
        out_shape=(jax.ShapeDtypeStruct(dst.shape, dst.dtype),
                   jax.ShapeDtypeStruct((n_seq, N_HEADS, HD, B_STATE), F32)),
        scratch_shapes=[pltpu.VMEM((L + 8, 512), F32), pltpu.VMEM((N_HEADS, HD, B_STATE), F32)],
        compiler_params=_cparams(("parallel", "arbitrary")), name="ssd",
    )(dst, bz, bxbc, gates, conv0, h0, cw, cb, dtb, alog, dsk, ng)


def _mlstm_kernel(dst_ref, qkv_ref, o_ref, g_ref, c0_ref, m0_ref, bias_ref, ng_ref,
                  y_ref, cout_ref, mout_ref, c_ref, m_ref):
    L = qkv_ref.shape[0]
    c = pl.program_id(1)

    @pl.when(c == 0)
    def _():
        c_ref[...] = c0_ref[...]
        m_ref[...] = m0_ref[...]

    g = g_ref[...] + bias_ref[...]
    lf = jnp.minimum(g, 0.0) - _log1p_exp_neg_abs(g)
    row = lax.broadcasted_iota(jnp.int32, (L, L), 0)
    colm = lax.broadcasted_iota(jnp.int32, (L, L), 1)
    tri = row >= colm
    bcum = _cumsum_rows(lf, tri.astype(BF16))
    bcum_t = bcum.T
    g_t = g.T
    unit = (lax.broadcasted_iota(jnp.int32, (L, HD), 1) == 0).astype(BF16)
    ng = ng_ref[...]
    og = o_ref[...]

    for h in range(N_HEADS):
        sl = slice(h * HD, (h + 1) * HD)
        q = qkv_ref[:, h * HD:(h + 1) * HD]
        k = qkv_ref[:, 256 + h * HD:256 + (h + 1) * HD]
        v = qkv_ref[:, 512 + h * HD:512 + (h + 1) * HD]
        vaug = jnp.concatenate([v, unit], axis=-1)
        bc = bcum[:, G_F + h:G_F + h + 1]
        br = bcum_t[G_F + h:G_F + h + 1, :]
        ic = g[:, G_I + h:G_I + h + 1]
        ir = g_t[G_I + h:G_I + h + 1, :]
        tot = bcum[L - 1:L, G_F + h:G_F + h + 1]
        mprev = m_ref[0:1, h:h + 1]

        dm = jnp.where(tri, bc - br + ir, NEG)
        inter = bc + mprev
        mt = jnp.maximum(inter, jnp.max(dm, axis=-1, keepdims=True))
        w_intra = jnp.exp(dm - mt)
        w_inter = jnp.exp(inter - mt)
        caug = c_ref[h]
        s = _dot_nt(q, k) * w_intra
        r = w_inter * _dot(q, caug.astype(BF16)) + _dot(s.astype(BF16), vaug)
        num = r[:, :HD]
        den = r[:, HD:HD + 1]
        hout = num / jnp.maximum(jnp.abs(den), jnp.exp(-mt))

        gend = tot - bc + ic
        mnew = jnp.maximum(tot + mprev, jnp.max(gend, axis=0, keepdims=True))
        kw = (k.astype(F32) * jnp.exp(gend - mnew)).astype(BF16)
        c_ref[h] = jnp.exp(tot + mprev - mnew) * caug + _dot_tn(kw, vaug)
        m_ref[0:1, h:h + 1] = mnew

        yh = _rms(hout, ng[:, sl]) * jax.nn.sigmoid(og[:, sl])
        y_ref[:, sl] = yh.astype(y_ref.dtype)

    @pl.when(c == pl.num_programs(1) - 1)
    def _():
        cout_ref[...] = c_ref[...]
        mout_ref[...] = m_ref[...]


def _mlstm(dst, mqkv, mo, gates, c0, m0, bias, ng, row0, n_seq, seq, L):
    nc = seq // L
    blk0 = row0 // L
    rspec = lambda n: pl.BlockSpec((L, n), lambda b, c: (blk0 + b * nc + c, 0))
    cspec = pl.BlockSpec((None, N_HEADS, HD, LANE), lambda b, c: (b, 0, 0, 0))
    mspec = pl.BlockSpec((None, 1, LANE), lambda b, c: (b, 0, 0))
    return pl.pallas_call(
        _mlstm_kernel, grid=(n_seq, nc),
        in_specs=[_DST_SPEC, rspec(768), rspec(256), rspec(LANE), cspec, mspec, _full_spec((1, LANE)),
                  _full_spec((1, 256))],
        out_specs=(rspec(256), cspec, mspec), input_output_aliases={0: 0},
        out_shape=(jax.ShapeDtypeStruct(dst.shape, dst.dtype),
                   jax.ShapeDtypeStruct((n_seq, N_HEADS, HD, LANE), F32),
                   jax.ShapeDtypeStruct((n_seq, 1, LANE), F32)),
        scratch_shapes=[pltpu.VMEM((N_HEADS, HD, LANE), F32), pltpu.VMEM((1, LANE), F32)],
        compiler_params=_cparams(("parallel", "arbitrary")), name="mlstm",
    )(dst, mqkv, mo, gates, c0, m0, bias, ng)


def _mla_prompt_kernel(dst_ref, q_ref, kc_ref, vt_ref, wuvt_ref, o_ref, acc_ref, m_ref, s_ref):
    tq = q_ref.shape[2]
    tk = C_TILE
    i = pl.program_id(1)
    m_ref[...] = jnp.full_like(m_ref, NEG)
    acc_ref[...] = jnp.zeros_like(acc_ref)

    def scores(j, slot):
        start = pl.multiple_of(j * tk, tk)
        kt = kc_ref[pl.ds(start, tk), :]
        for h in range(N_HEADS):
            s_ref[slot, h] = _dot(kt, q_ref[h])

    def consume(j, slot, masked):
        vt = vt_ref[j]
        for h in range(N_HEADS):
            s = s_ref[slot, h]
            if masked:
                krow = lax.broadcasted_iota(jnp.int32, (tk, tq), 0) // CHUNK
                qcol = lax.broadcasted_iota(jnp.int32, (tk, tq), 1) // CHUNK
                s = jnp.where(krow <= qcol, s, NEG)
            m_old = m_ref[h]
            m_new = jnp.maximum(m_old, jnp.max(s, axis=0, keepdims=True))
            p = jnp.exp2(s - m_new).astype(BF16)
            acc_ref[h] = jnp.exp2(m_old - m_new) * acc_ref[h] + _dot(vt, p)
            m_ref[h] = m_new

    scores(0, 0)

    def pair(j):
        scores(j + 1, 1)
        consume(j, 0, False)
        scores(j + 2, 0)
        consume(j + 1, 1, False)

    def body4(jj, carry):
        pair(4 * jj)
        pair(4 * jj + 2)
        return carry

    def body2(jj, carry):
        pair(4 * (i // 4) + 2 * jj)
        return carry

    lax.fori_loop(0, i // 4, body4, 0)
    lax.fori_loop(0, (i % 4) // 2, body2, 0)

    @pl.when(i % 2 == 1)
    def _():
        scores(i, 1)
        consume(i - 1, 0, False)
        consume(i, 1, True)

    @pl.when(i % 2 == 0)
    def _():
        consume(i, 0, True)

    ys = []
    for h in range(N_HEADS):
        acc = acc_ref[h]
        o = (acc[:C_KV_LORA] / acc[C_KV_LORA:C_KV_LORA + 1]).astype(BF16)
        ys.append(_dot(wuvt_ref[h], o))
    o_ref[...] = jnp.concatenate(ys, axis=0).T.astype(o_ref.dtype)


def _mla_prompt(dst, qt, kc, vt, wuvt, batch, seq):
    tq = C_TILE
    nq = seq // tq
    return pl.pallas_call(
        _mla_prompt_kernel, grid=(batch, nq),
        in_specs=[_DST_SPEC, pl.BlockSpec((N_HEADS, 256, tq), lambda b, i: (0, 0, b * nq + i)),
                  pl.BlockSpec((seq, 256), lambda b, i: (b, 0)),
                  pl.BlockSpec((seq // C_TILE, VT_ROWS, C_TILE), lambda b, i: (b, 0, 0)),
                  _full_spec((N_HEADS, HD, 128))],
        out_specs=pl.BlockSpec((tq, 256), lambda b, i: (b * nq + i, 0)),
        out_shape=jax.ShapeDtypeStruct(dst.shape, dst.dtype), input_output_aliases={0: 0},
        scratch_shapes=[pltpu.VMEM((N_HEADS, VT_ROWS, tq), F32), pltpu.VMEM((N_HEADS, 1, tq), F32),
                        pltpu.VMEM((2, N_HEADS, C_TILE, tq), F32)],
        compiler_params=_cparams(("parallel", "arbitrary")), name="mla_prompt",
    )(dst, qt, kc, vt, wuvt)


def _mla_sample_kernel(dst_ref, q_ref, kcache_ref, knew_ref, wuv_ref, o_ref):
    seq = q_ref.shape[1]
    qs = q_ref[...].reshape(N_HEADS * seq, 2 * LANE)
    kall = jnp.concatenate([kcache_ref[...], knew_ref[...]], axis=0)
    s = _dot_nt(qs, kall)
    m = jnp.max(s, axis=-1, keepdims=True)
    p = jnp.exp2(s - m)
    l = jnp.sum(p, axis=-1, keepdims=True)
    o = (_dot(p.astype(BF16), kall[:, :C_KV_LORA]) / l).astype(BF16)
    for h in range(N_HEADS):
        o_ref[:, h * HD:(h + 1) * HD] = _dot(o[h * seq:(h + 1) * seq], wuv_ref[h]).astype(o_ref.dtype)


def _mla_sample(dst, qabs, kc, kcache, wuv, layer, row0, n_seq, seq):
    blk0 = row0 // seq
    past = kcache.shape[2]
    return pl.pallas_call(
        _mla_sample_kernel, grid=(n_seq,),
        in_specs=[_DST_SPEC, pl.BlockSpec((N_HEADS, seq, 256), lambda b: (0, blk0 + b, 0)),
                  pl.BlockSpec((None, None, past, 256), lambda b: (layer, b, 0, 0)),
                  pl.BlockSpec((seq, 256), lambda b: (blk0 + b, 0)),
                  _full_spec((N_HEADS, 128, HD))],
        out_specs=pl.BlockSpec((seq, 256), lambda b: (blk0 + b, 0)),
        out_shape=jax.ShapeDtypeStruct(dst.shape, dst.dtype), input_output_aliases={0: 0},
        compiler_params=_cparams(("parallel",)), name="mla_sample",
    )(dst, qabs, kcache, kc, wuv)


def _outmlp_kernel(x_ref, ya_ref, yb_ref, yc_ref, yd_ref, wout_ref, g2_ref, w1_ref, w2_ref, fg_ref, o_ref, *, final):
    y = jnp.concatenate([ya_ref[...], yb_ref[...], yc_ref[...], yd_ref[...]], axis=-1)
    x1 = x_ref[...] + _dot(y, wout_ref[...])
    hm = _rms(x1, g2_ref[...]).astype(BF16)
    acc = x1
    for c in range(D_FF // D_MODEL):
        sl = slice(c * D_MODEL, (c + 1) * D_MODEL)
        hc = jnp.square(jnp.maximum(_dot(hm, w1_ref[:, sl]), 0.0)).astype(BF16)
        acc = acc + _dot(hc, w2_ref[sl, :])
    if final:
        acc = _rms(acc, fg_ref[...])
    o_ref[...] = acc


def _outmlp(x, ya, yb, yc, yd, wout, g2, w1, w2, fg, final):
    t = x.shape[0]
    tm = ROW_TILE
    row = lambda n: pl.BlockSpec((tm, n), lambda i: (i, 0))
    return pl.pallas_call(
        functools.partial(_outmlp_kernel, final=final), grid=(t // tm,),
        in_specs=[row(D_MODEL), row(256), row(256), row(256), row(256), _const_spec((D_MODEL, D_MODEL)),
                  _full_spec((1, D_MODEL)), _const_spec((D_MODEL, D_FF)), _const_spec((D_FF, D_MODEL)),
                  _full_spec((1, D_MODEL))],
        out_specs=row(D_MODEL), out_shape=jax.ShapeDtypeStruct((t, D_MODEL), F32),
        input_output_aliases={0: 0},
        compiler_params=_cparams(("parallel",)), name="outmlp",
    )(x, ya, yb, yc, yd, wout, g2, w1, w2, fg)


def _rot_cols(w):
    half = C_ROPE // 2
    return jnp.concatenate([-w[..., half:], w[..., :half]], axis=-1)


def _pad_lanes(v, n=LANE, at=0):
    out = jnp.zeros(v.shape[:-1] + (n,), v.dtype)
    return out.at[..., at:at + v.shape[-1]].set(v)


def _prep_w_in(w):
    sizes = (256, 256, 256, 256, 512, 4, 256, 128, 32, 256, 256, 256, 4, 4, 256)
    cuts = [0]
    for s in sizes:
        cuts.append(cuts[-1] + s)
    (a_q, a_k, a_v, b_z, b_xbc, b_dt, c_q, c_kv, c_kr, m_q, m_k, m_v, m_i, m_f, m_o) = (
        w[..., cuts[n]:cuts[n + 1]] for n in range(len(sizes)))
    kr4 = jnp.concatenate([c_kr] * N_HEADS, axis=-1)
    krr4 = jnp.concatenate([_rot_cols(c_kr)] * N_HEADS, axis=-1)
    gates = _pad_lanes(jnp.concatenate([b_dt, m_i, m_f], axis=-1))
    out = jnp.concatenate([a_q, a_k, a_v, b_z, b_xbc, c_q, c_kv, kr4, krr4, m_q, m_k, m_v, m_o, gates], axis=-1)
    assert out.shape[-1] == N_PROJ
    return out.astype(BF16)


def _prep_w_uq(w):
    d = w.shape[0]
    w4 = w.reshape(d, 256, N_HEADS, C_NOPE + C_ROPE)
    nope = w4[..., :C_NOPE].reshape(d, 256, N_HEADS * C_NOPE)
    rope = w4[..., C_NOPE:]
    return jnp.concatenate([nope, rope.reshape(d, 256, N_HEADS * C_ROPE),
                            _rot_cols(rope).reshape(d, 256, N_HEADS * C_ROPE)], axis=-1).astype(BF16)


def _prep_w_ukv(w):
    d = w.shape[0]
    w4 = w.reshape(d, C_KV_LORA, N_HEADS, C_NOPE + HD)
    w_uk = w4[..., :C_NOPE]
    w_uv = w4[..., C_NOPE:]
    wabs = jnp.zeros((d, N_HEADS * C_NOPE, N_HEADS * C_KV_LORA), w.dtype)
    for h in range(N_HEADS):
        wabs = wabs.at[:, h * C_NOPE:(h + 1) * C_NOPE, h * C_KV_LORA:(h + 1) * C_KV_LORA].set(
            jnp.swapaxes(w_uk[:, :, h, :], 1, 2))
    wuv = jnp.transpose(w_uv, (0, 2, 1, 3))
    wuvt = jnp.transpose(w_uv, (0, 2, 3, 1))
    return wabs.astype(BF16), wuv.astype(BF16), wuvt.astype(BF16)


def _rope_tables(pos):
    half = C_ROPE // 2
    inv = jnp.exp(-math.log(ROPE_BASE) * jnp.arange(half, dtype=F32) / half)
    ang = pos.astype(F32)[:, None] * inv[None, :]
    reps = LANE // half
    return jnp.tile(jnp.cos(ang), (1, reps)), jnp.tile(jnp.sin(ang), (1, reps))


def _band_bias(table, q_rows):
    cols = q_rows + A_SPAN
    k = np.arange(q_rows + cols - 1)
    u = table[:, np.clip((q_rows - 1 - k) + A_SPAN, -A_REL_CLIP, A_REL_CLIP) + A_REL_CLIP].astype(F32)
    period = q_rows + cols
    u = jnp.pad(u, ((0, 0), (0, 1)))
    flat = jnp.tile(u, (1, q_rows))[:, q_rows - 1:q_rows - 1 + q_rows * (period - 1)]
    toep = flat.reshape(table.shape[0], q_rows, period - 1)[:, :, :cols]
    r = np.arange(q_rows)[:, None]
    s = np.arange(cols)[None, :] - (r // CHUNK) * CHUNK
    inband = (s >= 0) & (s < A_SPAN + CHUNK)
    return jnp.where(jnp.asarray(inband)[None], toep, NEG)


def kernel(x_prompt, x_sample, cache_attn_k, cache_attn_v, state_ssm_conv, state_ssm, cache_mla_ckv, cache_mla_kr,
           state_mlstm_c, state_mlstm_n, state_mlstm_m, norm1_g, w_in, attn_rel_bias, ssm_conv_w, ssm_conv_b,
           ssm_dt_bias, ssm_a_log, ssm_d, ssm_norm_g, mla_q_norm_g, mla_w_uq, mla_kv_norm_g, mla_w_ukv,
           mlstm_b_i, mlstm_b_f, mlstm_norm_g, w_out, norm2_g, mlp_w1, mlp_w2, final_g):
    batch, seq, _ = x_prompt.shape
    dbatch, dseq, _ = x_sample.shape
    depth = w_in.shape[0]
    past = cache_mla_ckv.shape[2]
    tp, ts = batch * seq, dbatch * dseq
    lp = min(REC_CHUNK_PROMPT, seq)
    assert dseq == CHUNK and past % CHUNK == 0 and cache_attn_k.shape[2] == A_SPAN
    assert seq % C_TILE == 0 and tp % ROW_TILE == 0 and ts % ROW_TILE == 0

    w_in_r = _prep_w_in(w_in)
    wuq_r = _prep_w_uq(mla_w_uq)
    wabs, wuv, wuvt = _prep_w_ukv(mla_w_ukv)
    w_out_b, w1_b, w2_b = w_out.astype(BF16), mlp_w1.astype(BF16), mlp_w2.astype(BF16)
    pos = jnp.concatenate([jnp.tile(jnp.arange(seq), batch), jnp.tile(past + jnp.arange(dseq), dbatch)])
    cos, sin = _rope_tables(pos)
    dtb = _pad_lanes(ssm_dt_bias, at=G_DT)[:, None, :]
    alog = _pad_lanes(ssm_a_log, at=G_DT)[:, None, :]
    dsk = _pad_lanes(ssm_d)[:, None, :]
    gate_bias = (_pad_lanes(mlstm_b_i, at=G_I) + _pad_lanes(mlstm_b_f, at=G_F))[:, None, :]
    cache_k = cache_attn_k.reshape(depth, dbatch, A_SPAN, 256)
    cache_v = cache_attn_v.reshape(depth, dbatch, A_SPAN, 256)
    kcache = jnp.concatenate([cache_mla_ckv] + [cache_mla_kr] * N_HEADS, axis=-1).astype(BF16)
    conv0_s = jnp.pad(state_ssm_conv, ((0, 0), (0, 0), (8 - (B_CONV - 1), 0), (0, 0)))
    caug0_s = jnp.concatenate([state_mlstm_c, state_mlstm_n[..., None],
                               jnp.zeros(state_mlstm_c.shape[:-1] + (LANE - HD - 1,), F32)], axis=-1)
    m0_s = _pad_lanes(state_mlstm_m)[:, :, None, :]
    conv0_p = jnp.zeros((batch, 8, B_CONV_DIM), F32)
    h0_p = jnp.zeros((batch, N_HEADS, HD, B_STATE), F32)
    caug0_p = jnp.zeros((batch, N_HEADS, HD, LANE), F32)
    m0_p = jnp.zeros((batch, 1, LANE), F32)

    x = jnp.concatenate([x_prompt.reshape(tp, D_MODEL), x_sample.reshape(ts, D_MODEL)], axis=0)
    outs = [[] for _ in range(18)]
    ya, yb, yc, yd = (jnp.zeros((tp + ts, GROUP_W), BF16) for _ in range(4))
    fg = final_g[None, :]
    for l in range(depth):
        (aq, akv, bz, bxbc, qabs, qt, ckv, kr, kc, vt, mqkv, mo, gates) = _inproj(
            x, norm1_g[l][None], w_in_r[l], mla_q_norm_g[l][None], mla_kv_norm_g[l][None], wuq_r[l], wabs[l],
            cos, sin)

        ya = _attn_a_prompt(ya, aq, akv, _band_bias(attn_rel_bias[l], A_QBLOCK), batch, seq)
        ya = _attn_a_sample(ya, aq, akv, cache_k, cache_v, _band_bias(attn_rel_bias[l], CHUNK), l, tp, dbatch, dseq)

        ssd_w = (ssm_conv_w[l], ssm_conv_b[l][None], dtb[l], alog[l], dsk[l], ssm_norm_g[l][None])
        yb, h_p = _ssd(yb, bz, bxbc, gates, conv0_p, h0_p, *ssd_w, 0, batch, seq, lp)
        yb, h_s = _ssd(yb, bz, bxbc, gates, conv0_s[l], state_ssm[l], *ssd_w, tp, dbatch, dseq, dseq)

        yc = _mla_prompt(yc, qt, kc, vt, wuvt[l], batch, seq)
        yc = _mla_sample(yc, qabs, kc, kcache, wuv[l], l, tp, dbatch, dseq)

        ml_w = (gate_bias[l], mlstm_norm_g[l][None])
        yd, c_p, m_p = _mlstm(yd, mqkv, mo, gates, caug0_p, m0_p, *ml_w, 0, batch, seq, lp)
        yd, c_s, m_s = _mlstm(yd, mqkv, mo, gates, caug0_s[l], m0_s[l], *ml_w, tp, dbatch, dseq, dseq)

        x = _outmlp(x, ya, yb, yc, yd, w_out_b[l],
                    norm2_g[l][None], w1_b[l], w2_b[l], fg, l == depth - 1)

        akv_p = akv[:tp].reshape(batch, seq, 2, N_HEADS, HD)[:, seq - min(A_SPAN, seq):]
        akv_s = akv[tp:].reshape(dbatch, dseq, 2, N_HEADS, HD)
        new = (akv_p[:, :, 0], akv_p[:, :, 1],
               bxbc[:tp].reshape(batch, seq, B_CONV_DIM)[:, seq - (B_CONV - 1):], h_p,
               ckv[:tp].reshape(batch, seq, C_KV_LORA), kr[:tp].reshape(batch, seq, C_ROPE),
               c_p[..., :HD], c_p[..., HD], m_p[:, 0, :N_HEADS],
               jnp.concatenate([cache_attn_k[l][:, dseq:], akv_s[:, :, 0]], axis=1),
               jnp.concatenate([cache_attn_v[l][:, dseq:], akv_s[:, :, 1]], axis=1),
               bxbc[tp:].reshape(dbatch, dseq, B_CONV_DIM)[:, dseq - (B_CONV - 1):], h_s,
               ckv[tp:].reshape(dbatch, dseq, C_KV_LORA), kr[tp:].reshape(dbatch, dseq, C_ROPE),
               c_s[..., :HD], c_s[..., HD], m_s[:, 0, :N_HEADS])
        for dst, v in zip(outs, new):
            dst.append(v)

    y_prompt = x[:tp].reshape(batch, seq, D_MODEL)
    y_sample = x[tp:].reshape(dbatch, dseq, D_MODEL)
    return (y_prompt, y_sample) + tuple(jnp.stack(v) for v in outs)
```

```python
import functools
import math

import jax
import jax.numpy as jnp
import numpy as np
from jax import lax
from jax.experimental import pallas as pl
from jax.experimental.pallas import tpu as pltpu

F32 = jnp.float32
BF16 = jnp.bfloat16
EPS = 1e-6
NEG = -1e30
ROPE_BASE = 10000.0

CHUNK = 64
D_MODEL = 1024
GROUP_W = 256
D_FF = 4096
N_HEADS = 4
HD = 64
A_BAND_CHUNKS = 8
A_SPAN = A_BAND_CHUNKS * CHUNK
A_REL_CLIP = 128
B_GROUPS = 2
B_STATE = 64
B_CONV = 4
B_CONV_DIM = 512
C_NOPE = 64
C_ROPE = 32
C_KV_LORA = 128
C_SCALE = (C_NOPE + C_ROPE) ** -0.5
C_QSCALE = C_SCALE * math.log2(math.e)
VT_ROWS = 144
LANE = 128
VMEM_LIMIT = 56 * 1024 * 1024

OFF_A, OFF_BZ, OFF_XBC, OFF_CQ, OFF_CKV, OFF_KR, OFF_KRR, OFF_MQKV, OFF_MO, OFF_G, N_PROJ = (
    0, 768, 1024, 1536, 1792, 1920, 2048, 2176, 2944, 3200, 3328)
G_DT, G_I, G_F = 0, 4, 8

ROW_TILE = 512
A_QBLOCK = 256
C_TILE = 256
C_UNROLL_PAIRS = 4
REC_CHUNK_PROMPT = 256


def _cparams(sem):
    return pltpu.CompilerParams(dimension_semantics=sem, vmem_limit_bytes=VMEM_LIMIT)


def _rms(x, g):
    return x * lax.rsqrt(jnp.mean(x * x, axis=-1, keepdims=True) + EPS) * g


def _dot(a, b):
    return jnp.dot(a, b, preferred_element_type=F32)


def _dot_nt(a, b):
    return lax.dot_general(a, b, (((1,), (1,)), ((), ())), preferred_element_type=F32)


def _dot_tn(a, b):
    return lax.dot_general(a, b, (((0,), (0,)), ((), ())), preferred_element_type=F32)


def _cumsum_rows(x, tri_bf16):
    hi = x.astype(BF16)
    r1 = x - hi.astype(F32)
    mid = r1.astype(BF16)
    lo = (r1 - mid.astype(F32)).astype(BF16)
    return _dot(tri_bf16, hi) + _dot(tri_bf16, mid) + _dot(tri_bf16, lo)


def _log1p_exp_neg_abs(x):
    return jnp.log1p(jnp.exp(-jnp.abs(x)))


def _full_spec(shape):
    n = len(shape)
    return pl.BlockSpec(shape, lambda *_: (0,) * n)


_DST_SPEC = pl.BlockSpec(memory_space=pl.ANY)


def _const_spec(shape):
    n = len(shape)
    return pl.BlockSpec(shape, lambda *_: (0,) * n, pipeline_mode=pl.Buffered(1))


def _layer_spec(shape, layer):
    n = len(shape)
    return pl.BlockSpec((None,) + tuple(shape), lambda *_: (layer,) + (0,) * n, pipeline_mode=pl.Buffered(1))


def _split_rows(n_prompt_tiles, width):
    npt = n_prompt_tiles
    return (pl.BlockSpec((ROW_TILE, width), lambda i: (jnp.minimum(i, npt - 1), 0)),
            pl.BlockSpec((ROW_TILE, width), lambda i: (jnp.maximum(i - npt, 0), 0)))


def _inproj_kernel(xp_ref, xs_ref, g1_ref, w_ref, qg_ref, kvg_ref, wuq_ref, wabs_ref, cos_ref, sin_ref,
                   aq_ref, akv_ref, bz_ref, bxbc_ref, qabs_ref, qt_ref, ckv_ref, kr_ref, kc_ref, vt_ref,
                   mqkv_ref, mo_ref, gates_ref, *, n_prompt_tiles):
    tm = xp_ref.shape[0]
    x = jnp.where(pl.program_id(0) < n_prompt_tiles, xp_ref[...], xs_ref[...])
    hn = _rms(x, g1_ref[...]).astype(BF16)

    def seg(a, b):
        return _dot(hn, w_ref[:, a:b])

    a = seg(OFF_A, OFF_BZ)
    aq_ref[...] = (a[:, :GROUP_W] * HD ** -0.5).astype(BF16)
    akv_ref[...] = a[:, GROUP_W:]
    bz_ref[...] = seg(OFF_BZ, OFF_XBC)
    bxbc_ref[...] = seg(OFF_XBC, OFF_CQ)

    cos = cos_ref[...]
    sin = sin_ref[...]
    cqn = _rms(seg(OFF_CQ, OFF_CKV), qg_ref[...]).astype(BF16)
    qf = _dot(cqn, wuq_ref[...])
    qrope = (qf[:, 256:384] * cos + qf[:, 384:512] * sin) * C_QSCALE
    qlat = _dot(qf[:, :256].astype(BF16), wabs_ref[...]) * C_QSCALE
    lane_head = lax.broadcasted_iota(jnp.int32, (tm, LANE), 1) // C_ROPE
    for h in range(N_HEADS):
        qh = jnp.concatenate([qlat[:, h * LANE:(h + 1) * LANE], jnp.where(lane_head == h, qrope, 0.0)], axis=-1)
        qabs_ref[h] = qh.astype(BF16)
        qt_ref[h] = qh.T.astype(BF16)

    ckv = _rms(seg(OFF_CKV, OFF_KR), kvg_ref[...])
    kr4 = seg(OFF_KR, OFF_KRR) * cos + seg(OFF_KRR, OFF_MQKV) * sin
    ckv_ref[...] = ckv
    kr_ref[...] = kr4[:, :C_ROPE]
    kc_ref[...] = jnp.concatenate([ckv, kr4], axis=-1).astype(BF16)
    ones_rows = (lax.broadcasted_iota(jnp.int32, (VT_ROWS - C_KV_LORA, C_TILE), 0) == 0).astype(BF16)
    for s in range(tm // C_TILE):
        vt_ref[s, 0:C_KV_LORA, :] = ckv[s * C_TILE:(s + 1) * C_TILE, :].T.astype(BF16)
        vt_ref[s, C_KV_LORA:VT_ROWS, :] = ones_rows

    m = seg(OFF_MQKV, OFF_MO)
    mqkv_ref[:, 0:256] = m[:, 0:256].astype(BF16)
    mqkv_ref[:, 256:512] = (m[:, 256:512] * HD ** -0.5).astype(BF16)
    mqkv_ref[:, 512:768] = m[:, 512:768].astype(BF16)
    mo_ref[...] = seg(OFF_MO, OFF_G)
    gates_ref[...] = seg(OFF_G, N_PROJ)


def _inproj(xp, xs, g1, w, qg, kvg, wuq, wabs, cos, sin, layer):
    t = xp.shape[0] + xs.shape[0]
    tm = ROW_TILE
    npt = xp.shape[0] // tm
    row = lambda n: pl.BlockSpec((tm, n), lambda i: (i, 0))
    out_shape = (
        jax.ShapeDtypeStruct((t, 256), BF16),
        jax.ShapeDtypeStruct((t, 512), F32),
        jax.ShapeDtypeStruct((t, 256), F32),
        jax.ShapeDtypeStruct((t, 512), F32),
        jax.ShapeDtypeStruct((N_HEADS, t, 256), BF16),
        jax.ShapeDtypeStruct((N_HEADS, 256, t), BF16),
        jax.ShapeDtypeStruct((t, 128), F32),
        jax.ShapeDtypeStruct((t, C_ROPE), F32),
        jax.ShapeDtypeStruct((t, 256), BF16),
        jax.ShapeDtypeStruct((t // C_TILE, VT_ROWS, C_TILE), BF16),
        jax.ShapeDtypeStruct((t, 768), BF16),
        jax.ShapeDtypeStruct((t, 256), F32),
        jax.ShapeDtypeStruct((t, LANE), F32),
    )
    out_specs = (
        row(256), row(512), row(256), row(512),
        pl.BlockSpec((N_HEADS, tm, 256), lambda i: (0, i, 0)),
        pl.BlockSpec((N_HEADS, 256, tm), lambda i: (0, 0, i)),
        row(128), row(C_ROPE), row(256),
        pl.BlockSpec((tm // C_TILE, VT_ROWS, C_TILE), lambda i: (i, 0, 0)),
        row(768), row(256), row(LANE),
    )
    in_specs = [*_split_rows(npt, D_MODEL), _full_spec((1, D_MODEL)), _layer_spec((D_MODEL, N_PROJ), layer),
                _full_spec((1, 256)), _full_spec((1, 128)), _layer_spec((256, 512), layer),
                _layer_spec((256, 512), layer), row(LANE), row(LANE)]
    return pl.pallas_call(
        functools.partial(_inproj_kernel, n_prompt_tiles=npt), grid=(t // tm,), in_specs=in_specs,
        out_specs=out_specs, out_shape=out_shape,
        compiler_params=_cparams(("arbitrary",)), name="inproj",
    )(xp, xs, g1, w, qg, kvg, wuq, wabs, cos, sin)


def _band_heads(q, kb, vb, bias_ref, valid, o_ref):
    for h in range(N_HEADS):
        sl = slice(h * HD, (h + 1) * HD)
        s = _dot_nt(q[:, sl], kb[:, sl]) + bias_ref[h]
        if valid is not None:
            s = jnp.where(valid, s, NEG)
        m = jnp.max(s, axis=-1, keepdims=True)
        p = jnp.exp(s - m)
        l = jnp.sum(p, axis=-1, keepdims=True)
        o = _dot(p.astype(BF16), vb[:, sl]) / l
        o_ref[:, sl] = o.astype(o_ref.dtype)


def _attn_a_prompt_kernel(dst_ref, q_ref, k0_ref, k1_ref, k2_ref, bias_ref, o_ref):
    i = pl.program_id(1)
    kv = jnp.concatenate([k0_ref[...], k1_ref[...], k2_ref[...]], axis=0)
    kb = kv[:, :GROUP_W].astype(BF16)
    vb = kv[:, GROUP_W:].astype(BF16)
    col = lax.broadcasted_iota(jnp.int32, (A_QBLOCK, 3 * A_QBLOCK), 1)
    valid = col >= (2 - i) * A_QBLOCK
    _band_heads(q_ref[...], kb, vb, bias_ref, valid, o_ref)


def _attn_a_prompt(dst, aq, akv, biasm, batch, seq):
    nb = seq // A_QBLOCK
    qb = A_QBLOCK

    def kspec(back):
        return pl.BlockSpec((qb, 512), lambda b, i: (b * nb + jnp.maximum(i - back, 0), 0))

    return pl.pallas_call(
        _attn_a_prompt_kernel, grid=(batch, nb),
        in_specs=[_DST_SPEC, pl.BlockSpec((qb, 256), lambda b, i: (b * nb + i, 0)), kspec(2), kspec(1), kspec(0),
                  _full_spec(biasm.shape)],
        out_specs=pl.BlockSpec((qb, 256), lambda b, i: (b * nb + i, 0)),
        out_shape=jax.ShapeDtypeStruct(dst.shape, dst.dtype), input_output_aliases={0: 0},
        compiler_params=_cparams(("parallel", "parallel")), name="attn_a_prompt",
    )(dst, aq, akv, akv, akv, biasm)


def _attn_a_sample_kernel(dst_ref, q_ref, ck_ref, cv_ref, kvn_ref, bias_ref, o_ref):
    kvn = kvn_ref[...]
    kb = jnp.concatenate([ck_ref[...], kvn[:, :GROUP_W]], axis=0).astype(BF16)
    vb = jnp.concatenate([cv_ref[...], kvn[:, GROUP_W:]], axis=0).astype(BF16)
    _band_heads(q_ref[...], kb, vb, bias_ref, None, o_ref)


def _attn_a_sample(dst, aq, akv, cache_k, cache_v, bias, layer, row0, n_seq, seq):
    blk0 = row0 // seq
    cspec = pl.BlockSpec((None, None, A_SPAN, 256), lambda b: (layer, b, 0, 0))
    return pl.pallas_call(
        _attn_a_sample_kernel, grid=(n_seq,),
        in_specs=[_DST_SPEC, pl.BlockSpec((seq, 256), lambda b: (blk0 + b, 0)), cspec, cspec,
                  pl.BlockSpec((seq, 512), lambda b: (blk0 + b, 0)), _full_spec(bias.shape)],
        out_specs=pl.BlockSpec((seq, 256), lambda b: (blk0 + b, 0)),
        out_shape=jax.ShapeDtypeStruct(dst.shape, dst.dtype), input_output_aliases={0: 0},
        compiler_params=_cparams(("parallel",)), name="attn_a_sample",
    )(dst, aq, cache_k, cache_v, akv, bias)


def _ssd_kernel(dst_ref, z_ref, xbc_ref, g_ref, conv0_ref, h0_ref, cw_ref, cb_ref, dtb_ref, alog_ref, dsk_ref, ng_ref,
                y_ref, hout_ref, xpad_ref, h_ref):
    L = z_ref.shape[0]
    c = pl.program_id(1)

    @pl.when(c == 0)
    def _():
        xpad_ref[0:8, :] = conv0_ref[...]
        h_ref[...] = h0_ref[...]

    xpad_ref[8:8 + L, :] = xbc_ref[...]
    conv = cb_ref[...]
    for j in range(B_CONV):
        conv = conv + xpad_ref[5 + j:5 + j + L, :] * cw_ref[j:j + 1, :]
    xpad_ref[0:8, :] = xpad_ref[L:L + 8, :]
    u = conv * jax.nn.sigmoid(conv)

    g = g_ref[...] + dtb_ref[...]
    dt = jnp.maximum(g, 0.0) + _log1p_exp_neg_abs(g)
    da = dt * (-jnp.exp(alog_ref[...]))
    row = lax.broadcasted_iota(jnp.int32, (L, L), 0)
    colm = lax.broadcasted_iota(jnp.int32, (L, L), 1)
    tri = row >= colm
    cs = _cumsum_rows(da, tri.astype(BF16))
    cs_t = cs.T

    ys = []
    gmat = [None] * B_GROUPS
    for h in range(N_HEADS):
        grp = h // (N_HEADS // B_GROUPS)
        xs = u[:, h * HD:(h + 1) * HD]
        bm = u[:, 256 + grp * B_STATE:256 + (grp + 1) * B_STATE].astype(BF16)
        cm = u[:, 384 + grp * B_STATE:384 + (grp + 1) * B_STATE].astype(BF16)
        if gmat[grp] is None:
            gmat[grp] = _dot_nt(cm, bm)
        csc = cs[:, G_DT + h:G_DT + h + 1]
        csr = cs_t[G_DT + h:G_DT + h + 1, :]
        tot = cs[L - 1:L, G_DT + h:G_DT + h + 1]
        dec = jnp.exp(jnp.where(tri, csc - csr, NEG))
        xdt = xs * dt[:, G_DT + h:G_DT + h + 1]
        hprev = h_ref[h]
        y = _dot((gmat[grp] * dec).astype(BF16), xdt.astype(BF16))
        y = y + _dot_nt(cm, hprev.astype(BF16)) * jnp.exp(csc)
        y = y + dsk_ref[0:1, h:h + 1] * xs
        ys.append(y)
        wend = (xdt * jnp.exp(tot - csc)).astype(BF16)
        h_ref[h] = jnp.exp(tot) * hprev + _dot_tn(wend, bm)
    y = jnp.concatenate(ys, axis=-1)
    z = z_ref[...]
    y = y * (z * jax.nn.sigmoid(z))
    y_ref[...] = _rms(y, ng_ref[...]).astype(y_ref.dtype)

    @pl.when(c == pl.num_programs(1) - 1)
    def _():
        hout_ref[...] = h_ref[...]


def _ssd(dst, bz, bxbc, gates, conv0, h0, cw, cb, dtb, alog, dsk, ng, row0, n_seq, seq, L):
    nc = seq // L
    blk0 = row0 // L
    rspec = lambda n: pl.BlockSpec((L, n), lambda b, c: (blk0 + b * nc + c, 0))
    return pl.pallas_call(
        _ssd_kernel, grid=(n_seq, nc),
        in_specs=[_DST_SPEC, rspec(256), rspec(512), rspec(LANE),
                  pl.BlockSpec((None, 8, 512), lambda b, c: (b, 0, 0)),
                  pl.BlockSpec((None, N_HEADS, HD, B_STATE), lambda b, c: (b, 0, 0, 0)),
                  _full_spec((B_CONV, 512)), _full_spec((1, 512)), _full_spec((1, LANE)), _full_spec((1, LANE)),
                  _full_spec((1, LANE)), _full_spec((1, 256))],
        out_specs=(rspec(256), pl.BlockSpec((None, N_HEADS, HD, B_STATE), lambda b, c: (b, 0, 0, 0))),
        input_output_aliases={0: 0},
        out_shape=(jax.ShapeDtypeStruct(dst.shape, dst.dtype),
                   jax.ShapeDtypeStruct((n_seq, N_HEADS, HD, B_STATE), F32)),
        scratch_shapes=[pltpu.VMEM((L + 8, 512), F32), pltpu.VMEM((N_HEADS, HD, B_STATE), F32)],
        compiler_params=_cparams(("parallel", "arbitrary")), name="ssd",
    )(dst, bz, bxbc, gates, conv0, h0, cw, cb, dtb, alog, dsk, ng)


def _mlstm_kernel(dst_ref, qkv_ref, o_ref, g_ref, c0_ref, m0_ref, bias_ref, ng_ref,
                  y_ref, cout_ref, mout_ref, c_ref, m_ref):
    L = qkv_ref.shape[0]
    c = pl.program_id(1)

    @pl.when(c == 0)
    def _():
        c_ref[...] = c0_ref[...]
        m_ref[...] = m0_ref[...]

    g = g_ref[...] + bias_ref[...]
    lf = jnp.minimum(g, 0.0) - _log1p_exp_neg_abs(g)
    row = lax.broadcasted_iota(jnp.int32, (L, L), 0)
    colm = lax.broadcasted_iota(jnp.int32, (L, L), 1)
    tri = row >= colm
    bcum = _cumsum_rows(lf, tri.astype(BF16))
    bcum_t = bcum.T
    g_t = g.T
    unit = (lax.broadcasted_iota(jnp.int32, (L, HD), 1) == 0).astype(BF16)
    ng = ng_ref[...]
    og = o_ref[...]

    for h in range(N_HEADS):
        sl = slice(h * HD, (h + 1) * HD)
        q = qkv_ref[:, h * HD:(h + 1) * HD]
        k = qkv_ref[:, 256 + h * HD:256 + (h + 1) * HD]
        v = qkv_ref[:, 512 + h * HD:512 + (h + 1) * HD]
        vaug = jnp.concatenate([v, unit], axis=-1)
        bc = bcum[:, G_F + h:G_F + h + 1]
        br = bcum_t[G_F + h:G_F + h + 1, :]
        ic = g[:, G_I + h:G_I + h + 1]
        ir = g_t[G_I + h:G_I + h + 1, :]
        tot = bcum[L - 1:L, G_F + h:G_F + h + 1]
        mprev = m_ref[0:1, h:h + 1]

        dm = jnp.where(tri, bc - br + ir, NEG)
        inter = bc + mprev
        mt = jnp.maximum(inter, jnp.max(dm, axis=-1, keepdims=True))
        w_intra = jnp.exp(dm - mt)
        w_inter = jnp.exp(inter - mt)
        caug = c_ref[h]
        s = _dot_nt(q, k) * w_intra
        r = w_inter * _dot(q, caug.astype(BF16)) + _dot(s.astype(BF16), vaug)
        num = r[:, :HD]
        den = r[:, HD:HD + 1]
        hout = num / jnp.maximum(jnp.abs(den), jnp.exp(-mt))

        gend = tot - bc + ic
        mnew = jnp.maximum(tot + mprev, jnp.max(gend, axis=0, keepdims=True))
        kw = (k.astype(F32) * jnp.exp(gend - mnew)).astype(BF16)
        c_ref[h] = jnp.exp(tot + mprev - mnew) * caug + _dot_tn(kw, vaug)
        m_ref[0:1, h:h + 1] = mnew

        yh = _rms(hout, ng[:, sl]) * jax.nn.sigmoid(og[:, sl])
        y_ref[:, sl] = yh.astype(y_ref.dtype)

    @pl.when(c == pl.num_programs(1) - 1)
    def _():
        cout_ref[...] = c_ref[...]
        mout_ref[...] = m_ref[...]


def _mlstm(dst, mqkv, mo, gates, c0, m0, bias, ng, row0, n_seq, seq, L):
    nc = seq // L
    blk0 = row0 // L
    rspec = lambda n: pl.BlockSpec((L, n), lambda b, c: (blk0 + b * nc + c, 0))
    cspec = pl.BlockSpec((None, N_HEADS, HD, LANE), lambda b, c: (b, 0, 0, 0))
    mspec = pl.BlockSpec((None, 1, LANE), lambda b, c: (b, 0, 0))
    return pl.pallas_call(
        _mlstm_kernel, grid=(n_seq, nc),
        in_specs=[_DST_SPEC, rspec(768), rspec(256), rspec(LANE), cspec, mspec, _full_spec((1, LANE)),
                  _full_spec((1, 256))],
        out_specs=(rspec(256), cspec, mspec), input_output_aliases={0: 0},
        out_shape=(jax.ShapeDtypeStruct(dst.shape, dst.dtype),
                   jax.ShapeDtypeStruct((n_seq, N_HEADS, HD, LANE), F32),
                   jax.ShapeDtypeStruct((n_seq, 1, LANE), F32)),
        scratch_shapes=[pltpu.VMEM((N_HEADS, HD, LANE), F32), pltpu.VMEM((1, LANE), F32)],
        compiler_params=_cparams(("parallel", "arbitrary")), name="mlstm",
    )(dst, mqkv, mo, gates, c0, m0, bias, ng)


def _mla_prompt_kernel(dst_ref, q_ref, kc_ref, vt_ref, wuvt_ref, o_ref, acc_ref, m_ref, s_ref):
    tq = q_ref.shape[2]
    tk = C_TILE
    i = pl.program_id(1)
    m_ref[...] = jnp.full_like(m_ref, NEG)
    acc_ref[...] = jnp.zeros_like(acc_ref)

    def scores(j, slot):
        start = pl.multiple_of(j * tk, tk)
        kt = kc_ref[pl.ds(start, tk), :]
        for h in range(N_HEADS):
            s_ref[slot, h] = _dot(kt, q_ref[h])

    def consume(j, slot, masked):
        vt = vt_ref[j]
        for h in range(N_HEADS):
            s = s_ref[slot, h]
            if masked:
                krow = lax.broadcasted_iota(jnp.int32, (tk, tq), 0) // CHUNK
                qcol = lax.broadcasted_iota(jnp.int32, (tk, tq), 1) // CHUNK
                s = jnp.where(krow <= qcol, s, NEG)
            m_old = m_ref[h]
            m_new = jnp.maximum(m_old, jnp.max(s, axis=0, keepdims=True))
            p = jnp.exp2(s - m_new).astype(BF16)
            acc_ref[h] = jnp.exp2(m_old - m_new) * acc_ref[h] + _dot(vt, p)
            m_ref[h] = m_new

    scores(0, 0)

    def pair(j):
        scores(j + 1, 1)
        consume(j, 0, False)
        scores(j + 2, 0)
        consume(j + 1, 1, False)

    def body_main(jj, carry):
        for u in range(C_UNROLL_PAIRS):
            pair(2 * (C_UNROLL_PAIRS * jj + u))
        return carry

    def body_rest(jj, carry):
        pair(2 * (C_UNROLL_PAIRS * n_main + jj))
        return carry

    n_main = i // (2 * C_UNROLL_PAIRS)
    lax.fori_loop(0, n_main, body_main, 0)
    lax.fori_loop(0, i // 2 - C_UNROLL_PAIRS * n_main, body_rest, 0)

    @pl.when(i % 2 == 1)
    def _():
        scores(i, 1)
        consume(i - 1, 0, False)
        consume(i, 1, True)

    @pl.when(i % 2 == 0)
    def _():
        consume(i, 0, True)

    ys = []
    for h in range(N_HEADS):
        acc = acc_ref[h]
        o = (acc[:C_KV_LORA] / acc[C_KV_LORA:C_KV_LORA + 1]).astype(BF16)
        ys.append(_dot(wuvt_ref[h], o))
    o_ref[...] = jnp.concatenate(ys, axis=0).T.astype(o_ref.dtype)


def _mla_prompt(dst, qt, kc, vt, wuvt, batch, seq):
    tq = C_TILE
    nq = seq // tq
    return pl.pallas_call(
        _mla_prompt_kernel, grid=(batch, nq),
        in_specs=[_DST_SPEC, pl.BlockSpec((N_HEADS, 256, tq), lambda b, i: (0, 0, b * nq + i)),
                  pl.BlockSpec((seq, 256), lambda b, i: (b, 0)),
                  pl.BlockSpec((seq // C_TILE, VT_ROWS, C_TILE), lambda b, i: (b, 0, 0)),
                  _full_spec((N_HEADS, HD, 128))],
        out_specs=pl.BlockSpec((tq, 256), lambda b, i: (b * nq + i, 0)),
        out_shape=jax.ShapeDtypeStruct(dst.shape, dst.dtype), input_output_aliases={0: 0},
        scratch_shapes=[pltpu.VMEM((N_HEADS, VT_ROWS, tq), F32), pltpu.VMEM((N_HEADS, 1, tq), F32),
                        pltpu.VMEM((2, N_HEADS, C_TILE, tq), F32)],
        compiler_params=_cparams(("parallel", "arbitrary")), name="mla_prompt",
    )(dst, qt, kc, vt, wuvt)


def _mla_sample_kernel(dst_ref, q_ref, ckv_ref, kr_ref, knew_ref, wuv_ref, o_ref):
    seq = q_ref.shape[1]
    qs = q_ref[...].reshape(N_HEADS * seq, 2 * LANE)
    kcache = jnp.concatenate([ckv_ref[...]] + [kr_ref[...]] * N_HEADS, axis=-1).astype(BF16)
    kall = jnp.concatenate([kcache, knew_ref[...]], axis=0)
    s = _dot_nt(qs, kall)
    m = jnp.max(s, axis=-1, keepdims=True)
    p = jnp.exp2(s - m)
    l = jnp.sum(p, axis=-1, keepdims=True)
    o = (_dot(p.astype(BF16), kall[:, :C_KV_LORA]) / l).astype(BF16)
    for h in range(N_HEADS):
        o_ref[:, h * HD:(h + 1) * HD] = _dot(o[h * seq:(h + 1) * seq], wuv_ref[h]).astype(o_ref.dtype)


def _mla_sample(dst, qabs, kc, cache_ckv, cache_kr, wuv, layer, row0, n_seq, seq):
    blk0 = row0 // seq
    past = cache_ckv.shape[2]
    return pl.pallas_call(
        _mla_sample_kernel, grid=(n_seq,),
        in_specs=[_DST_SPEC, pl.BlockSpec((N_HEADS, seq, 256), lambda b: (0, blk0 + b, 0)),
                  pl.BlockSpec((None, None, past, C_KV_LORA), lambda b: (layer, b, 0, 0)),
                  pl.BlockSpec((None, None, past, C_ROPE), lambda b: (layer, b, 0, 0)),
                  pl.BlockSpec((seq, 256), lambda b: (blk0 + b, 0)),
                  _full_spec((N_HEADS, 128, HD))],
        out_specs=pl.BlockSpec((seq, 256), lambda b: (blk0 + b, 0)),
        out_shape=jax.ShapeDtypeStruct(dst.shape, dst.dtype), input_output_aliases={0: 0},
        compiler_params=_cparams(("parallel",)), name="mla_sample",
    )(dst, qabs, cache_ckv, cache_kr, kc, wuv)


def _outmlp_kernel(xp_ref, xs_ref, ya_ref, yb_ref, yc_ref, yd_ref, wout_ref, g2_ref, w1_ref, w2_ref, fg_ref,
                   op_ref, os_ref, *, final, n_prompt_tiles):
    i = pl.program_id(0)
    y = jnp.concatenate([ya_ref[...], yb_ref[...], yc_ref[...], yd_ref[...]], axis=-1)
    x1 = jnp.where(i < n_prompt_tiles, xp_ref[...], xs_ref[...]) + _dot(y, wout_ref[...])
    hm = _rms(x1, g2_ref[...]).astype(BF16)
    acc = x1
    for c in range(D_FF // D_MODEL):
        sl = slice(c * D_MODEL, (c + 1) * D_MODEL)
        hc = jnp.square(jnp.maximum(_dot(hm, w1_ref[:, sl]), 0.0)).astype(BF16)
        acc = acc + _dot(hc, w2_ref[sl, :])
    if final:
        acc = _rms(acc, fg_ref[...])

    @pl.when(i < n_prompt_tiles)
    def _():
        op_ref[...] = acc

    @pl.when(i >= n_prompt_tiles)
    def _():
        os_ref[...] = acc


def _outmlp(xp, xs, ya, yb, yc, yd, wout, g2, w1, w2, fg, layer, final, in_place):
    t = xp.shape[0] + xs.shape[0]
    tm = ROW_TILE
    npt = xp.shape[0] // tm
    row = lambda n: pl.BlockSpec((tm, n), lambda i: (i, 0))
    xspecs = _split_rows(npt, D_MODEL)
    return pl.pallas_call(
        functools.partial(_outmlp_kernel, final=final, n_prompt_tiles=npt), grid=(t // tm,),
        in_specs=[*xspecs, row(256), row(256), row(256), row(256), _layer_spec((D_MODEL, D_MODEL), layer),
                  _full_spec((1, D_MODEL)), _layer_spec((D_MODEL, D_FF), layer), _layer_spec((D_FF, D_MODEL), layer),
                  _full_spec((1, D_MODEL))],
        out_specs=xspecs,
        out_shape=(jax.ShapeDtypeStruct(xp.shape, F32), jax.ShapeDtypeStruct(xs.shape, F32)),
        input_output_aliases={0: 0, 1: 1} if in_place else {},
        compiler_params=_cparams(("arbitrary",)), name="outmlp",
    )(xp, xs, ya, yb, yc, yd, wout, g2, w1, w2, fg)


def _rot_cols(w):
    half = C_ROPE // 2
    return jnp.concatenate([-w[..., half:], w[..., :half]], axis=-1)


def _pad_lanes(v, n=LANE, at=0):
    out = jnp.zeros(v.shape[:-1] + (n,), v.dtype)
    return out.at[..., at:at + v.shape[-1]].set(v)


def _prep_w_in(w):
    sizes = (256, 256, 256, 256, 512, 4, 256, 128, 32, 256, 256, 256, 4, 4, 256)
    cuts = [0]
    for s in sizes:
        cuts.append(cuts[-1] + s)
    (a_q, a_k, a_v, b_z, b_xbc, b_dt, c_q, c_kv, c_kr, m_q, m_k, m_v, m_i, m_f, m_o) = (
        w[..., cuts[n]:cuts[n + 1]] for n in range(len(sizes)))
    kr4 = jnp.concatenate([c_kr] * N_HEADS, axis=-1)
    krr4 = jnp.concatenate([_rot_cols(c_kr)] * N_HEADS, axis=-1)
    gates = _pad_lanes(jnp.concatenate([b_dt, m_i, m_f], axis=-1))
    out = jnp.concatenate([a_q, a_k, a_v, b_z, b_xbc, c_q, c_kv, kr4, krr4, m_q, m_k, m_v, m_o, gates], axis=-1)
    assert out.shape[-1] == N_PROJ
    return out.astype(BF16)


def _prep_w_uq(w):
    d = w.shape[0]
    w4 = w.reshape(d, 256, N_HEADS, C_NOPE + C_ROPE)
    nope = w4[..., :C_NOPE].reshape(d, 256, N_HEADS * C_NOPE)
    rope = w4[..., C_NOPE:]
    return jnp.concatenate([nope, rope.reshape(d, 256, N_HEADS * C_ROPE),
                            _rot_cols(rope).reshape(d, 256, N_HEADS * C_ROPE)], axis=-1).astype(BF16)


def _prep_w_ukv(w):
    d = w.shape[0]
    w4 = w.reshape(d, C_KV_LORA, N_HEADS, C_NOPE + HD)
    w_uk = w4[..., :C_NOPE]
    w_uv = w4[..., C_NOPE:]
    wabs = jnp.zeros((d, N_HEADS * C_NOPE, N_HEADS * C_KV_LORA), w.dtype)
    for h in range(N_HEADS):
        wabs = wabs.at[:, h * C_NOPE:(h + 1) * C_NOPE, h * C_KV_LORA:(h + 1) * C_KV_LORA].set(
            jnp.swapaxes(w_uk[:, :, h, :], 1, 2))
    wuv = jnp.transpose(w_uv, (0, 2, 1, 3))
    wuvt = jnp.transpose(w_uv, (0, 2, 3, 1))
    return wabs.astype(BF16), wuv.astype(BF16), wuvt.astype(BF16)


def _rope_tables(pos):
    half = C_ROPE // 2
    inv = jnp.exp(-math.log(ROPE_BASE) * jnp.arange(half, dtype=F32) / half)
    ang = pos.astype(F32)[:, None] * inv[None, :]
    reps = LANE // half
    return jnp.tile(jnp.cos(ang), (1, reps)), jnp.tile(jnp.sin(ang), (1, reps))


def _band_bias(table, q_rows):
    cols = q_rows + A_SPAN
    k = np.arange(q_rows + cols - 1)
    u = table[:, np.clip((q_rows - 1 - k) + A_SPAN, -A_REL_CLIP, A_REL_CLIP) + A_REL_CLIP].astype(F32)
    period = q_rows + cols
    u = jnp.pad(u, ((0, 0), (0, 1)))
    flat = jnp.tile(u, (1, q_rows))[:, q_rows - 1:q_rows - 1 + q_rows * (period - 1)]
    toep = flat.reshape(table.shape[0], q_rows, period - 1)[:, :, :cols]
    r = np.arange(q_rows)[:, None]
    s = np.arange(cols)[None, :] - (r // CHUNK) * CHUNK
    inband = (s >= 0) & (s < A_SPAN + CHUNK)
    return jnp.where(jnp.asarray(inband)[None], toep, NEG)


def kernel(x_prompt, x_sample, cache_attn_k, cache_attn_v, state_ssm_conv, state_ssm, cache_mla_ckv, cache_mla_kr,
           state_mlstm_c, state_mlstm_n, state_mlstm_m, norm1_g, w_in, attn_rel_bias, ssm_conv_w, ssm_conv_b,
           ssm_dt_bias, ssm_a_log, ssm_d, ssm_norm_g, mla_q_norm_g, mla_w_uq, mla_kv_norm_g, mla_w_ukv,
           mlstm_b_i, mlstm_b_f, mlstm_norm_g, w_out, norm2_g, mlp_w1, mlp_w2, final_g):
    batch, seq, _ = x_prompt.shape
    dbatch, dseq, _ = x_sample.shape
    depth = w_in.shape[0]
    past = cache_mla_ckv.shape[2]
    tp, ts = batch * seq, dbatch * dseq
    lp = min(REC_CHUNK_PROMPT, seq)
    assert dseq == CHUNK and past % CHUNK == 0 and cache_attn_k.shape[2] == A_SPAN
    assert seq % C_TILE == 0 and tp % ROW_TILE == 0 and ts % ROW_TILE == 0

    w_in_r = _prep_w_in(w_in)
    wuq_r = _prep_w_uq(mla_w_uq)
    wabs, wuv, wuvt = _prep_w_ukv(mla_w_ukv)
    w_out_b, w1_b, w2_b = w_out.astype(BF16), mlp_w1.astype(BF16), mlp_w2.astype(BF16)
    pos = jnp.concatenate([jnp.tile(jnp.arange(seq), batch), jnp.tile(past + jnp.arange(dseq), dbatch)])
    cos, sin = _rope_tables(pos)
    dtb = _pad_lanes(ssm_dt_bias, at=G_DT)[:, None, :]
    alog = _pad_lanes(ssm_a_log, at=G_DT)[:, None, :]
    dsk = _pad_lanes(ssm_d)[:, None, :]
    gate_bias = (_pad_lanes(mlstm_b_i, at=G_I) + _pad_lanes(mlstm_b_f, at=G_F))[:, None, :]
    cache_k = cache_attn_k.reshape(depth, dbatch, A_SPAN, 256)
    cache_v = cache_attn_v.reshape(depth, dbatch, A_SPAN, 256)
    conv0_s = jnp.pad(state_ssm_conv, ((0, 0), (0, 0), (8 - (B_CONV - 1), 0), (0, 0)))
    caug0_s = jnp.concatenate([state_mlstm_c, state_mlstm_n[..., None],
                               jnp.zeros(state_mlstm_c.shape[:-1] + (LANE - HD - 1,), F32)], axis=-1)
    m0_s = _pad_lanes(state_mlstm_m)[:, :, None, :]
    conv0_p = jnp.zeros((batch, 8, B_CONV_DIM), F32)
    h0_p = jnp.zeros((batch, N_HEADS, HD, B_STATE), F32)
    caug0_p = jnp.zeros((batch, N_HEADS, HD, LANE), F32)
    m0_p = jnp.zeros((batch, 1, LANE), F32)

    xp, xs = x_prompt.reshape(tp, D_MODEL), x_sample.reshape(ts, D_MODEL)
    outs = [[] for _ in range(18)]
    ya, yb, yc, yd = (jnp.zeros((tp + ts, GROUP_W), BF16) for _ in range(4))
    fg = final_g[None, :]
    for l in range(depth):
        (aq, akv, bz, bxbc, qabs, qt, ckv, kr, kc, vt, mqkv, mo, gates) = _inproj(
            xp, xs, norm1_g[l][None], w_in_r, mla_q_norm_g[l][None], mla_kv_norm_g[l][None], wuq_r, wabs,
            cos, sin, l)

        ya = _attn_a_prompt(ya, aq, akv, _band_bias(attn_rel_bias[l], A_QBLOCK), batch, seq)
        ya = _attn_a_sample(ya, aq, akv, cache_k, cache_v, _band_bias(attn_rel_bias[l], CHUNK), l, tp, dbatch, dseq)

        ssd_w = (ssm_conv_w[l], ssm_conv_b[l][None], dtb[l], alog[l], dsk[l], ssm_norm_g[l][None])
        yb, h_p = _ssd(yb, bz, bxbc, gates, conv0_p, h0_p, *ssd_w, 0, batch, seq, lp)
        yb, h_s = _ssd(yb, bz, bxbc, gates, conv0_s[l], state_ssm[l], *ssd_w, tp, dbatch, dseq, dseq)

        yc = _mla_prompt(yc, qt, kc, vt, wuvt[l], batch, seq)
        yc = _mla_sample(yc, qabs, kc, cache_mla_ckv, cache_mla_kr, wuv[l], l, tp, dbatch, dseq)

        ml_w = (gate_bias[l], mlstm_norm_g[l][None])
        yd, c_p, m_p = _mlstm(yd, mqkv, mo, gates, caug0_p, m0_p, *ml_w, 0, batch, seq, lp)
        yd, c_s, m_s = _mlstm(yd, mqkv, mo, gates, caug0_s[l], m0_s[l], *ml_w, tp, dbatch, dseq, dseq)

        xp, xs = _outmlp(xp, xs, ya, yb, yc, yd, w_out_b, norm2_g[l][None], w1_b, w2_b, fg, l, l == depth - 1, l > 0)

        keep = min(A_SPAN, seq)
        tail = lambda a, n: jnp.stack([a[(b + 1) * seq - n:(b + 1) * seq] for b in range(batch)])
        akv_p = tail(akv, keep).reshape(batch, keep, 2, N_HEADS, HD)
        akv_s = akv[tp:].reshape(dbatch, dseq, 2, N_HEADS, HD)
        new = (akv_p[:, :, 0], akv_p[:, :, 1], tail(bxbc, B_CONV - 1), h_p,
               ckv[:tp].reshape(batch, seq, C_KV_LORA), kr[:tp].reshape(batch, seq, C_ROPE),
               c_p[..., :HD], c_p[..., HD], m_p[:, 0, :N_HEADS],
               jnp.concatenate([cache_attn_k[l][:, dseq:], akv_s[:, :, 0]], axis=1),
               jnp.concatenate([cache_attn_v[l][:, dseq:], akv_s[:, :, 1]], axis=1),
               bxbc[tp:].reshape(dbatch, dseq, B_CONV_DIM)[:, dseq - (B_CONV - 1):], h_s,
               ckv[tp:].reshape(dbatch, dseq, C_KV_LORA), kr[tp:].reshape(dbatch, dseq, C_ROPE),
               c_s[..., :HD], c_s[..., HD], m_s[:, 0, :N_HEADS])
        for dst, v in zip(outs, new):
            dst.append(v)

    y_prompt = xp.reshape(batch, seq, D_MODEL)
    y_sample = xs.reshape(dbatch, dseq, D_MODEL)
    return (y_prompt, y_sample) + tuple(jnp.stack(v) for v in outs)
```

```python
import functools
import math

import jax
import jax.numpy as jnp
import numpy as np
from jax import lax
from jax.experimental import pallas as pl
from jax.experimental.pallas import tpu as pltpu

F32 = jnp.float32
BF16 = jnp.bfloat16
EPS = 1e-6
NEG = -1e30
ROPE_BASE = 10000.0

CHUNK = 64
D_MODEL = 1024
GROUP_W = 256
D_FF = 4096
N_HEADS = 4
HD = 64
A_BAND_CHUNKS = 8
A_SPAN = A_BAND_CHUNKS * CHUNK
A_REL_CLIP = 128
B_GROUPS = 2
B_STATE = 64
B_CONV = 4
B_CONV_DIM = 512
C_NOPE = 64
C_ROPE = 32
C_KV_LORA = 128
C_SCALE = (C_NOPE + C_ROPE) ** -0.5
C_QSCALE = C_SCALE * math.log2(math.e)
VT_ROWS = 144
LANE = 128
VMEM_LIMIT = 56 * 1024 * 1024

OFF_A, OFF_BZ, OFF_XBC, OFF_CQ, OFF_CKV, OFF_KR, OFF_KRR, OFF_MQKV, OFF_MO, OFF_G, N_PROJ = (
    0, 768, 1024, 1536, 1792, 1920, 2048, 2176, 2944, 3200, 3328)
G_DT, G_I, G_F = 0, 4, 8
GT_ROWS = 16

ROW_TILE = 512
A_QBLOCK = 256
C_TILE = 256
C_UNROLL_PAIRS = 4
REC_CHUNK_PROMPT = 256


def _cparams(sem):
    return pltpu.CompilerParams(dimension_semantics=sem, vmem_limit_bytes=VMEM_LIMIT)


def _rms(x, g):
    return x * lax.rsqrt(jnp.mean(x * x, axis=-1, keepdims=True) + EPS) * g


def _dot(a, b):
    return jnp.dot(a, b, preferred_element_type=F32)


def _dot_nt(a, b):
    return lax.dot_general(a, b, (((1,), (1,)), ((), ())), preferred_element_type=F32)


def _dot_tn(a, b):
    return lax.dot_general(a, b, (((0,), (0,)), ((), ())), preferred_element_type=F32)


def _cumsum_rows(x, tri_bf16):
    hi = x.astype(BF16)
    r1 = x - hi.astype(F32)
    mid = r1.astype(BF16)
    lo = (r1 - mid.astype(F32)).astype(BF16)
    return _dot(tri_bf16, hi) + _dot(tri_bf16, mid) + _dot(tri_bf16, lo)


def _log1p_exp_neg_abs(x):
    return jnp.log1p(jnp.exp(-jnp.abs(x)))


def _full_spec(shape):
    n = len(shape)
    return pl.BlockSpec(shape, lambda *_: (0,) * n)


_DST_SPEC = pl.BlockSpec(memory_space=pl.ANY)


def _const_spec(shape):
    n = len(shape)
    return pl.BlockSpec(shape, lambda *_: (0,) * n, pipeline_mode=pl.Buffered(1))


def _layer_spec(shape, layer):
    n = len(shape)
    return pl.BlockSpec((None,) + tuple(shape), lambda *_: (layer,) + (0,) * n, pipeline_mode=pl.Buffered(1))


def _split_rows(n_prompt_tiles, width):
    npt = n_prompt_tiles
    return (pl.BlockSpec((ROW_TILE, width), lambda i: (jnp.minimum(i, npt - 1), 0)),
            pl.BlockSpec((ROW_TILE, width), lambda i: (jnp.maximum(i - npt, 0), 0)))


def _inproj_kernel(xp_ref, xs_ref, g1_ref, w_ref, qg_ref, kvg_ref, wuq_ref, wabs_ref, cos_ref, sin_ref,
                   aq_ref, akv_ref, bz_ref, bxbc_ref, qabs_ref, qt_ref, ckv_ref, kr_ref, kc_ref, vt_ref,
                   mqkv_ref, mo_ref, gates_ref, mkt_ref, mvt_ref, gt_ref, *, n_prompt_tiles):
    tm = xp_ref.shape[0]
    is_prompt = pl.program_id(0) < n_prompt_tiles
    x = jnp.where(is_prompt, xp_ref[...], xs_ref[...])
    hn = _rms(x, g1_ref[...]).astype(BF16)

    def seg(a, b):
        return _dot(hn, w_ref[:, a:b])

    a = seg(OFF_A, OFF_BZ)
    aq_ref[...] = (a[:, :GROUP_W] * HD ** -0.5).astype(BF16)
    akv_ref[...] = a[:, GROUP_W:]
    bz_ref[...] = seg(OFF_BZ, OFF_XBC)
    bxbc_ref[...] = seg(OFF_XBC, OFF_CQ)

    cos = cos_ref[...]
    sin = sin_ref[...]
    cqn = _rms(seg(OFF_CQ, OFF_CKV), qg_ref[...]).astype(BF16)
    qf = _dot(cqn, wuq_ref[...])
    qrope = (qf[:, 256:384] * cos + qf[:, 384:512] * sin) * C_QSCALE
    qlat = _dot(qf[:, :256].astype(BF16), wabs_ref[...]) * C_QSCALE
    lane_head = lax.broadcasted_iota(jnp.int32, (tm, LANE), 1) // C_ROPE
    qhs = [jnp.concatenate([qlat[:, h * LANE:(h + 1) * LANE], jnp.where(lane_head == h, qrope, 0.0)], axis=-1)
           for h in range(N_HEADS)]

    ckv = _rms(seg(OFF_CKV, OFF_KR), kvg_ref[...])
    kr4 = seg(OFF_KR, OFF_KRR) * cos + seg(OFF_KRR, OFF_MQKV) * sin
    ckv_ref[...] = ckv
    kr_ref[...] = kr4[:, :C_ROPE]
    kc_ref[...] = jnp.concatenate([ckv, kr4], axis=-1).astype(BF16)

    m = seg(OFF_MQKV, OFF_MO)
    mk = m[:, 256:512] * HD ** -0.5
    mqkv_ref[:, 0:256] = m[:, 0:256].astype(BF16)
    mqkv_ref[:, 256:512] = mk.astype(BF16)
    mqkv_ref[:, 512:768] = m[:, 512:768].astype(BF16)
    mo_ref[...] = seg(OFF_MO, OFF_G)
    g = seg(OFF_G, N_PROJ)
    gates_ref[...] = g

    @pl.when(jnp.logical_not(is_prompt))
    def _():
        for h in range(N_HEADS):
            qabs_ref[h] = qhs[h].astype(BF16)

    @pl.when(is_prompt)
    def _():
        for h in range(N_HEADS):
            qt_ref[h] = qhs[h].T.astype(BF16)
        ones_rows = (lax.broadcasted_iota(jnp.int32, (VT_ROWS - C_KV_LORA, C_TILE), 0) == 0).astype(BF16)
        for s in range(tm // C_TILE):
            vt_ref[s, 0:C_KV_LORA, :] = ckv[s * C_TILE:(s + 1) * C_TILE, :].T.astype(BF16)
            vt_ref[s, C_KV_LORA:VT_ROWS, :] = ones_rows
        mkt = mk.T
        mvt = m[:, 512:768].T
        ones_v = (lax.broadcasted_iota(jnp.int32, (LANE - HD, tm), 0) == 0).astype(BF16)
        for h in range(N_HEADS):
            mkt_ref[h] = mkt[h * HD:(h + 1) * HD].astype(BF16)
            mvt_ref[h, 0:HD, :] = mvt[h * HD:(h + 1) * HD].astype(BF16)
            mvt_ref[h, HD:LANE, :] = ones_v
        gt_ref[...] = g.T[0:GT_ROWS]


def _inproj(xp, xs, g1, w, qg, kvg, wuq, wabs, cos, sin, layer):
    tp, ts = xp.shape[0], xs.shape[0]
    t = tp + ts
    tm = ROW_TILE
    npt = tp // tm
    pcol = lambda *lead: pl.BlockSpec((*lead, tm), lambda i: (0,) * len(lead) + (jnp.minimum(i, npt - 1),))
    row = lambda n: pl.BlockSpec((tm, n), lambda i: (i, 0))
    out_shape = (
        jax.ShapeDtypeStruct((t, 256), BF16),
        jax.ShapeDtypeStruct((t, 512), F32),
        jax.ShapeDtypeStruct((t, 256), F32),
        jax.ShapeDtypeStruct((t, 512), F32),
        jax.ShapeDtypeStruct((N_HEADS, ts, 256), BF16),
        jax.ShapeDtypeStruct((N_HEADS, 256, tp), BF16),
        jax.ShapeDtypeStruct((t, 128), F32),
        jax.ShapeDtypeStruct((t, C_ROPE), F32),
        jax.ShapeDtypeStruct((t, 256), BF16),
        jax.ShapeDtypeStruct((tp // C_TILE, VT_ROWS, C_TILE), BF16),
        jax.ShapeDtypeStruct((t, 768), BF16),
        jax.ShapeDtypeStruct((t, 256), F32),
        jax.ShapeDtypeStruct((t, LANE), F32),
        jax.ShapeDtypeStruct((N_HEADS, HD, tp), BF16),
        jax.ShapeDtypeStruct((N_HEADS, LANE, tp), BF16),
        jax.ShapeDtypeStruct((GT_ROWS, tp), F32),
    )
    out_specs = (
        row(256), row(512), row(256), row(512),
        pl.BlockSpec((N_HEADS, tm, 256), lambda i: (0, jnp.maximum(i - npt, 0), 0)),
        pcol(N_HEADS, 256),
        row(128), row(C_ROPE), row(256),
        pl.BlockSpec((tm // C_TILE, VT_ROWS, C_TILE), lambda i: (jnp.minimum(i, npt - 1), 0, 0)),
        row(768), row(256), row(LANE),
        pcol(N_HEADS, HD), pcol(N_HEADS, LANE), pcol(GT_ROWS),
    )
    in_specs = [*_split_rows(npt, D_MODEL), _full_spec((1, D_MODEL)), _layer_spec((D_MODEL, N_PROJ), layer),
                _full_spec((1, 256)), _full_spec((1, 128)), _layer_spec((256, 512), layer),
                _layer_spec((256, 512), layer), row(LANE), row(LANE)]
    return pl.pallas_call(
        functools.partial(_inproj_kernel, n_prompt_tiles=npt), grid=(t // tm,), in_specs=in_specs,
        out_specs=out_specs, out_shape=out_shape,
        compiler_params=_cparams(("arbitrary",)), name="inproj",
    )(xp, xs, g1, w, qg, kvg, wuq, wabs, cos, sin)


def _band_heads(q, kb, vb, bias_ref, valid, o_ref):
    for h in range(N_HEADS):
        sl = slice(h * HD, (h + 1) * HD)
        s = _dot_nt(q[:, sl], kb[:, sl]) + bias_ref[h]
        if valid is not None:
            s = jnp.where(valid, s, NEG)
        m = jnp.max(s, axis=-1, keepdims=True)
        p = jnp.exp(s - m)
        l = jnp.sum(p, axis=-1, keepdims=True)
        o = _dot(p.astype(BF16), vb[:, sl]) / l
        o_ref[:, sl] = o.astype(o_ref.dtype)


def _attn_a_prompt_kernel(dst_ref, q_ref, k0_ref, k1_ref, k2_ref, bias_ref, o_ref):
    i = pl.program_id(1)
    kv = jnp.concatenate([k0_ref[...], k1_ref[...], k2_ref[...]], axis=0)
    kb = kv[:, :GROUP_W].astype(BF16)
    vb = kv[:, GROUP_W:].astype(BF16)
    col = lax.broadcasted_iota(jnp.int32, (A_QBLOCK, 3 * A_QBLOCK), 1)
    valid = col >= (2 - i) * A_QBLOCK
    _band_heads(q_ref[...], kb, vb, bias_ref, valid, o_ref)


def _attn_a_prompt(dst, aq, akv, biasm, batch, seq):
    nb = seq // A_QBLOCK
    qb = A_QBLOCK

    def kspec(back):
        return pl.BlockSpec((qb, 512), lambda b, i: (b * nb + jnp.maximum(i - back, 0), 0))

    return pl.pallas_call(
        _attn_a_prompt_kernel, grid=(batch, nb),
        in_specs=[_DST_SPEC, pl.BlockSpec((qb, 256), lambda b, i: (b * nb + i, 0)), kspec(2), kspec(1), kspec(0),
                  _full_spec(biasm.shape)],
        out_specs=pl.BlockSpec((qb, 256), lambda b, i: (b * nb + i, 0)),
        out_shape=jax.ShapeDtypeStruct(dst.shape, dst.dtype), input_output_aliases={0: 0},
        compiler_params=_cparams(("parallel", "parallel")), name="attn_a_prompt",
    )(dst, aq, akv, akv, akv, biasm)


def _attn_a_sample_kernel(dst_ref, q_ref, ck_ref, cv_ref, kvn_ref, bias_ref, o_ref):
    kvn = kvn_ref[...]
    kb = jnp.concatenate([ck_ref[...], kvn[:, :GROUP_W]], axis=0).astype(BF16)
    vb = jnp.concatenate([cv_ref[...], kvn[:, GROUP_W:]], axis=0).astype(BF16)
    _band_heads(q_ref[...], kb, vb, bias_ref, None, o_ref)


def _attn_a_sample(dst, aq, akv, cache_k, cache_v, bias, layer, row0, n_seq, seq):
    blk0 = row0 // seq
    cspec = pl.BlockSpec((None, None, A_SPAN, 256), lambda b: (layer, b, 0, 0))
    return pl.pallas_call(
        _attn_a_sample_kernel, grid=(n_seq,),
        in_specs=[_DST_SPEC, pl.BlockSpec((seq, 256), lambda b: (blk0 + b, 0)), cspec, cspec,
                  pl.BlockSpec((seq, 512), lambda b: (blk0 + b, 0)), _full_spec(bias.shape)],
        out_specs=pl.BlockSpec((seq, 256), lambda b: (blk0 + b, 0)),
        out_shape=jax.ShapeDtypeStruct(dst.shape, dst.dtype), input_output_aliases={0: 0},
        compiler_params=_cparams(("parallel",)), name="attn_a_sample",
    )(dst, aq, cache_k, cache_v, akv, bias)


def _ssd_kernel(dst_ref, z_ref, xbc_ref, g_ref, conv0_ref, h0_ref, cw_ref, cb_ref, dtb_ref, alog_ref, dsk_ref, ng_ref,
                y_ref, hout_ref, xpad_ref, h_ref):
    L = z_ref.shape[0]
    c = pl.program_id(1)

    @pl.when(c == 0)
    def _():
        xpad_ref[0:8, :] = conv0_ref[...]
        h_ref[...] = h0_ref[...]

    xpad_ref[8:8 + L, :] = xbc_ref[...]
    conv = cb_ref[...]
    for j in range(B_CONV):
        conv = conv + xpad_ref[5 + j:5 + j + L, :] * cw_ref[j:j + 1, :]
    xpad_ref[0:8, :] = xpad_ref[L:L + 8, :]
    u = conv * jax.nn.sigmoid(conv)

    g = g_ref[...] + dtb_ref[...]
    dt = jnp.maximum(g, 0.0) + _log1p_exp_neg_abs(g)
    da = dt * (-jnp.exp(alog_ref[...]))
    row = lax.broadcasted_iota(jnp.int32, (L, L), 0)
    colm = lax.broadcasted_iota(jnp.int32, (L, L), 1)
    tri = row >= colm
    cs = _cumsum_rows(da, tri.astype(BF16))
    cs_t = cs.T

    ys = []
    gmat = [None] * B_GROUPS
    for h in range(N_HEADS):
        grp = h // (N_HEADS // B_GROUPS)
        xs = u[:, h * HD:(h + 1) * HD]
        bm = u[:, 256 + grp * B_STATE:256 + (grp + 1) * B_STATE].astype(BF16)
        cm = u[:, 384 + grp * B_STATE:384 + (grp + 1) * B_STATE].astype(BF16)
        if gmat[grp] is None:
            gmat[grp] = _dot_nt(cm, bm)
        csc = cs[:, G_DT + h:G_DT + h + 1]
        csr = cs_t[G_DT + h:G_DT + h + 1, :]
        tot = cs[L - 1:L, G_DT + h:G_DT + h + 1]
        dec = jnp.exp(jnp.where(tri, csc - csr, NEG))
        xdt = xs * dt[:, G_DT + h:G_DT + h + 1]
        hprev = h_ref[h]
        y = _dot((gmat[grp] * dec).astype(BF16), xdt.astype(BF16))
        y = y + _dot_nt(cm, hprev.astype(BF16)) * jnp.exp(csc)
        y = y + dsk_ref[0:1, h:h + 1] * xs
        ys.append(y)
        wend = (xdt * jnp.exp(tot - csc)).astype(BF16)
        h_ref[h] = jnp.exp(tot) * hprev + _dot_tn(wend, bm)
    y = jnp.concatenate(ys, axis=-1)
    z = z_ref[...]
    y = y * (z * jax.nn.sigmoid(z))
    y_ref[...] = _rms(y, ng_ref[...]).astype(y_ref.dtype)

    @pl.when(c == pl.num_programs(1) - 1)
    def _():
        hout_ref[...] = h_ref[...]


def _ssd(dst, bz, bxbc, gates, conv0, h0, cw, cb, dtb, alog, dsk, ng, row0, n_seq, seq, L):
    nc = seq // L
    blk0 = row0 // L
    rspec = lambda n: pl.BlockSpec((L, n), lambda b, c: (blk0 + b * nc + c, 0))
    return pl.pallas_call(
        _ssd_kernel, grid=(n_seq, nc),
        in_specs=[_DST_SPEC, rspec(256), rspec(512), rspec(LANE),
                  pl.BlockSpec((None, 8, 512), lambda b, c: (b, 0, 0)),
                  pl.BlockSpec((None, N_HEADS, HD, B_STATE), lambda b, c: (b, 0, 0, 0)),
                  _full_spec((B_CONV, 512)), _full_spec((1, 512)), _full_spec((1, LANE)), _full_spec((1, LANE)),
                  _full_spec((1, LANE)), _full_spec((1, 256))],
        out_specs=(rspec(256), pl.BlockSpec((None, N_HEADS, HD, B_STATE), lambda b, c: (b, 0, 0, 0))),
        input_output_aliases={0: 0},
        out_shape=(jax.ShapeDtypeStruct(dst.shape, dst.dtype),
                   jax.ShapeDtypeStruct((n_seq, N_HEADS, HD, B_STATE), F32)),
        scratch_shapes=[pltpu.VMEM((L + 8, 512), F32), pltpu.VMEM((N_HEADS, HD, B_STATE), F32)],
        compiler_params=_cparams(("parallel", "arbitrary")), name="ssd",
    )(dst, bz, bxbc, gates, conv0, h0, cw, cb, dtb, alog, dsk, ng)


def _mlstm_kernel(dst_ref, qkv_ref, o_ref, g_ref, c0_ref, m0_ref, bias_ref, ng_ref,
                  y_ref, cout_ref, mout_ref, c_ref, m_ref):
    L = qkv_ref.shape[0]
    c = pl.program_id(1)

    @pl.when(c == 0)
    def _():
        c_ref[...] = c0_ref[...]
        m_ref[...] = m0_ref[...]

    g = g_ref[...] + bias_ref[...]
    lf = jnp.minimum(g, 0.0) - _log1p_exp_neg_abs(g)
    row = lax.broadcasted_iota(jnp.int32, (L, L), 0)
    colm = lax.broadcasted_iota(jnp.int32, (L, L), 1)
    tri = row >= colm
    bcum = _cumsum_rows(lf, tri.astype(BF16))
    bcum_t = bcum.T
    g_t = g.T
    unit = (lax.broadcasted_iota(jnp.int32, (L, HD), 1) == 0).astype(BF16)
    ng = ng_ref[...]
    og = o_ref[...]

    for h in range(N_HEADS):
        sl = slice(h * HD, (h + 1) * HD)
        q = qkv_ref[:, h * HD:(h + 1) * HD]
        k = qkv_ref[:, 256 + h * HD:256 + (h + 1) * HD]
        v = qkv_ref[:, 512 + h * HD:512 + (h + 1) * HD]
        vaug = jnp.concatenate([v, unit], axis=-1)
        bc = bcum[:, G_F + h:G_F + h + 1]
        br = bcum_t[G_F + h:G_F + h + 1, :]
        ic = g[:, G_I + h:G_I + h + 1]
        ir = g_t[G_I + h:G_I + h + 1, :]
        tot = bcum[L - 1:L, G_F + h:G_F + h + 1]
        mprev = m_ref[0:1, h:h + 1]

        dm = jnp.where(tri, bc - br + ir, NEG)
        inter = bc + mprev
        mt = jnp.maximum(inter, jnp.max(dm, axis=-1, keepdims=True))
        w_intra = jnp.exp(dm - mt)
        w_inter = jnp.exp(inter - mt)
        caug = c_ref[h]
        s = _dot_nt(q, k) * w_intra
        r = w_inter * _dot(q, caug.astype(BF16)) + _dot(s.astype(BF16), vaug)
        num = r[:, :HD]
        den = r[:, HD:HD + 1]
        hout = num / jnp.maximum(jnp.abs(den), jnp.exp(-mt))

        gend = tot - bc + ic
        mnew = jnp.maximum(tot + mprev, jnp.max(gend, axis=0, keepdims=True))
        kw = (k.astype(F32) * jnp.exp(gend - mnew)).astype(BF16)
        c_ref[h] = jnp.exp(tot + mprev - mnew) * caug + _dot_tn(kw, vaug)
        m_ref[0:1, h:h + 1] = mnew

        yh = _rms(hout, ng[:, sl]) * jax.nn.sigmoid(og[:, sl])
        y_ref[:, sl] = yh.astype(y_ref.dtype)

    @pl.when(c == pl.num_programs(1) - 1)
    def _():
        cout_ref[...] = c_ref[...]
        mout_ref[...] = m_ref[...]


def _cumsum_lanes(x, tri_bf16):
    hi = x.astype(BF16)
    r1 = x - hi.astype(F32)
    mid = r1.astype(BF16)
    lo = (r1 - mid.astype(F32)).astype(BF16)
    return _dot(hi, tri_bf16) + _dot(mid, tri_bf16) + _dot(lo, tri_bf16)


def _mlstm_prompt_kernel(dst_ref, qkv_ref, kt_ref, vt_ref, o_ref, g_ref, gt_ref, bias_ref, biast_ref, ng_ref,
                         y_ref, cout_ref, mout_ref, c_ref, m_ref):
    L = qkv_ref.shape[0]
    c = pl.program_id(1)

    @pl.when(c == 0)
    def _():
        c_ref[...] = jnp.zeros_like(c_ref)
        m_ref[...] = jnp.zeros_like(m_ref)

    g = g_ref[...] + bias_ref[...]
    gt = gt_ref[...] + biast_ref[...]
    lf = jnp.minimum(g, 0.0) - _log1p_exp_neg_abs(g)
    lft = jnp.minimum(gt, 0.0) - _log1p_exp_neg_abs(gt)
    row = lax.broadcasted_iota(jnp.int32, (L, L), 0)
    colm = lax.broadcasted_iota(jnp.int32, (L, L), 1)
    s_le_l = row <= colm
    bcum = _cumsum_rows(lf, (row >= colm).astype(BF16))
    bcum_t = _cumsum_lanes(lft, s_le_l.astype(BF16))

    yts = []
    for h in range(N_HEADS):
        q = qkv_ref[:, h * HD:(h + 1) * HD]
        k = qkv_ref[:, 256 + h * HD:256 + (h + 1) * HD]
        src = g[:, G_I + h:G_I + h + 1] - bcum[:, G_F + h:G_F + h + 1]
        bc = bcum_t[G_F + h:G_F + h + 1, :]
        ir = gt[G_I + h:G_I + h + 1, :]
        tot = bcum_t[G_F + h:G_F + h + 1, L - 1:L]
        mprev = m_ref[0:1, h:h + 1]

        dmt = jnp.where(s_le_l, bc + src, NEG)
        inter = bc + mprev
        mt = jnp.maximum(inter, jnp.max(dmt, axis=0, keepdims=True))
        w_intra = jnp.exp(dmt - mt)
        w_inter = jnp.exp(inter - mt)
        ct = c_ref[h]
        st = _dot_nt(k, q) * w_intra
        rt = w_inter * _dot_nt(ct.astype(BF16), q) + _dot(vt_ref[h], st.astype(BF16))
        hout = rt[:HD] / jnp.maximum(jnp.abs(rt[HD:HD + 1]), jnp.exp(-mt))
        yts.append(hout * lax.rsqrt(jnp.mean(hout * hout, axis=0, keepdims=True) + EPS))

        gend = tot - bc + ir
        mnew = jnp.maximum(tot + mprev, jnp.max(gend, axis=-1, keepdims=True))
        kw = (kt_ref[h].astype(F32) * jnp.exp(gend - mnew)).astype(BF16)
        c_ref[h] = jnp.exp(tot + mprev - mnew) * ct + _dot_nt(vt_ref[h], kw)
        m_ref[0:1, h:h + 1] = mnew

    y = jnp.concatenate(yts, axis=0).T
    y_ref[...] = (y * ng_ref[...] * jax.nn.sigmoid(o_ref[...])).astype(y_ref.dtype)

    @pl.when(c == pl.num_programs(1) - 1)
    def _():
        cout_ref[...] = c_ref[...]
        mout_ref[...] = m_ref[...]


def _mlstm_prompt(dst, mqkv, mkt, mvt, mo, gates, gt, bias, biast, ng, n_seq, seq, L):
    nc = seq // L
    rspec = lambda n: pl.BlockSpec((L, n), lambda b, c: (b * nc + c, 0))
    cspec = lambda *lead: pl.BlockSpec((*lead, L), lambda b, c: (0,) * len(lead) + (b * nc + c,))
    stspec = pl.BlockSpec((None, N_HEADS, LANE, HD), lambda b, c: (b, 0, 0, 0))
    mspec = pl.BlockSpec((None, 1, LANE), lambda b, c: (b, 0, 0))
    return pl.pallas_call(
        _mlstm_prompt_kernel, grid=(n_seq, nc),
        in_specs=[_DST_SPEC, rspec(768), cspec(N_HEADS, HD), cspec(N_HEADS, LANE), rspec(256), rspec(LANE),
                  cspec(GT_ROWS), _full_spec((1, LANE)), _full_spec((GT_ROWS, 1)), _full_spec((1, 256))],
        out_specs=(rspec(256), stspec, mspec), input_output_aliases={0: 0},
        out_shape=(jax.ShapeDtypeStruct(dst.shape, dst.dtype),
                   jax.ShapeDtypeStruct((n_seq, N_HEADS, LANE, HD), F32),
                   jax.ShapeDtypeStruct((n_seq, 1, LANE), F32)),
        scratch_shapes=[pltpu.VMEM((N_HEADS, LANE, HD), F32), pltpu.VMEM((1, LANE), F32)],
        compiler_params=_cparams(("parallel", "arbitrary")), name="mlstm_prompt",
    )(dst, mqkv, mkt, mvt, mo, gates, gt, bias, biast, ng)


def _mlstm(dst, mqkv, mo, gates, c0, m0, bias, ng, row0, n_seq, seq, L):
    nc = seq // L
    blk0 = row0 // L
    rspec = lambda n: pl.BlockSpec((L, n), lambda b, c: (blk0 + b * nc + c, 0))
    cspec = pl.BlockSpec((None, N_HEADS, HD, LANE), lambda b, c: (b, 0, 0, 0))
    mspec = pl.BlockSpec((None, 1, LANE), lambda b, c: (b, 0, 0))
    return pl.pallas_call(
        _mlstm_kernel, grid=(n_seq, nc),
        in_specs=[_DST_SPEC, rspec(768), rspec(256), rspec(LANE), cspec, mspec, _full_spec((1, LANE)),
                  _full_spec((1, 256))],
        out_specs=(rspec(256), cspec, mspec), input_output_aliases={0: 0},
        out_shape=(jax.ShapeDtypeStruct(dst.shape, dst.dtype),
                   jax.ShapeDtypeStruct((n_seq, N_HEADS, HD, LANE), F32),
                   jax.ShapeDtypeStruct((n_seq, 1, LANE), F32)),
        scratch_shapes=[pltpu.VMEM((N_HEADS, HD, LANE), F32), pltpu.VMEM((1, LANE), F32)],
        compiler_params=_cparams(("parallel", "arbitrary")), name="mlstm",
    )(dst, mqkv, mo, gates, c0, m0, bias, ng)


def _mla_prompt_kernel(dst_ref, q_ref, kc_ref, vt_ref, wuvt_ref, o_ref, acc_ref, m_ref, s_ref):
    tq = q_ref.shape[2]
    tk = C_TILE
    i = pl.program_id(1)
    m_ref[...] = jnp.full_like(m_ref, NEG)
    acc_ref[...] = jnp.zeros_like(acc_ref)

    def scores(j, slot):
        start = pl.multiple_of(j * tk, tk)
        kt = kc_ref[pl.ds(start, tk), :]
        for h in range(N_HEADS):
            s_ref[slot, h] = _dot(kt, q_ref[h])

    def consume(j, slot, masked):
        vt = vt_ref[j]
        for h in range(N_HEADS):
            s = s_ref[slot, h]
            if masked:
                krow = lax.broadcasted_iota(jnp.int32, (tk, tq), 0) // CHUNK
                qcol = lax.broadcasted_iota(jnp.int32, (tk, tq), 1) // CHUNK
                s = jnp.where(krow <= qcol, s, NEG)
            m_old = m_ref[h]
            m_new = jnp.maximum(m_old, jnp.max(s, axis=0, keepdims=True))
            p = jnp.exp2(s - m_new).astype(BF16)
            acc_ref[h] = jnp.exp2(m_old - m_new) * acc_ref[h] + _dot(vt, p)
            m_ref[h] = m_new

    scores(0, 0)

    def pair(j):
        scores(j + 1, 1)
        consume(j, 0, False)
        scores(j + 2, 0)
        consume(j + 1, 1, False)

    def body_main(jj, carry):
        for u in range(C_UNROLL_PAIRS):
            pair(2 * (C_UNROLL_PAIRS * jj + u))
        return carry

    def body_rest(jj, carry):
        pair(2 * (C_UNROLL_PAIRS * n_main + jj))
        return carry

    n_main = i // (2 * C_UNROLL_PAIRS)
    lax.fori_loop(0, n_main, body_main, 0)
    lax.fori_loop(0, i // 2 - C_UNROLL_PAIRS * n_main, body_rest, 0)

    @pl.when(i % 2 == 1)
    def _():
        scores(i, 1)
        consume(i - 1, 0, False)
        consume(i, 1, True)

    @pl.when(i % 2 == 0)
    def _():
        consume(i, 0, True)

    ys = []
    for h in range(N_HEADS):
        acc = acc_ref[h]
        o = (acc[:C_KV_LORA] / acc[C_KV_LORA:C_KV_LORA + 1]).astype(BF16)
        ys.append(_dot(wuvt_ref[h], o))
    o_ref[...] = jnp.concatenate(ys, axis=0).T.astype(o_ref.dtype)


def _mla_prompt(dst, qt, kc, vt, wuvt, batch, seq):
    tq = C_TILE
    nq = seq // tq
    return pl.pallas_call(
        _mla_prompt_kernel, grid=(batch, nq),
        in_specs=[_DST_SPEC, pl.BlockSpec((N_HEADS, 256, tq), lambda b, i: (0, 0, b * nq + i)),
                  pl.BlockSpec((seq, 256), lambda b, i: (b, 0)),
                  pl.BlockSpec((seq // C_TILE, VT_ROWS, C_TILE), lambda b, i: (b, 0, 0)),
                  _full_spec((N_HEADS, HD, 128))],
        out_specs=pl.BlockSpec((tq, 256), lambda b, i: (b * nq + i, 0)),
        out_shape=jax.ShapeDtypeStruct(dst.shape, dst.dtype), input_output_aliases={0: 0},
        scratch_shapes=[pltpu.VMEM((N_HEADS, VT_ROWS, tq), F32), pltpu.VMEM((N_HEADS, 1, tq), F32),
                        pltpu.VMEM((2, N_HEADS, C_TILE, tq), F32)],
        compiler_params=_cparams(("parallel", "arbitrary")), name="mla_prompt",
    )(dst, qt, kc, vt, wuvt)


def _mla_sample_kernel(dst_ref, q_ref, ckv_ref, kr_ref, knew_ref, wuv_ref, o_ref):
    seq = q_ref.shape[1]
    qs = q_ref[...].reshape(N_HEADS * seq, 2 * LANE)
    kcache = jnp.concatenate([ckv_ref[...]] + [kr_ref[...]] * N_HEADS, axis=-1).astype(BF16)
    kall = jnp.concatenate([kcache, knew_ref[...]], axis=0)
    s = _dot_nt(qs, kall)
    m = jnp.max(s, axis=-1, keepdims=True)
    p = jnp.exp2(s - m)
    l = jnp.sum(p, axis=-1, keepdims=True)
    o = (_dot(p.astype(BF16), kall[:, :C_KV_LORA]) / l).astype(BF16)
    for h in range(N_HEADS):
        o_ref[:, h * HD:(h + 1) * HD] = _dot(o[h * seq:(h + 1) * seq], wuv_ref[h]).astype(o_ref.dtype)


def _mla_sample(dst, qabs, kc, cache_ckv, cache_kr, wuv, layer, row0, n_seq, seq):
    blk0 = row0 // seq
    past = cache_ckv.shape[2]
    return pl.pallas_call(
        _mla_sample_kernel, grid=(n_seq,),
        in_specs=[_DST_SPEC, pl.BlockSpec((N_HEADS, seq, 256), lambda b: (0, b, 0)),
                  pl.BlockSpec((None, None, past, C_KV_LORA), lambda b: (layer, b, 0, 0)),
                  pl.BlockSpec((None, None, past, C_ROPE), lambda b: (layer, b, 0, 0)),
                  pl.BlockSpec((seq, 256), lambda b: (blk0 + b, 0)),
                  _full_spec((N_HEADS, 128, HD))],
        out_specs=pl.BlockSpec((seq, 256), lambda b: (blk0 + b, 0)),
        out_shape=jax.ShapeDtypeStruct(dst.shape, dst.dtype), input_output_aliases={0: 0},
        compiler_params=_cparams(("parallel",)), name="mla_sample",
    )(dst, qabs, cache_ckv, cache_kr, kc, wuv)


def _outmlp_kernel(xp_ref, xs_ref, ya_ref, yb_ref, yc_ref, yd_ref, wout_ref, g2_ref, w1_ref, w2_ref, fg_ref,
                   op_ref, os_ref, *, final, n_prompt_tiles):
    i = pl.program_id(0)
    y = jnp.concatenate([ya_ref[...], yb_ref[...], yc_ref[...], yd_ref[...]], axis=-1)
    x1 = jnp.where(i < n_prompt_tiles, xp_ref[...], xs_ref[...]) + _dot(y, wout_ref[...])
    hm = _rms(x1, g2_ref[...]).astype(BF16)
    acc = x1
    for c in range(D_FF // D_MODEL):
        sl = slice(c * D_MODEL, (c + 1) * D_MODEL)
        hc = jnp.square(jnp.maximum(_dot(hm, w1_ref[:, sl]), 0.0)).astype(BF16)
        acc = acc + _dot(hc, w2_ref[sl, :])
    if final:
        acc = _rms(acc, fg_ref[...])

    @pl.when(i < n_prompt_tiles)
    def _():
        op_ref[...] = acc

    @pl.when(i >= n_prompt_tiles)
    def _():
        os_ref[...] = acc


def _outmlp(xp, xs, ya, yb, yc, yd, wout, g2, w1, w2, fg, layer, final, in_place):
    t = xp.shape[0] + xs.shape[0]
    tm = ROW_TILE
    npt = xp.shape[0] // tm
    row = lambda n: pl.BlockSpec((tm, n), lambda i: (i, 0))
    xspecs = _split_rows(npt, D_MODEL)
    return pl.pallas_call(
        functools.partial(_outmlp_kernel, final=final, n_prompt_tiles=npt), grid=(t // tm,),
        in_specs=[*xspecs, row(256), row(256), row(256), row(256), _layer_spec((D_MODEL, D_MODEL), layer),
                  _full_spec((1, D_MODEL)), _layer_spec((D_MODEL, D_FF), layer), _layer_spec((D_FF, D_MODEL), layer),
                  _full_spec((1, D_MODEL))],
        out_specs=xspecs,
        out_shape=(jax.ShapeDtypeStruct(xp.shape, F32), jax.ShapeDtypeStruct(xs.shape, F32)),
        input_output_aliases={0: 0, 1: 1} if in_place else {},
        compiler_params=_cparams(("arbitrary",)), name="outmlp",
    )(xp, xs, ya, yb, yc, yd, wout, g2, w1, w2, fg)


def _rot_cols(w):
    half = C_ROPE // 2
    return jnp.concatenate([-w[..., half:], w[..., :half]], axis=-1)


def _pad_lanes(v, n=LANE, at=0):
    out = jnp.zeros(v.shape[:-1] + (n,), v.dtype)
    return out.at[..., at:at + v.shape[-1]].set(v)


def _prep_w_in(w):
    sizes = (256, 256, 256, 256, 512, 4, 256, 128, 32, 256, 256, 256, 4, 4, 256)
    cuts = [0]
    for s in sizes:
        cuts.append(cuts[-1] + s)
    (a_q, a_k, a_v, b_z, b_xbc, b_dt, c_q, c_kv, c_kr, m_q, m_k, m_v, m_i, m_f, m_o) = (
        w[..., cuts[n]:cuts[n + 1]] for n in range(len(sizes)))
    kr4 = jnp.concatenate([c_kr] * N_HEADS, axis=-1)
    krr4 = jnp.concatenate([_rot_cols(c_kr)] * N_HEADS, axis=-1)
    gates = _pad_lanes(jnp.concatenate([b_dt, m_i, m_f], axis=-1))
    out = jnp.concatenate([a_q, a_k, a_v, b_z, b_xbc, c_q, c_kv, kr4, krr4, m_q, m_k, m_v, m_o, gates], axis=-1)
    assert out.shape[-1] == N_PROJ
    return out.astype(BF16)


def _prep_w_uq(w):
    d = w.shape[0]
    w4 = w.reshape(d, 256, N_HEADS, C_NOPE + C_ROPE)
    nope = w4[..., :C_NOPE].reshape(d, 256, N_HEADS * C_NOPE)
    rope = w4[..., C_NOPE:]
    return jnp.concatenate([nope, rope.reshape(d, 256, N_HEADS * C_ROPE),
                            _rot_cols(rope).reshape(d, 256, N_HEADS * C_ROPE)], axis=-1).astype(BF16)


def _prep_w_ukv(w):
    d = w.shape[0]
    w4 = w.reshape(d, C_KV_LORA, N_HEADS, C_NOPE + HD)
    w_uk = w4[..., :C_NOPE]
    w_uv = w4[..., C_NOPE:]
    wabs = jnp.zeros((d, N_HEADS * C_NOPE, N_HEADS * C_KV_LORA), w.dtype)
    for h in range(N_HEADS):
        wabs = wabs.at[:, h * C_NOPE:(h + 1) * C_NOPE, h * C_KV_LORA:(h + 1) * C_KV_LORA].set(
            jnp.swapaxes(w_uk[:, :, h, :], 1, 2))
    wuv = jnp.transpose(w_uv, (0, 2, 1, 3))
    wuvt = jnp.transpose(w_uv, (0, 2, 3, 1))
    return wabs.astype(BF16), wuv.astype(BF16), wuvt.astype(BF16)


def _rope_tables(pos):
    half = C_ROPE // 2
    inv = jnp.exp(-math.log(ROPE_BASE) * jnp.arange(half, dtype=F32) / half)
    ang = pos.astype(F32)[:, None] * inv[None, :]
    reps = LANE // half
    return jnp.tile(jnp.cos(ang), (1, reps)), jnp.tile(jnp.sin(ang), (1, reps))


def _band_bias(table, q_rows):
    cols = q_rows + A_SPAN
    k = np.arange(q_rows + cols - 1)
    u = table[:, np.clip((q_rows - 1 - k) + A_SPAN, -A_REL_CLIP, A_REL_CLIP) + A_REL_CLIP].astype(F32)
    period = q_rows + cols
    u = jnp.pad(u, ((0, 0), (0, 1)))
    flat = jnp.tile(u, (1, q_rows))[:, q_rows - 1:q_rows - 1 + q_rows * (period - 1)]
    toep = flat.reshape(table.shape[0], q_rows, period - 1)[:, :, :cols]
    r = np.arange(q_rows)[:, None]
    s = np.arange(cols)[None, :] - (r // CHUNK) * CHUNK
    inband = (s >= 0) & (s < A_SPAN + CHUNK)
    return jnp.where(jnp.asarray(inband)[None], toep, NEG)


def kernel(x_prompt, x_sample, cache_attn_k, cache_attn_v, state_ssm_conv, state_ssm, cache_mla_ckv, cache_mla_kr,
           state_mlstm_c, state_mlstm_n, state_mlstm_m, norm1_g, w_in, attn_rel_bias, ssm_conv_w, ssm_conv_b,
           ssm_dt_bias, ssm_a_log, ssm_d, ssm_norm_g, mla_q_norm_g, mla_w_uq, mla_kv_norm_g, mla_w_ukv,
           mlstm_b_i, mlstm_b_f, mlstm_norm_g, w_out, norm2_g, mlp_w1, mlp_w2, final_g):
    batch, seq, _ = x_prompt.shape
    dbatch, dseq, _ = x_sample.shape
    depth = w_in.shape[0]
    past = cache_mla_ckv.shape[2]
    tp, ts = batch * seq, dbatch * dseq
    lp = min(REC_CHUNK_PROMPT, seq)
    assert dseq == CHUNK and past % CHUNK == 0 and cache_attn_k.shape[2] == A_SPAN
    assert seq % C_TILE == 0 and tp % ROW_TILE == 0 and ts % ROW_TILE == 0

    w_in_r = _prep_w_in(w_in)
    wuq_r = _prep_w_uq(mla_w_uq)
    wabs, wuv, wuvt = _prep_w_ukv(mla_w_ukv)
    w_out_b, w1_b, w2_b = w_out.astype(BF16), mlp_w1.astype(BF16), mlp_w2.astype(BF16)
    pos = jnp.concatenate([jnp.tile(jnp.arange(seq), batch), jnp.tile(past + jnp.arange(dseq), dbatch)])
    cos, sin = _rope_tables(pos)
    dtb = _pad_lanes(ssm_dt_bias, at=G_DT)[:, None, :]
    alog = _pad_lanes(ssm_a_log, at=G_DT)[:, None, :]
    dsk = _pad_lanes(ssm_d)[:, None, :]
    gate_bias = (_pad_lanes(mlstm_b_i, at=G_I) + _pad_lanes(mlstm_b_f, at=G_F))[:, None, :]
    gate_bias_t = jnp.swapaxes(gate_bias[:, :, :GT_ROWS], 1, 2)
    cache_k = cache_attn_k.reshape(depth, dbatch, A_SPAN, 256)
    cache_v = cache_attn_v.reshape(depth, dbatch, A_SPAN, 256)
    conv0_s = jnp.pad(state_ssm_conv, ((0, 0), (0, 0), (8 - (B_CONV - 1), 0), (0, 0)))
    caug0_s = jnp.concatenate([state_mlstm_c, state_mlstm_n[..., None],
                               jnp.zeros(state_mlstm_c.shape[:-1] + (LANE - HD - 1,), F32)], axis=-1)
    m0_s = _pad_lanes(state_mlstm_m)[:, :, None, :]
    conv0_p = jnp.zeros((batch, 8, B_CONV_DIM), F32)
    h0_p = jnp.zeros((batch, N_HEADS, HD, B_STATE), F32)

    xp, xs = x_prompt.reshape(tp, D_MODEL), x_sample.reshape(ts, D_MODEL)
    outs = [[] for _ in range(18)]
    ya, yb, yc, yd = (jnp.zeros((tp + ts, GROUP_W), BF16) for _ in range(4))
    fg = final_g[None, :]
    for l in range(depth):
        (aq, akv, bz, bxbc, qabs, qt, ckv, kr, kc, vt, mqkv, mo, gates, mkt, mvt, gt) = _inproj(
            xp, xs, norm1_g[l][None], w_in_r, mla_q_norm_g[l][None], mla_kv_norm_g[l][None], wuq_r, wabs,
            cos, sin, l)

        ya = _attn_a_prompt(ya, aq, akv, _band_bias(attn_rel_bias[l], A_QBLOCK), batch, seq)
        ya = _attn_a_sample(ya, aq, akv, cache_k, cache_v, _band_bias(attn_rel_bias[l], CHUNK), l, tp, dbatch, dseq)

        ssd_w = (ssm_conv_w[l], ssm_conv_b[l][None], dtb[l], alog[l], dsk[l], ssm_norm_g[l][None])
        yb, h_p = _ssd(yb, bz, bxbc, gates, conv0_p, h0_p, *ssd_w, 0, batch, seq, lp)
        yb, h_s = _ssd(yb, bz, bxbc, gates, conv0_s[l], state_ssm[l], *ssd_w, tp, dbatch, dseq, dseq)

        yc = _mla_prompt(yc, qt, kc, vt, wuvt[l], batch, seq)
        yc = _mla_sample(yc, qabs, kc, cache_mla_ckv, cache_mla_kr, wuv[l], l, tp, dbatch, dseq)

        ml_w = (gate_bias[l], mlstm_norm_g[l][None])
        yd, ct_p, m_p = _mlstm_prompt(yd, mqkv, mkt, mvt, mo, gates, gt, gate_bias[l], gate_bias_t[l],
                                      mlstm_norm_g[l][None], batch, seq, lp)
        yd, c_s, m_s = _mlstm(yd, mqkv, mo, gates, caug0_s[l], m0_s[l], *ml_w, tp, dbatch, dseq, dseq)

        xp, xs = _outmlp(xp, xs, ya, yb, yc, yd, w_out_b, norm2_g[l][None], w1_b, w2_b, fg, l, l == depth - 1, l > 0)

        keep = min(A_SPAN, seq)
        tail = lambda a, n: jnp.stack([a[(b + 1) * seq - n:(b + 1) * seq] for b in range(batch)])
        akv_p = tail(akv, keep).reshape(batch, keep, 2, N_HEADS, HD)
        akv_s = akv[tp:].reshape(dbatch, dseq, 2, N_HEADS, HD)
        new = (akv_p[:, :, 0], akv_p[:, :, 1], tail(bxbc, B_CONV - 1), h_p,
               ckv[:tp].reshape(batch, seq, C_KV_LORA), kr[:tp].reshape(batch, seq, C_ROPE),
               jnp.swapaxes(ct_p[:, :, :HD, :], 2, 3), ct_p[:, :, HD, :], m_p[:, 0, :N_HEADS],
               jnp.concatenate([cache_attn_k[l][:, dseq:], akv_s[:, :, 0]], axis=1),
               jnp.concatenate([cache_attn_v[l][:, dseq:], akv_s[:, :, 1]], axis=1),
               bxbc[tp:].reshape(dbatch, dseq, B_CONV_DIM)[:, dseq - (B_CONV - 1):], h_s,
               ckv[tp:].reshape(dbatch, dseq, C_KV_LORA), kr[tp:].reshape(dbatch, dseq, C_ROPE),
               c_s[..., :HD], c_s[..., HD], m_s[:, 0, :N_HEADS])
        for dst, v in zip(outs, new):
            dst.append(v)

    y_prompt = xp.reshape(batch, seq, D_MODEL)
    y_sample = xs.reshape(dbatch, dseq, D_MODEL)
    return (y_prompt, y_sample) + tuple(jnp.stack(v) for v in outs)
```

```python
import functools
import math

import jax
import jax.numpy as jnp
import numpy as np
from jax import lax
from jax.experimental import pallas as pl
from jax.experimental.pallas import tpu as pltpu

F32 = jnp.float32
BF16 = jnp.bfloat16
EPS = 1e-6
NEG = -1e30
ROPE_BASE = 10000.0

CHUNK = 64
D_MODEL = 1024
GROUP_W = 256
D_FF = 4096
N_HEADS = 4
HD = 64
A_BAND_CHUNKS = 8
A_SPAN = A_BAND_CHUNKS * CHUNK
A_REL_CLIP = 128
B_GROUPS = 2
B_STATE = 64
B_CONV = 4
B_CONV_DIM = 512
C_NOPE = 64
C_ROPE = 32
C_KV_LORA = 128
C_SCALE = (C_NOPE + C_ROPE) ** -0.5
C_QSCALE = C_SCALE * math.log2(math.e)
V_ROWS = 80
LANE = 128
VMEM_LIMIT = 56 * 1024 * 1024

OFF_A, OFF_BZ, OFF_XBC, OFF_CQ, OFF_CKV, OFF_KR, OFF_KRR, OFF_G, OFF_MQKV, OFF_MO, N_PROJ = (
    0, 768, 1024, 1536, 1792, 1920, 2048, 2176, 2304, 3072, 3328)
G_DT, G_I, G_F = 0, 4, 8
GT_ROWS = 16

ROW_TILE = 512
A_QBLOCK = 256
C_TILE = 256
C_UNROLL_PAIRS = 4
REC_CHUNK_PROMPT = 256


def _cparams(sem):
    return pltpu.CompilerParams(dimension_semantics=sem, vmem_limit_bytes=VMEM_LIMIT)


def _rms(x, g):
    return x * lax.rsqrt(jnp.mean(x * x, axis=-1, keepdims=True) + EPS) * g


def _dot(a, b):
    return jnp.dot(a, b, preferred_element_type=F32)


def _dot_nt(a, b):
    return lax.dot_general(a, b, (((1,), (1,)), ((), ())), preferred_element_type=F32)


def _dot_tn(a, b):
    return lax.dot_general(a, b, (((0,), (0,)), ((), ())), preferred_element_type=F32)


def _cumsum_rows(x, tri_bf16):
    hi = x.astype(BF16)
    r1 = x - hi.astype(F32)
    mid = r1.astype(BF16)
    lo = (r1 - mid.astype(F32)).astype(BF16)
    return _dot(tri_bf16, hi) + _dot(tri_bf16, mid) + _dot(tri_bf16, lo)


def _log1p_exp_neg_abs(x):
    return jnp.log1p(jnp.exp(-jnp.abs(x)))


def _full_spec(shape):
    n = len(shape)
    return pl.BlockSpec(shape, lambda *_: (0,) * n)


_DST_SPEC = pl.BlockSpec(memory_space=pl.ANY)


def _const_spec(shape):
    n = len(shape)
    return pl.BlockSpec(shape, lambda *_: (0,) * n, pipeline_mode=pl.Buffered(1))


def _layer_spec(shape, layer):
    n = len(shape)
    return pl.BlockSpec((None,) + tuple(shape), lambda *_: (layer,) + (0,) * n, pipeline_mode=pl.Buffered(1))


def _split_rows(n_prompt_tiles, width):
    npt = n_prompt_tiles
    return (pl.BlockSpec((ROW_TILE, width), lambda i: (jnp.minimum(i, npt - 1), 0)),
            pl.BlockSpec((ROW_TILE, width), lambda i: (jnp.maximum(i - npt, 0), 0)))


def _inproj_kernel(xp_ref, xs_ref, g1_ref, w_ref, qg_ref, kvg_ref, wuq_ref, wabs_ref, wuv_ref, cos_ref, sin_ref,
                   aq_ref, akv_ref, bz_ref, bxbc_ref, qabs_ref, qt_ref, ckv_ref, kr_ref, kc_ref, vt_ref,
                   mqkv_ref, mo_ref, gates_ref, mkt_ref, mvt_ref, gt_ref, *, n_prompt_tiles):
    tm = xp_ref.shape[0]
    is_prompt = pl.program_id(0) < n_prompt_tiles
    x = jnp.where(is_prompt, xp_ref[...], xs_ref[...])
    hn = _rms(x, g1_ref[...]).astype(BF16)

    def seg(a, b):
        return _dot(hn, w_ref[:, a:b])

    a = seg(OFF_A, OFF_BZ)
    aq_ref[...] = (a[:, :GROUP_W] * HD ** -0.5).astype(BF16)
    akv_ref[...] = a[:, GROUP_W:]
    bz_ref[...] = seg(OFF_BZ, OFF_XBC)
    bxbc_ref[...] = seg(OFF_XBC, OFF_CQ)

    cos = cos_ref[...]
    sin = sin_ref[...]
    cqn = _rms(seg(OFF_CQ, OFF_CKV), qg_ref[...]).astype(BF16)
    qf = _dot(cqn, wuq_ref[...])
    qrope = (qf[:, 256:384] * cos + qf[:, 384:512] * sin) * C_QSCALE
    qlat = _dot(qf[:, :256].astype(BF16), wabs_ref[...]) * C_QSCALE
    lane_head = lax.broadcasted_iota(jnp.int32, (tm, LANE), 1) // C_ROPE
    qhs = [jnp.concatenate([qlat[:, h * LANE:(h + 1) * LANE], jnp.where(lane_head == h, qrope, 0.0)], axis=-1)
           for h in range(N_HEADS)]

    ckv_kr = seg(OFF_CKV, OFF_KRR)
    krr_g = seg(OFF_KRR, OFF_MQKV)
    g = krr_g[:, LANE:]
    ckv = _rms(ckv_kr[:, :C_KV_LORA], kvg_ref[...])
    kr4 = ckv_kr[:, C_KV_LORA:] * cos + krr_g[:, :LANE] * sin
    ckv_ref[...] = ckv
    kr_ref[...] = kr4[:, :C_ROPE]
    kc_ref[...] = jnp.concatenate([ckv, kr4], axis=-1).astype(BF16)

    m = seg(OFF_MQKV, OFF_MO)
    mk = m[:, 256:512] * HD ** -0.5
    mqkv_ref[:, 0:256] = m[:, 0:256].astype(BF16)
    mqkv_ref[:, 256:512] = mk.astype(BF16)
    mqkv_ref[:, 512:768] = m[:, 512:768].astype(BF16)
    mo_ref[...] = seg(OFF_MO, N_PROJ)
    gates_ref[...] = g

    for h in range(N_HEADS):
        qt_ref[h] = qhs[h].T.astype(BF16)
    vall = _dot(ckv.astype(BF16), wuv_ref[...])
    ones_c = (lax.broadcasted_iota(jnp.int32, (V_ROWS - HD, C_TILE), 0) == 0).astype(BF16)
    for s in range(tm // C_TILE):
        vt = vall[s * C_TILE:(s + 1) * C_TILE, :].T
        for h in range(N_HEADS):
            vt_ref[s, h, 0:HD, :] = vt[h * HD:(h + 1) * HD].astype(BF16)
            vt_ref[s, h, HD:V_ROWS, :] = ones_c
    mkt = mk.T
    mvt = m[:, 512:768].T
    ones_v = (lax.broadcasted_iota(jnp.int32, (LANE - HD, tm), 0) == 0).astype(BF16)
    for h in range(N_HEADS):
        mkt_ref[h] = mkt[h * HD:(h + 1) * HD].astype(BF16)
        mvt_ref[h, 0:HD, :] = mvt[h * HD:(h + 1) * HD].astype(BF16)
        mvt_ref[h, HD:LANE, :] = ones_v
    gt_ref[...] = g.T[0:GT_ROWS]

    @pl.when(jnp.logical_not(is_prompt))
    def _():
        for h in range(N_HEADS):
            qabs_ref[h] = qhs[h].astype(BF16)


def _inproj(xp, xs, g1, w, qg, kvg, wuq, wabs, wuv_all, cos, sin, layer):
    tp, ts = xp.shape[0], xs.shape[0]
    t = tp + ts
    tm = ROW_TILE
    npt = tp // tm
    col = lambda *lead: pl.BlockSpec((*lead, tm), lambda i: (0,) * len(lead) + (i,))
    row = lambda n: pl.BlockSpec((tm, n), lambda i: (i, 0))
    out_shape = (
        jax.ShapeDtypeStruct((t, 256), BF16),
        jax.ShapeDtypeStruct((t, 512), F32),
        jax.ShapeDtypeStruct((t, 256), F32),
        jax.ShapeDtypeStruct((t, 512), F32),
        jax.ShapeDtypeStruct((N_HEADS, ts, 256), BF16),
        jax.ShapeDtypeStruct((N_HEADS, 256, t), BF16),
        jax.ShapeDtypeStruct((t, 128), F32),
        jax.ShapeDtypeStruct((t, C_ROPE), F32),
        jax.ShapeDtypeStruct((t, 256), BF16),
        jax.ShapeDtypeStruct((t // C_TILE, N_HEADS, V_ROWS, C_TILE), BF16),
        jax.ShapeDtypeStruct((t, 768), BF16),
        jax.ShapeDtypeStruct((t, 256), F32),
        jax.ShapeDtypeStruct((t, LANE), F32),
        jax.ShapeDtypeStruct((N_HEADS, HD, t), BF16),
        jax.ShapeDtypeStruct((N_HEADS, LANE, t), BF16),
        jax.ShapeDtypeStruct((GT_ROWS, t), F32),
    )
    out_specs = (
        row(256), row(512), row(256), row(512),
        pl.BlockSpec((N_HEADS, tm, 256), lambda i: (0, jnp.maximum(i - npt, 0), 0)),
        col(N_HEADS, 256),
        row(128), row(C_ROPE), row(256),
        pl.BlockSpec((tm // C_TILE, N_HEADS, V_ROWS, C_TILE), lambda i: (i, 0, 0, 0)),
        row(768), row(256), row(LANE),
        col(N_HEADS, HD), col(N_HEADS, LANE), col(GT_ROWS),
    )
    in_specs = [*_split_rows(npt, D_MODEL), _full_spec((1, D_MODEL)), _layer_spec((D_MODEL, N_PROJ), layer),
                _full_spec((1, 256)), _full_spec((1, 128)), _layer_spec((256, 512), layer),
                _layer_spec((256, 512), layer), _layer_spec((C_KV_LORA, 256), layer), row(LANE), row(LANE)]
    return pl.pallas_call(
        functools.partial(_inproj_kernel, n_prompt_tiles=npt), grid=(t // tm,), in_specs=in_specs,
        out_specs=out_specs, out_shape=out_shape,
        compiler_params=_cparams(("arbitrary",)), name="inproj",
    )(xp, xs, g1, w, qg, kvg, wuq, wabs, wuv_all, cos, sin)


def _band_heads(q, kb, vb, bias_ref, valid, o_ref):
    for h in range(N_HEADS):
        sl = slice(h * HD, (h + 1) * HD)
        s = _dot_nt(q[:, sl], kb[:, sl]) + bias_ref[h]
        if valid is not None:
            s = jnp.where(valid, s, NEG)
        m = jnp.max(s, axis=-1, keepdims=True)
        p = jnp.exp(s - m)
        l = jnp.sum(p, axis=-1, keepdims=True)
        o = _dot(p.astype(BF16), vb[:, sl]) / l
        o_ref[:, sl] = o.astype(o_ref.dtype)


def _attn_a_prompt_kernel(dst_ref, q_ref, k0_ref, k1_ref, k2_ref, bias_ref, o_ref):
    i = pl.program_id(1)
    kv = jnp.concatenate([k0_ref[...], k1_ref[...], k2_ref[...]], axis=0)
    kb = kv[:, :GROUP_W].astype(BF16)
    vb = kv[:, GROUP_W:].astype(BF16)
    col = lax.broadcasted_iota(jnp.int32, (A_QBLOCK, 3 * A_QBLOCK), 1)
    valid = col >= (2 - i) * A_QBLOCK
    _band_heads(q_ref[...], kb, vb, bias_ref, valid, o_ref)


def _attn_a_prompt(dst, aq, akv, biasm, batch, seq):
    nb = seq // A_QBLOCK
    qb = A_QBLOCK

    def kspec(back):
        return pl.BlockSpec((qb, 512), lambda b, i: (b * nb + jnp.maximum(i - back, 0), 0))

    return pl.pallas_call(
        _attn_a_prompt_kernel, grid=(batch, nb),
        in_specs=[_DST_SPEC, pl.BlockSpec((qb, 256), lambda b, i: (b * nb + i, 0)), kspec(2), kspec(1), kspec(0),
                  _full_spec(biasm.shape)],
        out_specs=pl.BlockSpec((qb, 256), lambda b, i: (b * nb + i, 0)),
        out_shape=jax.ShapeDtypeStruct(dst.shape, dst.dtype), input_output_aliases={0: 0},
        compiler_params=_cparams(("parallel", "parallel")), name="attn_a_prompt",
    )(dst, aq, akv, akv, akv, biasm)


def _attn_a_sample_kernel(dst_ref, q_ref, ck_ref, cv_ref, kvn_ref, bias_ref, o_ref):
    kvn = kvn_ref[...]
    kb = jnp.concatenate([ck_ref[...], kvn[:, :GROUP_W]], axis=0).astype(BF16)
    vb = jnp.concatenate([cv_ref[...], kvn[:, GROUP_W:]], axis=0).astype(BF16)
    _band_heads(q_ref[...], kb, vb, bias_ref, None, o_ref)


def _attn_a_sample(dst, aq, akv, cache_k, cache_v, bias, layer, row0, n_seq, seq):
    blk0 = row0 // seq
    cspec = pl.BlockSpec((None, None, A_SPAN, 256), lambda b: (layer, b, 0, 0))
    return pl.pallas_call(
        _attn_a_sample_kernel, grid=(n_seq,),
        in_specs=[_DST_SPEC, pl.BlockSpec((seq, 256), lambda b: (blk0 + b, 0)), cspec, cspec,
                  pl.BlockSpec((seq, 512), lambda b: (blk0 + b, 0)), _full_spec(bias.shape)],
        out_specs=pl.BlockSpec((seq, 256), lambda b: (blk0 + b, 0)),
        out_shape=jax.ShapeDtypeStruct(dst.shape, dst.dtype), input_output_aliases={0: 0},
        compiler_params=_cparams(("parallel",)), name="attn_a_sample",
    )(dst, aq, cache_k, cache_v, akv, bias)


def _ssd_kernel(dst_ref, z_ref, xbc_ref, g_ref, conv0_ref, h0_ref, cw_ref, cb_ref, dtb_ref, alog_ref, dsk_ref, ng_ref,
                y_ref, hout_ref, xpad_ref, h_ref):
    L = z_ref.shape[0]
    c = pl.program_id(1)

    @pl.when(c == 0)
    def _():
        xpad_ref[0:8, :] = conv0_ref[...]
        h_ref[...] = h0_ref[...]

    xpad_ref[8:8 + L, :] = xbc_ref[...]
    conv = cb_ref[...]
    for j in range(B_CONV):
        conv = conv + xpad_ref[5 + j:5 + j + L, :] * cw_ref[j:j + 1, :]
    xpad_ref[0:8, :] = xpad_ref[L:L + 8, :]
    u = conv * jax.nn.sigmoid(conv)

    g = g_ref[...] + dtb_ref[...]
    dt = jnp.maximum(g, 0.0) + _log1p_exp_neg_abs(g)
    da = dt * (-jnp.exp(alog_ref[...]))
    row = lax.broadcasted_iota(jnp.int32, (L, L), 0)
    colm = lax.broadcasted_iota(jnp.int32, (L, L), 1)
    tri = row >= colm
    cs = _cumsum_rows(da, tri.astype(BF16))
    cs_t = cs.T

    ys = []
    gmat = [None] * B_GROUPS
    for h in range(N_HEADS):
        grp = h // (N_HEADS // B_GROUPS)
        xs = u[:, h * HD:(h + 1) * HD]
        bm = u[:, 256 + grp * B_STATE:256 + (grp + 1) * B_STATE].astype(BF16)
        cm = u[:, 384 + grp * B_STATE:384 + (grp + 1) * B_STATE].astype(BF16)
        if gmat[grp] is None:
            gmat[grp] = _dot_nt(cm, bm)
        csc = cs[:, G_DT + h:G_DT + h + 1]
        csr = cs_t[G_DT + h:G_DT + h + 1, :]
        tot = cs[L - 1:L, G_DT + h:G_DT + h + 1]
        dec = jnp.exp(jnp.where(tri, csc - csr, NEG))
        xdt = xs * dt[:, G_DT + h:G_DT + h + 1]
        hprev = h_ref[h]
        y = _dot((gmat[grp] * dec).astype(BF16), xdt.astype(BF16))
        y = y + _dot_nt(cm, hprev.astype(BF16)) * jnp.exp(csc)
        y = y + dsk_ref[0:1, h:h + 1] * xs
        ys.append(y)
        wend = (xdt * jnp.exp(tot - csc)).astype(BF16)
        h_ref[h] = jnp.exp(tot) * hprev + _dot_tn(wend, bm)
    y = jnp.concatenate(ys, axis=-1)
    z = z_ref[...]
    y = y * (z * jax.nn.sigmoid(z))
    y_ref[...] = _rms(y, ng_ref[...]).astype(y_ref.dtype)

    @pl.when(c == pl.num_programs(1) - 1)
    def _():
        hout_ref[...] = h_ref[...]


def _ssd(dst, bz, bxbc, gates, conv0, h0, cw, cb, dtb, alog, dsk, ng, row0, n_seq, seq, L):
    nc = seq // L
    blk0 = row0 // L
    rspec = lambda n: pl.BlockSpec((L, n), lambda b, c: (blk0 + b * nc + c, 0))
    return pl.pallas_call(
        _ssd_kernel, grid=(n_seq, nc),
        in_specs=[_DST_SPEC, rspec(256), rspec(512), rspec(LANE),
                  pl.BlockSpec((None, 8, 512), lambda b, c: (b, 0, 0)),
                  pl.BlockSpec((None, N_HEADS, HD, B_STATE), lambda b, c: (b, 0, 0, 0)),
                  _full_spec((B_CONV, 512)), _full_spec((1, 512)), _full_spec((1, LANE)), _full_spec((1, LANE)),
                  _full_spec((1, LANE)), _full_spec((1, 256))],
        out_specs=(rspec(256), pl.BlockSpec((None, N_HEADS, HD, B_STATE), lambda b, c: (b, 0, 0, 0))),
        input_output_aliases={0: 0},
        out_shape=(jax.ShapeDtypeStruct(dst.shape, dst.dtype),
                   jax.ShapeDtypeStruct((n_seq, N_HEADS, HD, B_STATE), F32)),
        scratch_shapes=[pltpu.VMEM((L + 8, 512), F32), pltpu.VMEM((N_HEADS, HD, B_STATE), F32)],
        compiler_params=_cparams(("parallel", "arbitrary")), name="ssd",
    )(dst, bz, bxbc, gates, conv0, h0, cw, cb, dtb, alog, dsk, ng)


def _mlstm_kernel(dst_ref, qkv_ref, o_ref, g_ref, c0_ref, m0_ref, bias_ref, ng_ref,
                  y_ref, cout_ref, mout_ref, c_ref, m_ref):
    L = qkv_ref.shape[0]
    c = pl.program_id(1)

    @pl.when(c == 0)
    def _():
        c_ref[...] = c0_ref[...]
        m_ref[...] = m0_ref[...]

    g = g_ref[...] + bias_ref[...]
    lf = jnp.minimum(g, 0.0) - _log1p_exp_neg_abs(g)
    row = lax.broadcasted_iota(jnp.int32, (L, L), 0)
    colm = lax.broadcasted_iota(jnp.int32, (L, L), 1)
    tri = row >= colm
    bcum = _cumsum_rows(lf, tri.astype(BF16))
    bcum_t = bcum.T
    g_t = g.T
    unit = (lax.broadcasted_iota(jnp.int32, (L, HD), 1) == 0).astype(BF16)
    ng = ng_ref[...]
    og = o_ref[...]

    for h in range(N_HEADS):
        sl = slice(h * HD, (h + 1) * HD)
        q = qkv_ref[:, h * HD:(h + 1) * HD]
        k = qkv_ref[:, 256 + h * HD:256 + (h + 1) * HD]
        v = qkv_ref[:, 512 + h * HD:512 + (h + 1) * HD]
        vaug = jnp.concatenate([v, unit], axis=-1)
        bc = bcum[:, G_F + h:G_F + h + 1]
        br = bcum_t[G_F + h:G_F + h + 1, :]
        ic = g[:, G_I + h:G_I + h + 1]
        ir = g_t[G_I + h:G_I + h + 1, :]
        tot = bcum[L - 1:L, G_F + h:G_F + h + 1]
        mprev = m_ref[0:1, h:h + 1]

        dm = jnp.where(tri, bc - br + ir, NEG)
        inter = bc + mprev
        mt = jnp.maximum(inter, jnp.max(dm, axis=-1, keepdims=True))
        w_intra = jnp.exp(dm - mt)
        w_inter = jnp.exp(inter - mt)
        caug = c_ref[h]
        s = _dot_nt(q, k) * w_intra
        r = w_inter * _dot(q, caug.astype(BF16)) + _dot(s.astype(BF16), vaug)
        num = r[:, :HD]
        den = r[:, HD:HD + 1]
        hout = num / jnp.maximum(jnp.abs(den), jnp.exp(-mt))

        gend = tot - bc + ic
        mnew = jnp.maximum(tot + mprev, jnp.max(gend, axis=0, keepdims=True))
        kw = (k.astype(F32) * jnp.exp(gend - mnew)).astype(BF16)
        c_ref[h] = jnp.exp(tot + mprev - mnew) * caug + _dot_tn(kw, vaug)
        m_ref[0:1, h:h + 1] = mnew

        yh = _rms(hout, ng[:, sl]) * jax.nn.sigmoid(og[:, sl])
        y_ref[:, sl] = yh.astype(y_ref.dtype)

    @pl.when(c == pl.num_programs(1) - 1)
    def _():
        cout_ref[...] = c_ref[...]
        mout_ref[...] = m_ref[...]


def _cumsum_lanes(x, tri_bf16):
    hi = x.astype(BF16)
    r1 = x - hi.astype(F32)
    mid = r1.astype(BF16)
    lo = (r1 - mid.astype(F32)).astype(BF16)
    return _dot(hi, tri_bf16) + _dot(mid, tri_bf16) + _dot(lo, tri_bf16)


def _mlstm_prompt_kernel(dst_ref, qkv_ref, kt_ref, vt_ref, o_ref, g_ref, gt_ref, bias_ref, biast_ref, ng_ref,
                         y_ref, cout_ref, mout_ref, c_ref, m_ref):
    L = qkv_ref.shape[0]
    c = pl.program_id(1)

    @pl.when(c == 0)
    def _():
        c_ref[...] = jnp.zeros_like(c_ref)
        m_ref[...] = jnp.zeros_like(m_ref)

    g = g_ref[...] + bias_ref[...]
    gt = gt_ref[...] + biast_ref[...]
    lf = jnp.minimum(g, 0.0) - _log1p_exp_neg_abs(g)
    lft = jnp.minimum(gt, 0.0) - _log1p_exp_neg_abs(gt)
    row = lax.broadcasted_iota(jnp.int32, (L, L), 0)
    colm = lax.broadcasted_iota(jnp.int32, (L, L), 1)
    s_le_l = row <= colm
    bcum = _cumsum_rows(lf, (row >= colm).astype(BF16))
    bcum_t = _cumsum_lanes(lft, s_le_l.astype(BF16))

    yts = []
    for h in range(N_HEADS):
        q = qkv_ref[:, h * HD:(h + 1) * HD]
        k = qkv_ref[:, 256 + h * HD:256 + (h + 1) * HD]
        src = g[:, G_I + h:G_I + h + 1] - bcum[:, G_F + h:G_F + h + 1]
        bc = bcum_t[G_F + h:G_F + h + 1, :]
        ir = gt[G_I + h:G_I + h + 1, :]
        tot = bcum_t[G_F + h:G_F + h + 1, L - 1:L]
        mprev = m_ref[0:1, h:h + 1]

        dmt = jnp.where(s_le_l, bc + src, NEG)
        inter = bc + mprev
        mt = jnp.maximum(inter, jnp.max(dmt, axis=0, keepdims=True))
        w_intra = jnp.exp(dmt - mt)
        w_inter = jnp.exp(inter - mt)
        ct = c_ref[h]
        st = _dot_nt(k, q) * w_intra
        rt = w_inter * _dot_nt(ct.astype(BF16), q) + _dot(vt_ref[h], st.astype(BF16))
        hout = rt[:HD] / jnp.maximum(jnp.abs(rt[HD:HD + 1]), jnp.exp(-mt))
        yts.append(hout * lax.rsqrt(jnp.mean(hout * hout, axis=0, keepdims=True) + EPS))

        gend = tot - bc + ir
        mnew = jnp.maximum(tot + mprev, jnp.max(gend, axis=-1, keepdims=True))
        kw = (kt_ref[h].astype(F32) * jnp.exp(gend - mnew)).astype(BF16)
        c_ref[h] = jnp.exp(tot + mprev - mnew) * ct + _dot_nt(vt_ref[h], kw)
        m_ref[0:1, h:h + 1] = mnew

    y = jnp.concatenate(yts, axis=0).T
    y_ref[...] = (y * ng_ref[...] * jax.nn.sigmoid(o_ref[...])).astype(y_ref.dtype)

    @pl.when(c == pl.num_programs(1) - 1)
    def _():
        cout_ref[...] = c_ref[...]
        mout_ref[...] = m_ref[...]


def _mlstm_prompt(dst, mqkv, mkt, mvt, mo, gates, gt, bias, biast, ng, n_seq, seq, L):
    nc = seq // L
    rspec = lambda n: pl.BlockSpec((L, n), lambda b, c: (b * nc + c, 0))
    cspec = lambda *lead: pl.BlockSpec((*lead, L), lambda b, c: (0,) * len(lead) + (b * nc + c,))
    stspec = pl.BlockSpec((None, N_HEADS, LANE, HD), lambda b, c: (b, 0, 0, 0))
    mspec = pl.BlockSpec((None, 1, LANE), lambda b, c: (b, 0, 0))
    return pl.pallas_call(
        _mlstm_prompt_kernel, grid=(n_seq, nc),
        in_specs=[_DST_SPEC, rspec(768), cspec(N_HEADS, HD), cspec(N_HEADS, LANE), rspec(256), rspec(LANE),
                  cspec(GT_ROWS), _full_spec((1, LANE)), _full_spec((GT_ROWS, 1)), _full_spec((1, 256))],
        out_specs=(rspec(256), stspec, mspec), input_output_aliases={0: 0},
        out_shape=(jax.ShapeDtypeStruct(dst.shape, dst.dtype),
                   jax.ShapeDtypeStruct((n_seq, N_HEADS, LANE, HD), F32),
                   jax.ShapeDtypeStruct((n_seq, 1, LANE), F32)),
        scratch_shapes=[pltpu.VMEM((N_HEADS, LANE, HD), F32), pltpu.VMEM((1, LANE), F32)],
        compiler_params=_cparams(("parallel", "arbitrary")), name="mlstm_prompt",
    )(dst, mqkv, mkt, mvt, mo, gates, gt, bias, biast, ng)


def _mlstm(dst, mqkv, mo, gates, c0, m0, bias, ng, row0, n_seq, seq, L):
    nc = seq // L
    blk0 = row0 // L
    rspec = lambda n: pl.BlockSpec((L, n), lambda b, c: (blk0 + b * nc + c, 0))
    cspec = pl.BlockSpec((None, N_HEADS, HD, LANE), lambda b, c: (b, 0, 0, 0))
    mspec = pl.BlockSpec((None, 1, LANE), lambda b, c: (b, 0, 0))
    return pl.pallas_call(
        _mlstm_kernel, grid=(n_seq, nc),
        in_specs=[_DST_SPEC, rspec(768), rspec(256), rspec(LANE), cspec, mspec, _full_spec((1, LANE)),
                  _full_spec((1, 256))],
        out_specs=(rspec(256), cspec, mspec), input_output_aliases={0: 0},
        out_shape=(jax.ShapeDtypeStruct(dst.shape, dst.dtype),
                   jax.ShapeDtypeStruct((n_seq, N_HEADS, HD, LANE), F32),
                   jax.ShapeDtypeStruct((n_seq, 1, LANE), F32)),
        scratch_shapes=[pltpu.VMEM((N_HEADS, HD, LANE), F32), pltpu.VMEM((1, LANE), F32)],
        compiler_params=_cparams(("parallel", "arbitrary")), name="mlstm",
    )(dst, mqkv, mo, gates, c0, m0, bias, ng)


def _mla_prompt_kernel(dst_ref, q_ref, kc_ref, vt_ref, o_ref, acc_ref, m_ref, s_ref):
    tq = q_ref.shape[2]
    tk = C_TILE
    i = pl.program_id(1)
    m_ref[...] = jnp.full_like(m_ref, NEG)
    acc_ref[...] = jnp.zeros_like(acc_ref)

    def scores(j, slot):
        start = pl.multiple_of(j * tk, tk)
        kt = kc_ref[pl.ds(start, tk), :]
        for h in range(N_HEADS):
            s_ref[slot, h] = _dot(kt, q_ref[h])

    def consume(j, slot, masked):
        for h in range(N_HEADS):
            s = s_ref[slot, h]
            if masked:
                krow = lax.broadcasted_iota(jnp.int32, (tk, tq), 0) // CHUNK
                qcol = lax.broadcasted_iota(jnp.int32, (tk, tq), 1) // CHUNK
                s = jnp.where(krow <= qcol, s, NEG)
            m_old = m_ref[h]
            m_new = jnp.maximum(m_old, jnp.max(s, axis=0, keepdims=True))
            p = jnp.exp2(s - m_new).astype(BF16)
            acc_ref[h] = jnp.exp2(m_old - m_new) * acc_ref[h] + _dot(vt_ref[j, h], p)
            m_ref[h] = m_new

    scores(0, 0)

    def pair(j):
        scores(j + 1, 1)
        consume(j, 0, False)
        scores(j + 2, 0)
        consume(j + 1, 1, False)

    def body_main(jj, carry):
        for u in range(C_UNROLL_PAIRS):
            pair(2 * (C_UNROLL_PAIRS * jj + u))
        return carry

    def body_rest(jj, carry):
        pair(2 * (C_UNROLL_PAIRS * n_main + jj))
        return carry

    n_main = i // (2 * C_UNROLL_PAIRS)
    lax.fori_loop(0, n_main, body_main, 0)
    lax.fori_loop(0, i // 2 - C_UNROLL_PAIRS * n_main, body_rest, 0)

    @pl.when(i % 2 == 1)
    def _():
        scores(i, 1)
        consume(i - 1, 0, False)
        consume(i, 1, True)

    @pl.when(i % 2 == 0)
    def _():
        consume(i, 0, True)

    ys = []
    for h in range(N_HEADS):
        acc = acc_ref[h]
        ys.append(acc[:HD] / acc[HD:HD + 1])
    o_ref[...] = jnp.concatenate(ys, axis=0).T.astype(o_ref.dtype)


def _mla_prompt(dst, qt, kc, vt, batch, seq):
    tq = C_TILE
    nq = seq // tq
    return pl.pallas_call(
        _mla_prompt_kernel, grid=(batch, nq),
        in_specs=[_DST_SPEC, pl.BlockSpec((N_HEADS, 256, tq), lambda b, i: (0, 0, b * nq + i)),
                  pl.BlockSpec((seq, 256), lambda b, i: (b, 0)),
                  pl.BlockSpec((seq // C_TILE, N_HEADS, V_ROWS, C_TILE), lambda b, i: (b, 0, 0, 0))],
        out_specs=pl.BlockSpec((tq, 256), lambda b, i: (b * nq + i, 0)),
        out_shape=jax.ShapeDtypeStruct(dst.shape, dst.dtype), input_output_aliases={0: 0},
        scratch_shapes=[pltpu.VMEM((N_HEADS, V_ROWS, tq), F32), pltpu.VMEM((N_HEADS, 1, tq), F32),
                        pltpu.VMEM((2, N_HEADS, C_TILE, tq), F32)],
        compiler_params=_cparams(("parallel", "arbitrary")), name="mla_prompt",
    )(dst, qt, kc, vt)


def _mla_sample_kernel(dst_ref, q_ref, ckv_ref, kr_ref, knew_ref, wuv_ref, o_ref):
    seq = q_ref.shape[1]
    qs = q_ref[...].reshape(N_HEADS * seq, 2 * LANE)
    kcache = jnp.concatenate([ckv_ref[...]] + [kr_ref[...]] * N_HEADS, axis=-1).astype(BF16)
    kall = jnp.concatenate([kcache, knew_ref[...]], axis=0)
    s = _dot_nt(qs, kall)
    m = jnp.max(s, axis=-1, keepdims=True)
    p = jnp.exp2(s - m)
    l = jnp.sum(p, axis=-1, keepdims=True)
    o = (_dot(p.astype(BF16), kall[:, :C_KV_LORA]) / l).astype(BF16)
    for h in range(N_HEADS):
        o_ref[:, h * HD:(h + 1) * HD] = _dot(o[h * seq:(h + 1) * seq], wuv_ref[h]).astype(o_ref.dtype)


def _mla_sample(dst, qabs, kc, cache_ckv, cache_kr, wuv, layer, row0, n_seq, seq):
    blk0 = row0 // seq
    past = cache_ckv.shape[2]
    return pl.pallas_call(
        _mla_sample_kernel, grid=(n_seq,),
        in_specs=[_DST_SPEC, pl.BlockSpec((N_HEADS, seq, 256), lambda b: (0, b, 0)),
                  pl.BlockSpec((None, None, past, C_KV_LORA), lambda b: (layer, b, 0, 0)),
                  pl.BlockSpec((None, None, past, C_ROPE), lambda b: (layer, b, 0, 0)),
                  pl.BlockSpec((seq, 256), lambda b: (blk0 + b, 0)),
                  _full_spec((N_HEADS, 128, HD))],
        out_specs=pl.BlockSpec((seq, 256), lambda b: (blk0 + b, 0)),
        out_shape=jax.ShapeDtypeStruct(dst.shape, dst.dtype), input_output_aliases={0: 0},
        compiler_params=_cparams(("parallel",)), name="mla_sample",
    )(dst, qabs, cache_ckv, cache_kr, kc, wuv)


def _outmlp_kernel(xp_ref, xs_ref, ya_ref, yb_ref, yc_ref, yd_ref, wout_ref, g2_ref, w1_ref, w2_ref, fg_ref,
                   op_ref, os_ref, *, final, n_prompt_tiles):
    i = pl.program_id(0)
    y = jnp.concatenate([ya_ref[...], yb_ref[...], yc_ref[...], yd_ref[...]], axis=-1)
    x1 = jnp.where(i < n_prompt_tiles, xp_ref[...], xs_ref[...]) + _dot(y, wout_ref[...])
    hm = _rms(x1, g2_ref[...]).astype(BF16)
    acc = x1
    for c in range(D_FF // D_MODEL):
        sl = slice(c * D_MODEL, (c + 1) * D_MODEL)
        hc = jnp.square(jnp.maximum(_dot(hm, w1_ref[:, sl]), 0.0)).astype(BF16)
        acc = acc + _dot(hc, w2_ref[sl, :])
    if final:
        acc = _rms(acc, fg_ref[...])

    @pl.when(i < n_prompt_tiles)
    def _():
        op_ref[...] = acc

    @pl.when(i >= n_prompt_tiles)
    def _():
        os_ref[...] = acc


def _outmlp(xp, xs, ya, yb, yc, yd, wout, g2, w1, w2, fg, layer, final, in_place):
    t = xp.shape[0] + xs.shape[0]
    tm = ROW_TILE
    npt = xp.shape[0] // tm
    row = lambda n: pl.BlockSpec((tm, n), lambda i: (i, 0))
    xspecs = _split_rows(npt, D_MODEL)
    return pl.pallas_call(
        functools.partial(_outmlp_kernel, final=final, n_prompt_tiles=npt), grid=(t // tm,),
        in_specs=[*xspecs, row(256), row(256), row(256), row(256), _layer_spec((D_MODEL, D_MODEL), layer),
                  _full_spec((1, D_MODEL)), _layer_spec((D_MODEL, D_FF), layer), _layer_spec((D_FF, D_MODEL), layer),
                  _full_spec((1, D_MODEL))],
        out_specs=xspecs,
        out_shape=(jax.ShapeDtypeStruct(xp.shape, F32), jax.ShapeDtypeStruct(xs.shape, F32)),
        input_output_aliases={0: 0, 1: 1} if in_place else {},
        compiler_params=_cparams(("arbitrary",)), name="outmlp",
    )(xp, xs, ya, yb, yc, yd, wout, g2, w1, w2, fg)


def _rot_cols(w):
    half = C_ROPE // 2
    return jnp.concatenate([-w[..., half:], w[..., :half]], axis=-1)


def _pad_lanes(v, n=LANE, at=0):
    out = jnp.zeros(v.shape[:-1] + (n,), v.dtype)
    return out.at[..., at:at + v.shape[-1]].set(v)


def _prep_w_in(w):
    sizes = (256, 256, 256, 256, 512, 4, 256, 128, 32, 256, 256, 256, 4, 4, 256)
    cuts = [0]
    for s in sizes:
        cuts.append(cuts[-1] + s)
    (a_q, a_k, a_v, b_z, b_xbc, b_dt, c_q, c_kv, c_kr, m_q, m_k, m_v, m_i, m_f, m_o) = (
        w[..., cuts[n]:cuts[n + 1]] for n in range(len(sizes)))
    kr4 = jnp.concatenate([c_kr] * N_HEADS, axis=-1)
    krr4 = jnp.concatenate([_rot_cols(c_kr)] * N_HEADS, axis=-1)
    gates = _pad_lanes(jnp.concatenate([b_dt, m_i, m_f], axis=-1))
    out = jnp.concatenate([a_q, a_k, a_v, b_z, b_xbc, c_q, c_kv, kr4, krr4, gates, m_q, m_k, m_v, m_o], axis=-1)
    assert out.shape[-1] == N_PROJ
    return out.astype(BF16)


def _prep_w_uq(w):
    d = w.shape[0]
    w4 = w.reshape(d, 256, N_HEADS, C_NOPE + C_ROPE)
    nope = w4[..., :C_NOPE].reshape(d, 256, N_HEADS * C_NOPE)
    rope = w4[..., C_NOPE:]
    return jnp.concatenate([nope, rope.reshape(d, 256, N_HEADS * C_ROPE),
                            _rot_cols(rope).reshape(d, 256, N_HEADS * C_ROPE)], axis=-1).astype(BF16)


def _prep_w_ukv(w):
    d = w.shape[0]
    w4 = w.reshape(d, C_KV_LORA, N_HEADS, C_NOPE + HD)
    w_uk = w4[..., :C_NOPE]
    w_uv = w4[..., C_NOPE:]
    wabs = jnp.zeros((d, N_HEADS * C_NOPE, N_HEADS * C_KV_LORA), w.dtype)
    for h in range(N_HEADS):
        wabs = wabs.at[:, h * C_NOPE:(h + 1) * C_NOPE, h * C_KV_LORA:(h + 1) * C_KV_LORA].set(
            jnp.swapaxes(w_uk[:, :, h, :], 1, 2))
    wuv = jnp.transpose(w_uv, (0, 2, 1, 3))
    wuv_all = w_uv.reshape(d, C_KV_LORA, N_HEADS * HD)
    return wabs.astype(BF16), wuv.astype(BF16), wuv_all.astype(BF16)


def _rope_tables(pos):
    half = C_ROPE // 2
    inv = jnp.exp(-math.log(ROPE_BASE) * jnp.arange(half, dtype=F32) / half)
    ang = pos.astype(F32)[:, None] * inv[None, :]
    reps = LANE // half
    return jnp.tile(jnp.cos(ang), (1, reps)), jnp.tile(jnp.sin(ang), (1, reps))


def _band_bias(table, q_rows):
    cols = q_rows + A_SPAN
    k = np.arange(q_rows + cols - 1)
    u = table[:, np.clip((q_rows - 1 - k) + A_SPAN, -A_REL_CLIP, A_REL_CLIP) + A_REL_CLIP].astype(F32)
    period = q_rows + cols
    u = jnp.pad(u, ((0, 0), (0, 1)))
    flat = jnp.tile(u, (1, q_rows))[:, q_rows - 1:q_rows - 1 + q_rows * (period - 1)]
    toep = flat.reshape(table.shape[0], q_rows, period - 1)[:, :, :cols]
    r = np.arange(q_rows)[:, None]
    s = np.arange(cols)[None, :] - (r // CHUNK) * CHUNK
    inband = (s >= 0) & (s < A_SPAN + CHUNK)
    return jnp.where(jnp.asarray(inband)[None], toep, NEG)


def kernel(x_prompt, x_sample, cache_attn_k, cache_attn_v, state_ssm_conv, state_ssm, cache_mla_ckv, cache_mla_kr,
           state_mlstm_c, state_mlstm_n, state_mlstm_m, norm1_g, w_in, attn_rel_bias, ssm_conv_w, ssm_conv_b,
           ssm_dt_bias, ssm_a_log, ssm_d, ssm_norm_g, mla_q_norm_g, mla_w_uq, mla_kv_norm_g, mla_w_ukv,
           mlstm_b_i, mlstm_b_f, mlstm_norm_g, w_out, norm2_g, mlp_w1, mlp_w2, final_g):
    batch, seq, _ = x_prompt.shape
    dbatch, dseq, _ = x_sample.shape
    depth = w_in.shape[0]
    past = cache_mla_ckv.shape[2]
    tp, ts = batch * seq, dbatch * dseq
    lp = min(REC_CHUNK_PROMPT, seq)
    assert dseq == CHUNK and past % CHUNK == 0 and cache_attn_k.shape[2] == A_SPAN
    assert seq % C_TILE == 0 and tp % ROW_TILE == 0 and ts % ROW_TILE == 0

    w_in_r = _prep_w_in(w_in)
    wuq_r = _prep_w_uq(mla_w_uq)
    wabs, wuv, wuv_all = _prep_w_ukv(mla_w_ukv)
    w_out_b, w1_b, w2_b = w_out.astype(BF16), mlp_w1.astype(BF16), mlp_w2.astype(BF16)
    pos = jnp.concatenate([jnp.tile(jnp.arange(seq), batch), jnp.tile(past + jnp.arange(dseq), dbatch)])
    cos, sin = _rope_tables(pos)
    dtb = _pad_lanes(ssm_dt_bias, at=G_DT)[:, None, :]
    alog = _pad_lanes(ssm_a_log, at=G_DT)[:, None, :]
    dsk = _pad_lanes(ssm_d)[:, None, :]
    gate_bias = (_pad_lanes(mlstm_b_i, at=G_I) + _pad_lanes(mlstm_b_f, at=G_F))[:, None, :]
    gate_bias_t = jnp.swapaxes(gate_bias[:, :, :GT_ROWS], 1, 2)
    cache_k = cache_attn_k.reshape(depth, dbatch, A_SPAN, 256)
    cache_v = cache_attn_v.reshape(depth, dbatch, A_SPAN, 256)
    conv0_s = jnp.pad(state_ssm_conv, ((0, 0), (0, 0), (8 - (B_CONV - 1), 0), (0, 0)))
    caug0_s = jnp.concatenate([state_mlstm_c, state_mlstm_n[..., None],
                               jnp.zeros(state_mlstm_c.shape[:-1] + (LANE - HD - 1,), F32)], axis=-1)
    m0_s = _pad_lanes(state_mlstm_m)[:, :, None, :]
    conv0_p = jnp.zeros((batch, 8, B_CONV_DIM), F32)
    h0_p = jnp.zeros((batch, N_HEADS, HD, B_STATE), F32)

    xp, xs = x_prompt.reshape(tp, D_MODEL), x_sample.reshape(ts, D_MODEL)
    outs = [[] for _ in range(18)]
    ya, yb, yc, yd = (jnp.zeros((tp + ts, GROUP_W), BF16) for _ in range(4))
    fg = final_g[None, :]
    for l in range(depth):
        (aq, akv, bz, bxbc, qabs, qt, ckv, kr, kc, vt, mqkv, mo, gates, mkt, mvt, gt) = _inproj(
            xp, xs, norm1_g[l][None], w_in_r, mla_q_norm_g[l][None], mla_kv_norm_g[l][None], wuq_r, wabs,
            wuv_all, cos, sin, l)

        ya = _attn_a_prompt(ya, aq, akv, _band_bias(attn_rel_bias[l], A_QBLOCK), batch, seq)
        ya = _attn_a_sample(ya, aq, akv, cache_k, cache_v, _band_bias(attn_rel_bias[l], CHUNK), l, tp, dbatch, dseq)

        ssd_w = (ssm_conv_w[l], ssm_conv_b[l][None], dtb[l], alog[l], dsk[l], ssm_norm_g[l][None])
        yb, h_p = _ssd(yb, bz, bxbc, gates, conv0_p, h0_p, *ssd_w, 0, batch, seq, lp)
        yb, h_s = _ssd(yb, bz, bxbc, gates, conv0_s[l], state_ssm[l], *ssd_w, tp, dbatch, dseq, dseq)

        yc = _mla_prompt(yc, qt, kc, vt, batch, seq)
        yc = _mla_sample(yc, qabs, kc, cache_mla_ckv, cache_mla_kr, wuv[l], l, tp, dbatch, dseq)

        ml_w = (gate_bias[l], mlstm_norm_g[l][None])
        yd, ct_p, m_p = _mlstm_prompt(yd, mqkv, mkt, mvt, mo, gates, gt, gate_bias[l], gate_bias_t[l],
                                      mlstm_norm_g[l][None], batch, seq, lp)
        yd, c_s, m_s = _mlstm(yd, mqkv, mo, gates, caug0_s[l], m0_s[l], *ml_w, tp, dbatch, dseq, dseq)

        xp, xs = _outmlp(xp, xs, ya, yb, yc, yd, w_out_b, norm2_g[l][None], w1_b, w2_b, fg, l, l == depth - 1, l > 0)

        keep = min(A_SPAN, seq)
        tail = lambda a, n: jnp.stack([a[(b + 1) * seq - n:(b + 1) * seq] for b in range(batch)])
        akv_p = tail(akv, keep).reshape(batch, keep, 2, N_HEADS, HD)
        akv_s = akv[tp:].reshape(dbatch, dseq, 2, N_HEADS, HD)
        new = (akv_p[:, :, 0], akv_p[:, :, 1], tail(bxbc, B_CONV - 1), h_p,
               ckv[:tp].reshape(batch, seq, C_KV_LORA), kr[:tp].reshape(batch, seq, C_ROPE),
               jnp.swapaxes(ct_p[:, :, :HD, :], 2, 3), ct_p[:, :, HD, :], m_p[:, 0, :N_HEADS],
               jnp.concatenate([cache_attn_k[l][:, dseq:], akv_s[:, :, 0]], axis=1),
               jnp.concatenate([cache_attn_v[l][:, dseq:], akv_s[:, :, 1]], axis=1),
               bxbc[tp:].reshape(dbatch, dseq, B_CONV_DIM)[:, dseq - (B_CONV - 1):], h_s,
               ckv[tp:].reshape(dbatch, dseq, C_KV_LORA), kr[tp:].reshape(dbatch, dseq, C_ROPE),
               c_s[..., :HD], c_s[..., HD], m_s[:, 0, :N_HEADS])
        for dst, v in zip(outs, new):
            dst.append(v)

    y_prompt = xp.reshape(batch, seq, D_MODEL)
    y_sample = xs.reshape(dbatch, dseq, D_MODEL)
    return (y_prompt, y_sample) + tuple(jnp.stack(v) for v in outs)
```

```python
import functools
import math

import jax
import jax.numpy as jnp
import numpy as np
from jax import lax
from jax.experimental import pallas as pl
from jax.experimental.pallas import tpu as pltpu

F32 = jnp.float32
BF16 = jnp.bfloat16
EPS = 1e-6
NEG = -1e30
ROPE_BASE = 10000.0

CHUNK = 64
D_MODEL = 1024
GROUP_W = 256
D_FF = 4096
N_HEADS = 4
HD = 64
A_BAND_CHUNKS = 8
A_SPAN = A_BAND_CHUNKS * CHUNK
A_REL_CLIP = 128
B_GROUPS = 2
B_STATE = 64
B_CONV = 4
B_CONV_DIM = 512
C_NOPE = 64
C_ROPE = 32
C_KV_LORA = 128
A_QSCALE = HD ** -0.5 * math.log2(math.e)
C_SCALE = (C_NOPE + C_ROPE) ** -0.5
C_QSCALE = C_SCALE * math.log2(math.e)
V_ROWS = 80
LANE = 128
VMEM_LIMIT = 56 * 1024 * 1024

OFF_A, OFF_BZ, OFF_XBC, OFF_CQ, OFF_CKV, OFF_KR, OFF_KRR, OFF_G, OFF_MQKV, OFF_MO, N_PROJ = (
    0, 768, 1024, 1536, 1792, 1920, 2048, 2176, 2304, 3072, 3328)
G_DT, G_I, G_F = 0, 4, 8
GT_ROWS = 16

ROW_TILE = 512
A_QBLOCK = 256
C_TILE = 256
C_QTILE = 2 * C_TILE
C_UNROLL_PAIRS = 4
REC_CHUNK_PROMPT = 256


def _cparams(sem):
    return pltpu.CompilerParams(dimension_semantics=sem, vmem_limit_bytes=VMEM_LIMIT)


def _rms(x, g):
    return x * lax.rsqrt(jnp.mean(x * x, axis=-1, keepdims=True) + EPS) * g


def _dot(a, b):
    return jnp.dot(a, b, preferred_element_type=F32)


def _dot_nt(a, b):
    return lax.dot_general(a, b, (((1,), (1,)), ((), ())), preferred_element_type=F32)


def _dot_tn(a, b):
    return lax.dot_general(a, b, (((0,), (0,)), ((), ())), preferred_element_type=F32)


def _cumsum_rows(x, tri_bf16):
    hi = x.astype(BF16)
    r1 = x - hi.astype(F32)
    mid = r1.astype(BF16)
    lo = (r1 - mid.astype(F32)).astype(BF16)
    return _dot(tri_bf16, hi) + _dot(tri_bf16, mid) + _dot(tri_bf16, lo)


def _log1p_exp_neg_abs(x):
    return jnp.log1p(jnp.exp(-jnp.abs(x)))


def _full_spec(shape):
    n = len(shape)
    return pl.BlockSpec(shape, lambda *_: (0,) * n)


_DST_SPEC = pl.BlockSpec(memory_space=pl.ANY)


def _const_spec(shape):
    n = len(shape)
    return pl.BlockSpec(shape, lambda *_: (0,) * n, pipeline_mode=pl.Buffered(1))


def _layer_spec(shape, layer):
    n = len(shape)
    return pl.BlockSpec((None,) + tuple(shape), lambda *_: (layer,) + (0,) * n, pipeline_mode=pl.Buffered(1))


def _split_rows(n_prompt_tiles, width):
    npt = n_prompt_tiles
    return (pl.BlockSpec((ROW_TILE, width), lambda i: (jnp.minimum(i, npt - 1), 0)),
            pl.BlockSpec((ROW_TILE, width), lambda i: (jnp.maximum(i - npt, 0), 0)))


def _inproj_kernel(xp_ref, xs_ref, g1_ref, w_ref, qg_ref, kvg_ref, wuq_ref, wabs_ref, wuv_ref, cos_ref, sin_ref,
                   aq_ref, akv_ref, bz_ref, bxbc_ref, qabs_ref, qt_ref, ckv_ref, kr_ref, kc_ref, vt_ref,
                   mqkv_ref, mo_ref, gates_ref, mkt_ref, mvt_ref, gt_ref, aqt_ref, akh_ref, avt_ref, *,
                   n_prompt_tiles):
    tm = xp_ref.shape[0]
    is_prompt = pl.program_id(0) < n_prompt_tiles
    x = jnp.where(is_prompt, xp_ref[...], xs_ref[...])
    hn = _rms(x, g1_ref[...]).astype(BF16)

    def seg(a, b):
        return _dot(hn, w_ref[:, a:b])

    a = seg(OFF_A, OFF_BZ)
    aq = a[:, :GROUP_W] * A_QSCALE
    aq_ref[...] = aq.astype(BF16)
    akv_ref[...] = a[:, GROUP_W:]
    aqt_ref[...] = aq.T.astype(BF16)
    avt_ref[...] = a[:, 2 * GROUP_W:].T.astype(BF16)
    for h in range(N_HEADS):
        akh_ref[h] = a[:, GROUP_W + h * HD:GROUP_W + (h + 1) * HD].astype(BF16)
    bz_ref[...] = seg(OFF_BZ, OFF_XBC)
    bxbc_ref[...] = seg(OFF_XBC, OFF_CQ)

    cos = cos_ref[...]
    sin = sin_ref[...]
    cqn = _rms(seg(OFF_CQ, OFF_CKV), qg_ref[...]).astype(BF16)
    qf = _dot(cqn, wuq_ref[...])
    qrope = (qf[:, 256:384] * cos + qf[:, 384:512] * sin) * C_QSCALE
    qlat = _dot(qf[:, :256].astype(BF16), wabs_ref[...]) * C_QSCALE
    lane_head = lax.broadcasted_iota(jnp.int32, (tm, LANE), 1) // C_ROPE
    qhs = [jnp.concatenate([qlat[:, h * LANE:(h + 1) * LANE], jnp.where(lane_head == h, qrope, 0.0)], axis=-1)
           for h in range(N_HEADS)]

    ckv_kr = seg(OFF_CKV, OFF_KRR)
    krr_g = seg(OFF_KRR, OFF_MQKV)
    g = krr_g[:, LANE:]
    ckv = _rms(ckv_kr[:, :C_KV_LORA], kvg_ref[...])
    kr4 = ckv_kr[:, C_KV_LORA:] * cos + krr_g[:, :LANE] * sin
    ckv_ref[...] = ckv
    kr_ref[...] = kr4[:, :C_ROPE]
    kc_ref[...] = jnp.concatenate([ckv, kr4], axis=-1).astype(BF16)

    m = seg(OFF_MQKV, OFF_MO)
    mk = m[:, 256:512] * HD ** -0.5
    mqkv_ref[:, 0:256] = m[:, 0:256].astype(BF16)
    mqkv_ref[:, 256:512] = mk.astype(BF16)
    mqkv_ref[:, 512:768] = m[:, 512:768].astype(BF16)
    mo_ref[...] = seg(OFF_MO, N_PROJ)
    gates_ref[...] = g

    for h in range(N_HEADS):
        qt_ref[h] = qhs[h].T.astype(BF16)
    vall = _dot(ckv.astype(BF16), wuv_ref[...])
    ones_c = (lax.broadcasted_iota(jnp.int32, (V_ROWS - HD, C_TILE), 0) == 0).astype(BF16)
    for s in range(tm // C_TILE):
        vt = vall[s * C_TILE:(s + 1) * C_TILE, :].T
        for h in range(N_HEADS):
            vt_ref[s, h, 0:HD, :] = vt[h * HD:(h + 1) * HD].astype(BF16)
            vt_ref[s, h, HD:V_ROWS, :] = ones_c
    mkt = mk.T
    mvt = m[:, 512:768].T
    ones_v = (lax.broadcasted_iota(jnp.int32, (LANE - HD, tm), 0) == 0).astype(BF16)
    for h in range(N_HEADS):
        mkt_ref[h] = mkt[h * HD:(h + 1) * HD].astype(BF16)
        mvt_ref[h, 0:HD, :] = mvt[h * HD:(h + 1) * HD].astype(BF16)
        mvt_ref[h, HD:LANE, :] = ones_v
    gt_ref[...] = g.T[0:GT_ROWS]

    @pl.when(jnp.logical_not(is_prompt))
    def _():
        for h in range(N_HEADS):
            qabs_ref[h] = qhs[h].astype(BF16)


def _inproj(xp, xs, g1, w, qg, kvg, wuq, wabs, wuv_all, cos, sin, layer):
    tp, ts = xp.shape[0], xs.shape[0]
    t = tp + ts
    tm = ROW_TILE
    npt = tp // tm
    col = lambda *lead: pl.BlockSpec((*lead, tm), lambda i: (0,) * len(lead) + (i,))
    row = lambda n: pl.BlockSpec((tm, n), lambda i: (i, 0))
    out_shape = (
        jax.ShapeDtypeStruct((t, 256), BF16),
        jax.ShapeDtypeStruct((t, 512), F32),
        jax.ShapeDtypeStruct((t, 256), F32),
        jax.ShapeDtypeStruct((t, 512), F32),
        jax.ShapeDtypeStruct((N_HEADS, ts, 256), BF16),
        jax.ShapeDtypeStruct((N_HEADS, 256, t), BF16),
        jax.ShapeDtypeStruct((t, 128), F32),
        jax.ShapeDtypeStruct((t, C_ROPE), F32),
        jax.ShapeDtypeStruct((t, 256), BF16),
        jax.ShapeDtypeStruct((t // C_TILE, N_HEADS, V_ROWS, C_TILE), BF16),
        jax.ShapeDtypeStruct((t, 768), BF16),
        jax.ShapeDtypeStruct((t, 256), F32),
        jax.ShapeDtypeStruct((t, LANE), F32),
        jax.ShapeDtypeStruct((N_HEADS, HD, t), BF16),
        jax.ShapeDtypeStruct((N_HEADS, LANE, t), BF16),
        jax.ShapeDtypeStruct((GT_ROWS, t), F32),
        jax.ShapeDtypeStruct((GROUP_W, t), BF16),
        jax.ShapeDtypeStruct((N_HEADS, t, HD), BF16),
        jax.ShapeDtypeStruct((GROUP_W, t), BF16),
    )
    out_specs = (
        row(256), row(512), row(256), row(512),
        pl.BlockSpec((N_HEADS, tm, 256), lambda i: (0, jnp.maximum(i - npt, 0), 0)),
        col(N_HEADS, 256),
        row(128), row(C_ROPE), row(256),
        pl.BlockSpec((tm // C_TILE, N_HEADS, V_ROWS, C_TILE), lambda i: (i, 0, 0, 0)),
        row(768), row(256), row(LANE),
        col(N_HEADS, HD), col(N_HEADS, LANE), col(GT_ROWS),
        col(GROUP_W), pl.BlockSpec((N_HEADS, tm, HD), lambda i: (0, i, 0)), col(GROUP_W),
    )
    in_specs = [*_split_rows(npt, D_MODEL), _full_spec((1, D_MODEL)), _layer_spec((D_MODEL, N_PROJ), layer),
                _full_spec((1, 256)), _full_spec((1, 128)), _layer_spec((256, 512), layer),
                _layer_spec((256, 512), layer), _layer_spec((C_KV_LORA, 256), layer), row(LANE), row(LANE)]
    return pl.pallas_call(
        functools.partial(_inproj_kernel, n_prompt_tiles=npt), grid=(t // tm,), in_specs=in_specs,
        out_specs=out_specs, out_shape=out_shape,
        compiler_params=_cparams(("arbitrary",)), name="inproj",
    )(xp, xs, g1, w, qg, kvg, wuq, wabs, wuv_all, cos, sin)


def _band_heads(q, kb, vb, bias_ref, valid, o_ref):
    for h in range(N_HEADS):
        sl = slice(h * HD, (h + 1) * HD)
        s = _dot_nt(q[:, sl], kb[:, sl]) + bias_ref[h]
        if valid is not None:
            s = jnp.where(valid, s, NEG)
        m = jnp.max(s, axis=-1, keepdims=True)
        p = jnp.exp2(s - m)
        l = jnp.sum(p, axis=-1, keepdims=True)
        o = _dot(p.astype(BF16), vb[:, sl]) / l
        o_ref[:, sl] = o.astype(o_ref.dtype)


def _attn_a_prompt_kernel(dst_ref, qt_ref, k0_ref, k1_ref, k2_ref, v0_ref, v1_ref, v2_ref, bias_ref, o_ref, s_ref):
    i = pl.program_id(1)
    nk = 3 * A_QBLOCK
    vt = jnp.concatenate([v0_ref[...], v1_ref[...], v2_ref[...]], axis=1)
    ones = (lax.broadcasted_iota(jnp.int32, (V_ROWS - HD, nk), 0) == 0).astype(BF16)

    def scores(h):
        kh = jnp.concatenate([k0_ref[h], k1_ref[h], k2_ref[h]], axis=0)
        s_ref[h % 2] = _dot(kh, qt_ref[h * HD:(h + 1) * HD, :]) + bias_ref[h]

    def heads(masked):
        outs = []
        scores(0)
        for h in range(N_HEADS):
            if h + 1 < N_HEADS:
                scores(h + 1)
            s = s_ref[h % 2]
            if masked:
                krow = lax.broadcasted_iota(jnp.int32, (nk, A_QBLOCK), 0)
                s = jnp.where(krow >= (2 - i) * A_QBLOCK, s, NEG)
            m = jnp.max(s, axis=0, keepdims=True)
            p = jnp.exp2(s - m).astype(BF16)
            vaug = jnp.concatenate([vt[h * HD:(h + 1) * HD, :], ones], axis=0)
            o = _dot(vaug, p)
            outs.append(o[:HD] / o[HD:HD + 1])
        o_ref[...] = jnp.concatenate(outs, axis=0).T.astype(o_ref.dtype)

    @pl.when(i < 2)
    def _():
        heads(True)

    @pl.when(i >= 2)
    def _():
        heads(False)


def _attn_a_prompt(dst, aqt, akh, avt, biasm_t, batch, seq):
    nb = seq // A_QBLOCK
    qb = A_QBLOCK
    back = lambda b, i, n: b * nb + jnp.maximum(i - n, 0)
    kspec = lambda n: pl.BlockSpec((N_HEADS, qb, HD), lambda b, i: (0, back(b, i, n), 0))
    vspec = lambda n: pl.BlockSpec((GROUP_W, qb), lambda b, i: (0, back(b, i, n)))
    return pl.pallas_call(
        _attn_a_prompt_kernel, grid=(batch, nb),
        in_specs=[_DST_SPEC, pl.BlockSpec((GROUP_W, qb), lambda b, i: (0, b * nb + i)),
                  kspec(2), kspec(1), kspec(0), vspec(2), vspec(1), vspec(0), _full_spec(biasm_t.shape)],
        out_specs=pl.BlockSpec((qb, 256), lambda b, i: (b * nb + i, 0)),
        out_shape=jax.ShapeDtypeStruct(dst.shape, dst.dtype), input_output_aliases={0: 0},
        scratch_shapes=[pltpu.VMEM((2, 3 * qb, qb), F32)],
        compiler_params=_cparams(("parallel", "parallel")), name="attn_a_prompt",
    )(dst, aqt, akh, akh, akh, avt, avt, avt, biasm_t)


def _attn_a_sample_kernel(dst_ref, q_ref, ck_ref, cv_ref, kvn_ref, bias_ref, o_ref):
    kvn = kvn_ref[...]
    kb = jnp.concatenate([ck_ref[...], kvn[:, :GROUP_W]], axis=0).astype(BF16)
    vb = jnp.concatenate([cv_ref[...], kvn[:, GROUP_W:]], axis=0).astype(BF16)
    _band_heads(q_ref[...], kb, vb, bias_ref, None, o_ref)


def _attn_a_sample(dst, aq, akv, cache_k, cache_v, bias, layer, row0, n_seq, seq):
    blk0 = row0 // seq
    cspec = pl.BlockSpec((None, None, A_SPAN, 256), lambda b: (layer, b, 0, 0))
    return pl.pallas_call(
        _attn_a_sample_kernel, grid=(n_seq,),
        in_specs=[_DST_SPEC, pl.BlockSpec((seq, 256), lambda b: (blk0 + b, 0)), cspec, cspec,
                  pl.BlockSpec((seq, 512), lambda b: (blk0 + b, 0)), _full_spec(bias.shape)],
        out_specs=pl.BlockSpec((seq, 256), lambda b: (blk0 + b, 0)),
        out_shape=jax.ShapeDtypeStruct(dst.shape, dst.dtype), input_output_aliases={0: 0},
        compiler_params=_cparams(("parallel",)), name="attn_a_sample",
    )(dst, aq, cache_k, cache_v, akv, bias)


def _ssd_kernel(dst_ref, z_ref, xbc_ref, g_ref, conv0_ref, h0_ref, cw_ref, cb_ref, dtb_ref, alog_ref, dsk_ref, ng_ref,
                y_ref, hout_ref, xpad_ref, h_ref):
    L = z_ref.shape[0]
    c = pl.program_id(1)

    @pl.when(c == 0)
    def _():
        xpad_ref[...] = conv0_ref[...]
        h_ref[...] = h0_ref[...]

    xbc = xbc_ref[...]
    xfull = jnp.concatenate([xpad_ref[...], xbc], axis=0)
    conv = cb_ref[...] + xbc * cw_ref[B_CONV - 1:B_CONV, :]
    for j in range(B_CONV - 1):
        conv = conv + pltpu.roll(xfull, B_CONV - 1 - j, axis=0)[8:8 + L, :] * cw_ref[j:j + 1, :]
    xpad_ref[...] = xbc[L - 8:L, :]
    u = conv * jax.nn.sigmoid(conv)

    g = g_ref[...] + dtb_ref[...]
    dt = jnp.maximum(g, 0.0) + _log1p_exp_neg_abs(g)
    da = dt * (-jnp.exp(alog_ref[...]))
    row = lax.broadcasted_iota(jnp.int32, (L, L), 0)
    colm = lax.broadcasted_iota(jnp.int32, (L, L), 1)
    tri = row >= colm
    cs = _cumsum_rows(da, tri.astype(BF16))
    cs_t = cs.T

    ys = []
    gmat = [None] * B_GROUPS
    for h in range(N_HEADS):
        grp = h // (N_HEADS // B_GROUPS)
        xs = u[:, h * HD:(h + 1) * HD]
        bm = u[:, 256 + grp * B_STATE:256 + (grp + 1) * B_STATE].astype(BF16)
        cm = u[:, 384 + grp * B_STATE:384 + (grp + 1) * B_STATE].astype(BF16)
        if gmat[grp] is None:
            gmat[grp] = _dot_nt(cm, bm)
        csc = cs[:, G_DT + h:G_DT + h + 1]
        csr = cs_t[G_DT + h:G_DT + h + 1, :]
        tot = cs[L - 1:L, G_DT + h:G_DT + h + 1]
        dec = jnp.exp(jnp.where(tri, csc - csr, NEG))
        xdt = xs * dt[:, G_DT + h:G_DT + h + 1]
        hprev = h_ref[h]
        y = _dot((gmat[grp] * dec).astype(BF16), xdt.astype(BF16))
        y = y + _dot_nt(cm, hprev.astype(BF16)) * jnp.exp(csc)
        y = y + dsk_ref[0:1, h:h + 1] * xs
        ys.append(y)
        wend = (xdt * jnp.exp(tot - csc)).astype(BF16)
        h_ref[h] = jnp.exp(tot) * hprev + _dot_tn(wend, bm)
    y = jnp.concatenate(ys, axis=-1)
    z = z_ref[...]
    y = y * (z * jax.nn.sigmoid(z))
    y_ref[...] = _rms(y, ng_ref[...]).astype(y_ref.dtype)

    @pl.when(c == pl.num_programs(1) - 1)
    def _():
        hout_ref[...] = h_ref[...]


def _ssd(dst, bz, bxbc, gates, conv0, h0, cw, cb, dtb, alog, dsk, ng, row0, n_seq, seq, L):
    nc = seq // L
    blk0 = row0 // L
    rspec = lambda n: pl.BlockSpec((L, n), lambda b, c: (blk0 + b * nc + c, 0))
    return pl.pallas_call(
        _ssd_kernel, grid=(n_seq, nc),
        in_specs=[_DST_SPEC, rspec(256), rspec(512), rspec(LANE),
                  pl.BlockSpec((None, 8, 512), lambda b, c: (b, 0, 0)),
                  pl.BlockSpec((None, N_HEADS, HD, B_STATE), lambda b, c: (b, 0, 0, 0)),
                  _full_spec((B_CONV, 512)), _full_spec((1, 512)), _full_spec((1, LANE)), _full_spec((1, LANE)),
                  _full_spec((1, LANE)), _full_spec((1, 256))],
        out_specs=(rspec(256), pl.BlockSpec((None, N_HEADS, HD, B_STATE), lambda b, c: (b, 0, 0, 0))),
        input_output_aliases={0: 0},
        out_shape=(jax.ShapeDtypeStruct(dst.shape, dst.dtype),
                   jax.ShapeDtypeStruct((n_seq, N_HEADS, HD, B_STATE), F32)),
        scratch_shapes=[pltpu.VMEM((8, 512), F32), pltpu.VMEM((N_HEADS, HD, B_STATE), F32)],
        compiler_params=_cparams(("parallel", "arbitrary")), name="ssd",
    )(dst, bz, bxbc, gates, conv0, h0, cw, cb, dtb, alog, dsk, ng)


def _mlstm_kernel(dst_ref, qkv_ref, o_ref, g_ref, c0_ref, m0_ref, bias_ref, ng_ref,
                  y_ref, cout_ref, mout_ref, c_ref, m_ref):
    L = qkv_ref.shape[0]
    c = pl.program_id(1)

    @pl.when(c == 0)
    def _():
        c_ref[...] = c0_ref[...]
        m_ref[...] = m0_ref[...]

    g = g_ref[...] + bias_ref[...]
    lf = jnp.minimum(g, 0.0) - _log1p_exp_neg_abs(g)
    row = lax.broadcasted_iota(jnp.int32, (L, L), 0)
    colm = lax.broadcasted_iota(jnp.int32, (L, L), 1)
    tri = row >= colm
    bcum = _cumsum_rows(lf, tri.astype(BF16))
    bcum_t = bcum.T
    g_t = g.T
    unit = (lax.broadcasted_iota(jnp.int32, (L, HD), 1) == 0).astype(BF16)
    ng = ng_ref[...]
    og = o_ref[...]

    for h in range(N_HEADS):
        sl = slice(h * HD, (h + 1) * HD)
        q = qkv_ref[:, h * HD:(h + 1) * HD]
        k = qkv_ref[:, 256 + h * HD:256 + (h + 1) * HD]
        v = qkv_ref[:, 512 + h * HD:512 + (h + 1) * HD]
        vaug = jnp.concatenate([v, unit], axis=-1)
        bc = bcum[:, G_F + h:G_F + h + 1]
        br = bcum_t[G_F + h:G_F + h + 1, :]
        ic = g[:, G_I + h:G_I + h + 1]
        ir = g_t[G_I + h:G_I + h + 1, :]
        tot = bcum[L - 1:L, G_F + h:G_F + h + 1]
        mprev = m_ref[0:1, h:h + 1]

        dm = jnp.where(tri, bc - br + ir, NEG)
        inter = bc + mprev
        mt = jnp.maximum(inter, jnp.max(dm, axis=-1, keepdims=True))
        w_intra = jnp.exp(dm - mt)
        w_inter = jnp.exp(inter - mt)
        caug = c_ref[h]
        s = _dot_nt(q, k) * w_intra
        r = w_inter * _dot(q, caug.astype(BF16)) + _dot(s.astype(BF16), vaug)
        num = r[:, :HD]
        den = r[:, HD:HD + 1]
        hout = num / jnp.maximum(jnp.abs(den), jnp.exp(-mt))

        gend = tot - bc + ic
        mnew = jnp.maximum(tot + mprev, jnp.max(gend, axis=0, keepdims=True))
        kw = (k.astype(F32) * jnp.exp(gend - mnew)).astype(BF16)
        c_ref[h] = jnp.exp(tot + mprev - mnew) * caug + _dot_tn(kw, vaug)
        m_ref[0:1, h:h + 1] = mnew

        yh = _rms(hout, ng[:, sl]) * jax.nn.sigmoid(og[:, sl])
        y_ref[:, sl] = yh.astype(y_ref.dtype)

    @pl.when(c == pl.num_programs(1) - 1)
    def _():
        cout_ref[...] = c_ref[...]
        mout_ref[...] = m_ref[...]


def _cumsum_lanes(x, tri_bf16):
    hi = x.astype(BF16)
    r1 = x - hi.astype(F32)
    mid = r1.astype(BF16)
    lo = (r1 - mid.astype(F32)).astype(BF16)
    return _dot(hi, tri_bf16) + _dot(mid, tri_bf16) + _dot(lo, tri_bf16)


def _mlstm_prompt_kernel(dst_ref, qkv_ref, kt_ref, vt_ref, o_ref, g_ref, gt_ref, bias_ref, biast_ref, ng_ref,
                         y_ref, cout_ref, mout_ref, c_ref, m_ref):
    L = qkv_ref.shape[0]
    c = pl.program_id(1)

    @pl.when(c == 0)
    def _():
        c_ref[...] = jnp.zeros_like(c_ref)
        m_ref[...] = jnp.zeros_like(m_ref)

    g = g_ref[...] + bias_ref[...]
    gt = gt_ref[...] + biast_ref[...]
    lf = jnp.minimum(g, 0.0) - _log1p_exp_neg_abs(g)
    lft = jnp.minimum(gt, 0.0) - _log1p_exp_neg_abs(gt)
    row = lax.broadcasted_iota(jnp.int32, (L, L), 0)
    colm = lax.broadcasted_iota(jnp.int32, (L, L), 1)
    s_le_l = row <= colm
    bcum = _cumsum_rows(lf, (row >= colm).astype(BF16))
    bcum_t = _cumsum_lanes(lft, s_le_l.astype(BF16))

    yts = []
    for h in range(N_HEADS):
        q = qkv_ref[:, h * HD:(h + 1) * HD]
        k = qkv_ref[:, 256 + h * HD:256 + (h + 1) * HD]
        src = g[:, G_I + h:G_I + h + 1] - bcum[:, G_F + h:G_F + h + 1]
        bc = bcum_t[G_F + h:G_F + h + 1, :]
        ir = gt[G_I + h:G_I + h + 1, :]
        tot = bcum_t[G_F + h:G_F + h + 1, L - 1:L]
        mprev = m_ref[0:1, h:h + 1]

        dmt = jnp.where(s_le_l, bc + src, NEG)
        inter = bc + mprev
        mt = jnp.maximum(inter, jnp.max(dmt, axis=0, keepdims=True))
        w_intra = jnp.exp(dmt - mt)
        w_inter = jnp.exp(inter - mt)
        ct = c_ref[h]
        st = _dot_nt(k, q) * w_intra
        rt = w_inter * _dot_nt(ct.astype(BF16), q) + _dot(vt_ref[h], st.astype(BF16))
        hout = rt[:HD] / jnp.maximum(jnp.abs(rt[HD:HD + 1]), jnp.exp(-mt))
        yts.append(hout * lax.rsqrt(jnp.mean(hout * hout, axis=0, keepdims=True) + EPS))

        gend = tot - bc + ir
        mnew = jnp.maximum(tot + mprev, jnp.max(gend, axis=-1, keepdims=True))
        kw = (kt_ref[h].astype(F32) * jnp.exp(gend - mnew)).astype(BF16)
        c_ref[h] = jnp.exp(tot + mprev - mnew) * ct + _dot_nt(vt_ref[h], kw)
        m_ref[0:1, h:h + 1] = mnew

    y = jnp.concatenate(yts, axis=0).T
    y_ref[...] = (y * ng_ref[...] * jax.nn.sigmoid(o_ref[...])).astype(y_ref.dtype)

    @pl.when(c == pl.num_programs(1) - 1)
    def _():
        cout_ref[...] = c_ref[...]
        mout_ref[...] = m_ref[...]


def _mlstm_prompt(dst, mqkv, mkt, mvt, mo, gates, gt, bias, biast, ng, n_seq, seq, L):
    nc = seq // L
    rspec = lambda n: pl.BlockSpec((L, n), lambda b, c: (b * nc + c, 0))
    cspec = lambda *lead: pl.BlockSpec((*lead, L), lambda b, c: (0,) * len(lead) + (b * nc + c,))
    stspec = pl.BlockSpec((None, N_HEADS, LANE, HD), lambda b, c: (b, 0, 0, 0))
    mspec = pl.BlockSpec((None, 1, LANE), lambda b, c: (b, 0, 0))
    return pl.pallas_call(
        _mlstm_prompt_kernel, grid=(n_seq, nc),
        in_specs=[_DST_SPEC, rspec(768), cspec(N_HEADS, HD), cspec(N_HEADS, LANE), rspec(256), rspec(LANE),
                  cspec(GT_ROWS), _full_spec((1, LANE)), _full_spec((GT_ROWS, 1)), _full_spec((1, 256))],
        out_specs=(rspec(256), stspec, mspec), input_output_aliases={0: 0},
        out_shape=(jax.ShapeDtypeStruct(dst.shape, dst.dtype),
                   jax.ShapeDtypeStruct((n_seq, N_HEADS, LANE, HD), F32),
                   jax.ShapeDtypeStruct((n_seq, 1, LANE), F32)),
        scratch_shapes=[pltpu.VMEM((N_HEADS, LANE, HD), F32), pltpu.VMEM((1, LANE), F32)],
        compiler_params=_cparams(("parallel", "arbitrary")), name="mlstm_prompt",
    )(dst, mqkv, mkt, mvt, mo, gates, gt, bias, biast, ng)


def _mlstm(dst, mqkv, mo, gates, c0, m0, bias, ng, row0, n_seq, seq, L):
    nc = seq // L
    blk0 = row0 // L
    rspec = lambda n: pl.BlockSpec((L, n), lambda b, c: (blk0 + b * nc + c, 0))
    cspec = pl.BlockSpec((None, N_HEADS, HD, LANE), lambda b, c: (b, 0, 0, 0))
    mspec = pl.BlockSpec((None, 1, LANE), lambda b, c: (b, 0, 0))
    return pl.pallas_call(
        _mlstm_kernel, grid=(n_seq, nc),
        in_specs=[_DST_SPEC, rspec(768), rspec(256), rspec(LANE), cspec, mspec, _full_spec((1, LANE)),
                  _full_spec((1, 256))],
        out_specs=(rspec(256), cspec, mspec), input_output_aliases={0: 0},
        out_shape=(jax.ShapeDtypeStruct(dst.shape, dst.dtype),
                   jax.ShapeDtypeStruct((n_seq, N_HEADS, HD, LANE), F32),
                   jax.ShapeDtypeStruct((n_seq, 1, LANE), F32)),
        scratch_shapes=[pltpu.VMEM((N_HEADS, HD, LANE), F32), pltpu.VMEM((1, LANE), F32)],
        compiler_params=_cparams(("parallel", "arbitrary")), name="mlstm",
    )(dst, mqkv, mo, gates, c0, m0, bias, ng)


def _mla_prompt_kernel(dst_ref, q_ref, kc_ref, vt_ref, o_ref, acc_ref, m_ref, s_ref):
    tq = q_ref.shape[2]
    tk = C_TILE
    i = pl.program_id(1)
    m_ref[...] = jnp.full_like(m_ref, NEG)
    acc_ref[...] = jnp.zeros_like(acc_ref)

    def scores(j, slot):
        start = pl.multiple_of(j * tk, tk)
        kt = kc_ref[pl.ds(start, tk), :]
        for h in range(N_HEADS):
            s_ref[slot, h] = _dot(kt, q_ref[h])

    def consume(j, slot, diag_offset=None):
        for h in range(N_HEADS):
            s = s_ref[slot, h]
            if diag_offset is not None:
                krow = (diag_offset + lax.broadcasted_iota(jnp.int32, (tk, tq), 0)) // CHUNK
                qcol = lax.broadcasted_iota(jnp.int32, (tk, tq), 1) // CHUNK
                s = jnp.where(krow <= qcol, s, NEG)
            m_old = m_ref[h]
            m_new = jnp.maximum(m_old, jnp.max(s, axis=0, keepdims=True))
            p = jnp.exp2(s - m_new).astype(BF16)
            acc_ref[h] = jnp.exp2(m_old - m_new) * acc_ref[h] + _dot(vt_ref[j, h], p)
            m_ref[h] = m_new

    scores(0, 0)

    def pair(j):
        scores(j + 1, 1)
        consume(j, 0)
        scores(j + 2, 0)
        consume(j + 1, 1)

    def body_main(jj, carry):
        for u in range(C_UNROLL_PAIRS):
            pair(2 * (C_UNROLL_PAIRS * jj + u))
        return carry

    def body_rest(jj, carry):
        pair(2 * (C_UNROLL_PAIRS * n_main + jj))
        return carry

    n_main = i // C_UNROLL_PAIRS
    lax.fori_loop(0, n_main, body_main, 0)
    lax.fori_loop(0, i - C_UNROLL_PAIRS * n_main, body_rest, 0)
    scores(2 * i + 1, 1)
    consume(2 * i, 0, 0)
    consume(2 * i + 1, 1, tk)

    ys = []
    for h in range(N_HEADS):
        acc = acc_ref[h]
        ys.append(acc[:HD] / acc[HD:HD + 1])
    o_ref[...] = jnp.concatenate(ys, axis=0).T.astype(o_ref.dtype)


def _mla_prompt(dst, qt, kc, vt, batch, seq):
    tq = C_QTILE
    nq = seq // tq
    return pl.pallas_call(
        _mla_prompt_kernel, grid=(batch, nq),
        in_specs=[_DST_SPEC, pl.BlockSpec((N_HEADS, 256, tq), lambda b, i: (0, 0, b * nq + i)),
                  pl.BlockSpec((seq, 256), lambda b, i: (b, 0)),
                  pl.BlockSpec((seq // C_TILE, N_HEADS, V_ROWS, C_TILE), lambda b, i: (b, 0, 0, 0))],
        out_specs=pl.BlockSpec((tq, 256), lambda b, i: (b * nq + i, 0)),
        out_shape=jax.ShapeDtypeStruct(dst.shape, dst.dtype), input_output_aliases={0: 0},
        scratch_shapes=[pltpu.VMEM((N_HEADS, V_ROWS, tq), F32), pltpu.VMEM((N_HEADS, 1, tq), F32),
                        pltpu.VMEM((2, N_HEADS, C_TILE, tq), F32)],
        compiler_params=_cparams(("parallel", "arbitrary")), name="mla_prompt",
    )(dst, qt, kc, vt)


def _mla_sample_kernel(dst_ref, q_ref, ckv_ref, kr_ref, knew_ref, wuv_ref, o_ref):
    seq = q_ref.shape[1]
    qs = q_ref[...].reshape(N_HEADS * seq, 2 * LANE)
    kcache = jnp.concatenate([ckv_ref[...]] + [kr_ref[...]] * N_HEADS, axis=-1).astype(BF16)
    kall = jnp.concatenate([kcache, knew_ref[...]], axis=0)
    s = _dot_nt(qs, kall)
    m = jnp.max(s, axis=-1, keepdims=True)
    p = jnp.exp2(s - m)
    l = jnp.sum(p, axis=-1, keepdims=True)
    o = (_dot(p.astype(BF16), kall[:, :C_KV_LORA]) / l).astype(BF16)
    for h in range(N_HEADS):
        o_ref[:, h * HD:(h + 1) * HD] = _dot(o[h * seq:(h + 1) * seq], wuv_ref[h]).astype(o_ref.dtype)


def _mla_sample(dst, qabs, kc, cache_ckv, cache_kr, wuv, layer, row0, n_seq, seq):
    blk0 = row0 // seq
    past = cache_ckv.shape[2]
    return pl.pallas_call(
        _mla_sample_kernel, grid=(n_seq,),
        in_specs=[_DST_SPEC, pl.BlockSpec((N_HEADS, seq, 256), lambda b: (0, b, 0)),
                  pl.BlockSpec((None, None, past, C_KV_LORA), lambda b: (layer, b, 0, 0)),
                  pl.BlockSpec((None, None, past, C_ROPE), lambda b: (layer, b, 0, 0)),
                  pl.BlockSpec((seq, 256), lambda b: (blk0 + b, 0)),
                  _full_spec((N_HEADS, 128, HD))],
        out_specs=pl.BlockSpec((seq, 256), lambda b: (blk0 + b, 0)),
        out_shape=jax.ShapeDtypeStruct(dst.shape, dst.dtype), input_output_aliases={0: 0},
        compiler_params=_cparams(("parallel",)), name="mla_sample",
    )(dst, qabs, cache_ckv, cache_kr, kc, wuv)


def _outmlp_kernel(xp_ref, xs_ref, ya_ref, yb_ref, yc_ref, yd_ref, wout_ref, g2_ref, w1_ref, w2_ref, fg_ref,
                   op_ref, os_ref, *, final, n_prompt_tiles):
    i = pl.program_id(0)
    y = jnp.concatenate([ya_ref[...], yb_ref[...], yc_ref[...], yd_ref[...]], axis=-1)
    x1 = jnp.where(i < n_prompt_tiles, xp_ref[...], xs_ref[...]) + _dot(y, wout_ref[...])
    hm = _rms(x1, g2_ref[...]).astype(BF16)
    acc = x1
    for c in range(D_FF // D_MODEL):
        sl = slice(c * D_MODEL, (c + 1) * D_MODEL)
        hc = jnp.square(jnp.maximum(_dot(hm, w1_ref[:, sl]), 0.0)).astype(BF16)
        acc = acc + _dot(hc, w2_ref[sl, :])
    if final:
        acc = _rms(acc, fg_ref[...])

    @pl.when(i < n_prompt_tiles)
    def _():
        op_ref[...] = acc

    @pl.when(i >= n_prompt_tiles)
    def _():
        os_ref[...] = acc


def _outmlp(xp, xs, ya, yb, yc, yd, wout, g2, w1, w2, fg, layer, final, in_place):
    t = xp.shape[0] + xs.shape[0]
    tm = ROW_TILE
    npt = xp.shape[0] // tm
    row = lambda n: pl.BlockSpec((tm, n), lambda i: (i, 0))
    xspecs = _split_rows(npt, D_MODEL)
    return pl.pallas_call(
        functools.partial(_outmlp_kernel, final=final, n_prompt_tiles=npt), grid=(t // tm,),
        in_specs=[*xspecs, row(256), row(256), row(256), row(256), _layer_spec((D_MODEL, D_MODEL), layer),
                  _full_spec((1, D_MODEL)), _layer_spec((D_MODEL, D_FF), layer), _layer_spec((D_FF, D_MODEL), layer),
                  _full_spec((1, D_MODEL))],
        out_specs=xspecs,
        out_shape=(jax.ShapeDtypeStruct(xp.shape, F32), jax.ShapeDtypeStruct(xs.shape, F32)),
        input_output_aliases={0: 0, 1: 1} if in_place else {},
        compiler_params=_cparams(("arbitrary",)), name="outmlp",
    )(xp, xs, ya, yb, yc, yd, wout, g2, w1, w2, fg)


def _rot_cols(w):
    half = C_ROPE // 2
    return jnp.concatenate([-w[..., half:], w[..., :half]], axis=-1)


def _pad_lanes(v, n=LANE, at=0):
    out = jnp.zeros(v.shape[:-1] + (n,), v.dtype)
    return out.at[..., at:at + v.shape[-1]].set(v)


def _prep_w_in(w):
    sizes = (256, 256, 256, 256, 512, 4, 256, 128, 32, 256, 256, 256, 4, 4, 256)
    cuts = [0]
    for s in sizes:
        cuts.append(cuts[-1] + s)
    (a_q, a_k, a_v, b_z, b_xbc, b_dt, c_q, c_kv, c_kr, m_q, m_k, m_v, m_i, m_f, m_o) = (
        w[..., cuts[n]:cuts[n + 1]] for n in range(len(sizes)))
    kr4 = jnp.concatenate([c_kr] * N_HEADS, axis=-1)
    krr4 = jnp.concatenate([_rot_cols(c_kr)] * N_HEADS, axis=-1)
    gates = _pad_lanes(jnp.concatenate([b_dt, m_i, m_f], axis=-1))
    out = jnp.concatenate([a_q, a_k, a_v, b_z, b_xbc, c_q, c_kv, kr4, krr4, gates, m_q, m_k, m_v, m_o], axis=-1)
    assert out.shape[-1] == N_PROJ
    return out.astype(BF16)


def _prep_w_uq(w):
    d = w.shape[0]
    w4 = w.reshape(d, 256, N_HEADS, C_NOPE + C_ROPE)
    nope = w4[..., :C_NOPE].reshape(d, 256, N_HEADS * C_NOPE)
    rope = w4[..., C_NOPE:]
    return jnp.concatenate([nope, rope.reshape(d, 256, N_HEADS * C_ROPE),
                            _rot_cols(rope).reshape(d, 256, N_HEADS * C_ROPE)], axis=-1).astype(BF16)


def _prep_w_ukv(w):
    d = w.shape[0]
    w4 = w.reshape(d, C_KV_LORA, N_HEADS, C_NOPE + HD)
    w_uk = w4[..., :C_NOPE]
    w_uv = w4[..., C_NOPE:]
    wabs = jnp.zeros((d, N_HEADS * C_NOPE, N_HEADS * C_KV_LORA), w.dtype)
    for h in range(N_HEADS):
        wabs = wabs.at[:, h * C_NOPE:(h + 1) * C_NOPE, h * C_KV_LORA:(h + 1) * C_KV_LORA].set(
            jnp.swapaxes(w_uk[:, :, h, :], 1, 2))
    wuv = jnp.transpose(w_uv, (0, 2, 1, 3))
    wuv_all = w_uv.reshape(d, C_KV_LORA, N_HEADS * HD)
    return wabs.astype(BF16), wuv.astype(BF16), wuv_all.astype(BF16)


def _rope_tables(pos):
    half = C_ROPE // 2
    inv = jnp.exp(-math.log(ROPE_BASE) * jnp.arange(half, dtype=F32) / half)
    ang = pos.astype(F32)[:, None] * inv[None, :]
    reps = LANE // half
    return jnp.tile(jnp.cos(ang), (1, reps)), jnp.tile(jnp.sin(ang), (1, reps))


def _band_bias(table, q_rows):
    cols = q_rows + A_SPAN
    k = np.arange(q_rows + cols - 1)
    u = table[:, np.clip((q_rows - 1 - k) + A_SPAN, -A_REL_CLIP, A_REL_CLIP) + A_REL_CLIP].astype(F32)
    period = q_rows + cols
    u = jnp.pad(u, ((0, 0), (0, 1)))
    flat = jnp.tile(u, (1, q_rows))[:, q_rows - 1:q_rows - 1 + q_rows * (period - 1)]
    toep = flat.reshape(table.shape[0], q_rows, period - 1)[:, :, :cols]
    r = np.arange(q_rows)[:, None]
    s = np.arange(cols)[None, :] - (r // CHUNK) * CHUNK
    inband = (s >= 0) & (s < A_SPAN + CHUNK)
    return jnp.where(jnp.asarray(inband)[None], toep, NEG)


def kernel(x_prompt, x_sample, cache_attn_k, cache_attn_v, state_ssm_conv, state_ssm, cache_mla_ckv, cache_mla_kr,
           state_mlstm_c, state_mlstm_n, state_mlstm_m, norm1_g, w_in, attn_rel_bias, ssm_conv_w, ssm_conv_b,
           ssm_dt_bias, ssm_a_log, ssm_d, ssm_norm_g, mla_q_norm_g, mla_w_uq, mla_kv_norm_g, mla_w_ukv,
           mlstm_b_i, mlstm_b_f, mlstm_norm_g, w_out, norm2_g, mlp_w1, mlp_w2, final_g):
    batch, seq, _ = x_prompt.shape
    dbatch, dseq, _ = x_sample.shape
    depth = w_in.shape[0]
    past = cache_mla_ckv.shape[2]
    tp, ts = batch * seq, dbatch * dseq
    lp = min(REC_CHUNK_PROMPT, seq)
    assert dseq == CHUNK and past % CHUNK == 0 and cache_attn_k.shape[2] == A_SPAN
    assert seq % C_QTILE == 0 and tp % ROW_TILE == 0 and ts % ROW_TILE == 0

    w_in_r = _prep_w_in(w_in)
    wuq_r = _prep_w_uq(mla_w_uq)
    wabs, wuv, wuv_all = _prep_w_ukv(mla_w_ukv)
    w_out_b, w1_b, w2_b = w_out.astype(BF16), mlp_w1.astype(BF16), mlp_w2.astype(BF16)
    pos = jnp.concatenate([jnp.tile(jnp.arange(seq), batch), jnp.tile(past + jnp.arange(dseq), dbatch)])
    cos, sin = _rope_tables(pos)
    dtb = _pad_lanes(ssm_dt_bias, at=G_DT)[:, None, :]
    alog = _pad_lanes(ssm_a_log, at=G_DT)[:, None, :]
    dsk = _pad_lanes(ssm_d)[:, None, :]
    gate_bias = (_pad_lanes(mlstm_b_i, at=G_I) + _pad_lanes(mlstm_b_f, at=G_F))[:, None, :]
    gate_bias_t = jnp.swapaxes(gate_bias[:, :, :GT_ROWS], 1, 2)
    cache_k = cache_attn_k.reshape(depth, dbatch, A_SPAN, 256)
    cache_v = cache_attn_v.reshape(depth, dbatch, A_SPAN, 256)
    conv0_s = jnp.pad(state_ssm_conv, ((0, 0), (0, 0), (8 - (B_CONV - 1), 0), (0, 0)))
    caug0_s = jnp.concatenate([state_mlstm_c, state_mlstm_n[..., None],
                               jnp.zeros(state_mlstm_c.shape[:-1] + (LANE - HD - 1,), F32)], axis=-1)
    m0_s = _pad_lanes(state_mlstm_m)[:, :, None, :]
    conv0_p = jnp.zeros((batch, 8, B_CONV_DIM), F32)
    h0_p = jnp.zeros((batch, N_HEADS, HD, B_STATE), F32)

    xp, xs = x_prompt.reshape(tp, D_MODEL), x_sample.reshape(ts, D_MODEL)
    outs = [[] for _ in range(18)]
    ya, yb, yc, yd = (jnp.zeros((tp + ts, GROUP_W), BF16) for _ in range(4))
    fg = final_g[None, :]
    for l in range(depth):
        (aq, akv, bz, bxbc, qabs, qt, ckv, kr, kc, vt, mqkv, mo, gates, mkt, mvt, gt, aqt, akh, avt) = _inproj(
            xp, xs, norm1_g[l][None], w_in_r, mla_q_norm_g[l][None], mla_kv_norm_g[l][None], wuq_r, wabs,
            wuv_all, cos, sin, l)

        rel = attn_rel_bias[l] * math.log2(math.e)
        ya = _attn_a_prompt(ya, aqt, akh, avt, jnp.swapaxes(_band_bias(rel, A_QBLOCK), 1, 2), batch, seq)
        ya = _attn_a_sample(ya, aq, akv, cache_k, cache_v, _band_bias(rel, CHUNK), l, tp, dbatch, dseq)

        ssd_w = (ssm_conv_w[l], ssm_conv_b[l][None], dtb[l], alog[l], dsk[l], ssm_norm_g[l][None])
        yb, h_p = _ssd(yb, bz, bxbc, gates, conv0_p, h0_p, *ssd_w, 0, batch, seq, lp)
        yb, h_s = _ssd(yb, bz, bxbc, gates, conv0_s[l], state_ssm[l], *ssd_w, tp, dbatch, dseq, dseq)

        yc = _mla_prompt(yc, qt, kc, vt, batch, seq)
        yc = _mla_sample(yc, qabs, kc, cache_mla_ckv, cache_mla_kr, wuv[l], l, tp, dbatch, dseq)

        ml_w = (gate_bias[l], mlstm_norm_g[l][None])
        yd, ct_p, m_p = _mlstm_prompt(yd, mqkv, mkt, mvt, mo, gates, gt, gate_bias[l], gate_bias_t[l],
                                      mlstm_norm_g[l][None], batch, seq, lp)
        yd, c_s, m_s = _mlstm(yd, mqkv, mo, gates, caug0_s[l], m0_s[l], *ml_w, tp, dbatch, dseq, dseq)

        xp, xs = _outmlp(xp, xs, ya, yb, yc, yd, w_out_b, norm2_g[l][None], w1_b, w2_b, fg, l, l == depth - 1, l > 0)

        keep = min(A_SPAN, seq)
        tail = lambda a, n: jnp.stack([a[(b + 1) * seq - n:(b + 1) * seq] for b in range(batch)])
        akv_p = tail(akv, keep).reshape(batch, keep, 2, N_HEADS, HD)
        akv_s = akv[tp:].reshape(dbatch, dseq, 2, N_HEADS, HD)
        new = (akv_p[:, :, 0], akv_p[:, :, 1], tail(bxbc, B_CONV - 1), h_p,
               ckv[:tp].reshape(batch, seq, C_KV_LORA), kr[:tp].reshape(batch, seq, C_ROPE),
               jnp.swapaxes(ct_p[:, :, :HD, :], 2, 3), ct_p[:, :, HD, :], m_p[:, 0, :N_HEADS],
               jnp.concatenate([cache_attn_k[l][:, dseq:], akv_s[:, :, 0]], axis=1),
               jnp.concatenate([cache_attn_v[l][:, dseq:], akv_s[:, :, 1]], axis=1),
               bxbc[tp:].reshape(dbatch, dseq, B_CONV_DIM)[:, dseq - (B_CONV - 1):], h_s,
               ckv[tp:].reshape(dbatch, dseq, C_KV_LORA), kr[tp:].reshape(dbatch, dseq, C_ROPE),
               c_s[..., :HD], c_s[..., HD], m_s[:, 0, :N_HEADS])
        for dst, v in zip(outs, new):
            dst.append(v)

    y_prompt = xp.reshape(batch, seq, D_MODEL)
    y_sample = xs.reshape(dbatch, dseq, D_MODEL)
    return (y_prompt, y_sample) + tuple(jnp.stack(v) for v in outs)
```

```python
import functools
import math

import jax
import jax.numpy as jnp
import numpy as np
from jax import lax
from jax.experimental import pallas as pl
from jax.experimental.pallas import tpu as pltpu

F32 = jnp.float32
BF16 = jnp.bfloat16
EPS = 1e-6
NEG = -1e30
ROPE_BASE = 10000.0

CHUNK = 64
D_MODEL = 1024
GROUP_W = 256
D_FF = 4096
N_HEADS = 4
HD = 64
A_BAND_CHUNKS = 8
A_SPAN = A_BAND_CHUNKS * CHUNK
A_REL_CLIP = 128
B_GROUPS = 2
B_STATE = 64
B_CONV = 4
B_CONV_DIM = 512
C_NOPE = 64
C_ROPE = 32
C_KV_LORA = 128
A_QSCALE = HD ** -0.5 * math.log2(math.e)
C_SCALE = (C_NOPE + C_ROPE) ** -0.5
C_QSCALE = C_SCALE * math.log2(math.e)
V_ROWS = 80
LANE = 128
VMEM_LIMIT = 56 * 1024 * 1024

OFF_A, OFF_BZ, OFF_XBC, OFF_CQ, OFF_CKV, OFF_KR, OFF_KRR, OFF_G, OFF_MQKV, OFF_MO, N_PROJ = (
    0, 768, 1024, 1536, 1792, 1920, 2048, 2176, 2304, 3072, 3328)
G_DT, G_I, G_F = 0, 4, 8
GT_ROWS = 16

ROW_TILE = 512
A_QBLOCK = 256
C_TILE = 256
C_QTILE = 2 * C_TILE
C_UNROLL_PAIRS = 2
REC_CHUNK_PROMPT = 256


def _cparams(sem):
    return pltpu.CompilerParams(dimension_semantics=sem, vmem_limit_bytes=VMEM_LIMIT)


def _rms(x, g):
    return x * lax.rsqrt(jnp.mean(x * x, axis=-1, keepdims=True) + EPS) * g


def _dot(a, b):
    return jnp.dot(a, b, preferred_element_type=F32)


def _dot_nt(a, b):
    return lax.dot_general(a, b, (((1,), (1,)), ((), ())), preferred_element_type=F32)


def _dot_tn(a, b):
    return lax.dot_general(a, b, (((0,), (0,)), ((), ())), preferred_element_type=F32)


def _cumsum_rows(x, tri_bf16):
    hi = x.astype(BF16)
    r1 = x - hi.astype(F32)
    mid = r1.astype(BF16)
    lo = (r1 - mid.astype(F32)).astype(BF16)
    return _dot(tri_bf16, hi) + _dot(tri_bf16, mid) + _dot(tri_bf16, lo)


def _log1p_exp_neg_abs(x):
    return jnp.log1p(jnp.exp(-jnp.abs(x)))


def _full_spec(shape):
    n = len(shape)
    return pl.BlockSpec(shape, lambda *_: (0,) * n)


_DST_SPEC = pl.BlockSpec(memory_space=pl.ANY)


def _const_spec(shape):
    n = len(shape)
    return pl.BlockSpec(shape, lambda *_: (0,) * n, pipeline_mode=pl.Buffered(1))


def _layer_spec(shape, layer):
    n = len(shape)
    return pl.BlockSpec((None,) + tuple(shape), lambda *_: (layer,) + (0,) * n, pipeline_mode=pl.Buffered(1))


def _split_rows(n_prompt_tiles, width):
    npt = n_prompt_tiles
    return (pl.BlockSpec((ROW_TILE, width), lambda i: (jnp.minimum(i, npt - 1), 0)),
            pl.BlockSpec((ROW_TILE, width), lambda i: (jnp.maximum(i - npt, 0), 0)))


def _inproj_kernel(xp_ref, xs_ref, g1_ref, w_ref, qg_ref, kvg_ref, wuq_ref, wabs_ref, wuv_ref, cos_ref, sin_ref,
                   aq_ref, akv_ref, bz_ref, bxbc_ref, qabs_ref, qt_ref, ckv_ref, kr_ref, kc_ref, vt_ref,
                   mqkv_ref, mo_ref, gates_ref, mkt_ref, mvt_ref, gt_ref, aqt_ref, akh_ref, avt_ref, *,
                   n_prompt_tiles):
    tm = xp_ref.shape[0]
    is_prompt = pl.program_id(0) < n_prompt_tiles
    x = jnp.where(is_prompt, xp_ref[...], xs_ref[...])
    hn = _rms(x, g1_ref[...]).astype(BF16)

    def seg(a, b):
        return _dot(hn, w_ref[:, a:b])

    a = seg(OFF_A, OFF_BZ)
    aq = a[:, :GROUP_W] * A_QSCALE
    aq_ref[...] = aq.astype(BF16)
    akv_ref[...] = a[:, GROUP_W:]
    aqt_ref[...] = aq.T.astype(BF16)
    avt_ref[...] = a[:, 2 * GROUP_W:].T.astype(BF16)
    for h in range(N_HEADS):
        akh_ref[h] = a[:, GROUP_W + h * HD:GROUP_W + (h + 1) * HD].astype(BF16)
    bz_ref[...] = seg(OFF_BZ, OFF_XBC)
    bxbc_ref[...] = seg(OFF_XBC, OFF_CQ)

    cos = cos_ref[...]
    sin = sin_ref[...]
    cqn = _rms(seg(OFF_CQ, OFF_CKV), qg_ref[...]).astype(BF16)
    qf = _dot(cqn, wuq_ref[...])
    qrope = (qf[:, 256:384] * cos + qf[:, 384:512] * sin) * C_QSCALE
    qlat = _dot(qf[:, :256].astype(BF16), wabs_ref[...]) * C_QSCALE
    lane_head = lax.broadcasted_iota(jnp.int32, (tm, LANE), 1) // C_ROPE
    qhs = [jnp.concatenate([qlat[:, h * LANE:(h + 1) * LANE], jnp.where(lane_head == h, qrope, 0.0)], axis=-1)
           for h in range(N_HEADS)]

    ckv_kr = seg(OFF_CKV, OFF_KRR)
    krr_g = seg(OFF_KRR, OFF_MQKV)
    g = krr_g[:, LANE:]
    ckv = _rms(ckv_kr[:, :C_KV_LORA], kvg_ref[...])
    kr4 = ckv_kr[:, C_KV_LORA:] * cos + krr_g[:, :LANE] * sin
    ckv_ref[...] = ckv
    kr_ref[...] = kr4[:, :C_ROPE]
    kc_ref[...] = jnp.concatenate([ckv, kr4], axis=-1).astype(BF16)

    m = seg(OFF_MQKV, OFF_MO)
    mk = m[:, 256:512] * HD ** -0.5
    mqkv_ref[:, 0:256] = m[:, 0:256].astype(BF16)
    mqkv_ref[:, 256:512] = mk.astype(BF16)
    mqkv_ref[:, 512:768] = m[:, 512:768].astype(BF16)
    mo_ref[...] = seg(OFF_MO, N_PROJ)
    gates_ref[...] = g

    for h in range(N_HEADS):
        qt_ref[h] = qhs[h].T.astype(BF16)
    vall = _dot(ckv.astype(BF16), wuv_ref[...])
    ones_c = (lax.broadcasted_iota(jnp.int32, (V_ROWS - HD, C_TILE), 0) == 0).astype(BF16)
    for s in range(tm // C_TILE):
        vt = vall[s * C_TILE:(s + 1) * C_TILE, :].T
        for h in range(N_HEADS):
            vt_ref[s, h, 0:HD, :] = vt[h * HD:(h + 1) * HD].astype(BF16)
            vt_ref[s, h, HD:V_ROWS, :] = ones_c
    mkt = mk.T
    mvt = m[:, 512:768].T
    ones_v = (lax.broadcasted_iota(jnp.int32, (LANE - HD, tm), 0) == 0).astype(BF16)
    for h in range(N_HEADS):
        mkt_ref[h] = mkt[h * HD:(h + 1) * HD].astype(BF16)
        mvt_ref[h, 0:HD, :] = mvt[h * HD:(h + 1) * HD].astype(BF16)
        mvt_ref[h, HD:LANE, :] = ones_v
    gt_ref[...] = g.T[0:GT_ROWS]

    @pl.when(jnp.logical_not(is_prompt))
    def _():
        for h in range(N_HEADS):
            qabs_ref[h] = qhs[h].astype(BF16)


def _inproj(xp, xs, g1, w, qg, kvg, wuq, wabs, wuv_all, cos, sin, layer):
    tp, ts = xp.shape[0], xs.shape[0]
    t = tp + ts
    tm = ROW_TILE
    npt = tp // tm
    col = lambda *lead: pl.BlockSpec((*lead, tm), lambda i: (0,) * len(lead) + (i,))
    row = lambda n: pl.BlockSpec((tm, n), lambda i: (i, 0))
    out_shape = (
        jax.ShapeDtypeStruct((t, 256), BF16),
        jax.ShapeDtypeStruct((t, 512), F32),
        jax.ShapeDtypeStruct((t, 256), F32),
        jax.ShapeDtypeStruct((t, 512), F32),
        jax.ShapeDtypeStruct((N_HEADS, ts, 256), BF16),
        jax.ShapeDtypeStruct((N_HEADS, 256, t), BF16),
        jax.ShapeDtypeStruct((t, 128), F32),
        jax.ShapeDtypeStruct((t, C_ROPE), F32),
        jax.ShapeDtypeStruct((t, 256), BF16),
        jax.ShapeDtypeStruct((t // C_TILE, N_HEADS, V_ROWS, C_TILE), BF16),
        jax.ShapeDtypeStruct((t, 768), BF16),
        jax.ShapeDtypeStruct((t, 256), F32),
        jax.ShapeDtypeStruct((t, LANE), F32),
        jax.ShapeDtypeStruct((N_HEADS, HD, t), BF16),
        jax.ShapeDtypeStruct((N_HEADS, LANE, t), BF16),
        jax.ShapeDtypeStruct((GT_ROWS, t), F32),
        jax.ShapeDtypeStruct((GROUP_W, t), BF16),
        jax.ShapeDtypeStruct((N_HEADS, t, HD), BF16),
        jax.ShapeDtypeStruct((GROUP_W, t), BF16),
    )
    out_specs = (
        row(256), row(512), row(256), row(512),
        pl.BlockSpec((N_HEADS, tm, 256), lambda i: (0, jnp.maximum(i - npt, 0), 0)),
        col(N_HEADS, 256),
        row(128), row(C_ROPE), row(256),
        pl.BlockSpec((tm // C_TILE, N_HEADS, V_ROWS, C_TILE), lambda i: (i, 0, 0, 0)),
        row(768), row(256), row(LANE),
        col(N_HEADS, HD), col(N_HEADS, LANE), col(GT_ROWS),
        col(GROUP_W), pl.BlockSpec((N_HEADS, tm, HD), lambda i: (0, i, 0)), col(GROUP_W),
    )
    in_specs = [*_split_rows(npt, D_MODEL), _full_spec((1, D_MODEL)), _layer_spec((D_MODEL, N_PROJ), layer),
                _full_spec((1, 256)), _full_spec((1, 128)), _layer_spec((256, 512), layer),
                _layer_spec((256, 512), layer), _layer_spec((C_KV_LORA, 256), layer), row(LANE), row(LANE)]
    return pl.pallas_call(
        functools.partial(_inproj_kernel, n_prompt_tiles=npt), grid=(t // tm,), in_specs=in_specs,
        out_specs=out_specs, out_shape=out_shape,
        compiler_params=_cparams(("arbitrary",)), name="inproj",
    )(xp, xs, g1, w, qg, kvg, wuq, wabs, wuv_all, cos, sin)


def _band_heads(q, kb, vb, bias_ref, valid, o_ref):
    for h in range(N_HEADS):
        sl = slice(h * HD, (h + 1) * HD)
        s = _dot_nt(q[:, sl], kb[:, sl]) + bias_ref[h]
        if valid is not None:
            s = jnp.where(valid, s, NEG)
        m = jnp.max(s, axis=-1, keepdims=True)
        p = jnp.exp2(s - m)
        l = jnp.sum(p, axis=-1, keepdims=True)
        o = _dot(p.astype(BF16), vb[:, sl]) / l
        o_ref[:, sl] = o.astype(o_ref.dtype)


def _attn_a_prompt_kernel(dst_ref, qt_ref, k0_ref, k1_ref, k2_ref, v0_ref, v1_ref, v2_ref, bias_ref, o_ref, s_ref):
    i = pl.program_id(1)
    nk = 3 * A_QBLOCK
    vt = jnp.concatenate([v0_ref[...], v1_ref[...], v2_ref[...]], axis=1)
    ones = (lax.broadcasted_iota(jnp.int32, (V_ROWS - HD, nk), 0) == 0).astype(BF16)

    def scores(h):
        kh = jnp.concatenate([k0_ref[h], k1_ref[h], k2_ref[h]], axis=0)
        s_ref[h % 2] = _dot(kh, qt_ref[h * HD:(h + 1) * HD, :]) + bias_ref[h]

    def heads(masked):
        outs = []
        scores(0)
        for h in range(N_HEADS):
            if h + 1 < N_HEADS:
                scores(h + 1)
            s = s_ref[h % 2]
            if masked:
                krow = lax.broadcasted_iota(jnp.int32, (nk, A_QBLOCK), 0)
                s = jnp.where(krow >= (2 - i) * A_QBLOCK, s, NEG)
            m = jnp.max(s, axis=0, keepdims=True)
            p = jnp.exp2(s - m).astype(BF16)
            vaug = jnp.concatenate([vt[h * HD:(h + 1) * HD, :], ones], axis=0)
            o = _dot(vaug, p)
            outs.append(o[:HD] / o[HD:HD + 1])
        o_ref[...] = jnp.concatenate(outs, axis=0).T.astype(o_ref.dtype)

    @pl.when(i < 2)
    def _():
        heads(True)

    @pl.when(i >= 2)
    def _():
        heads(False)


def _attn_a_prompt(dst, aqt, akh, avt, biasm_t, batch, seq):
    nb = seq // A_QBLOCK
    qb = A_QBLOCK
    back = lambda b, i, n: b * nb + jnp.maximum(i - n, 0)
    kspec = lambda n: pl.BlockSpec((N_HEADS, qb, HD), lambda b, i: (0, back(b, i, n), 0))
    vspec = lambda n: pl.BlockSpec((GROUP_W, qb), lambda b, i: (0, back(b, i, n)))
    return pl.pallas_call(
        _attn_a_prompt_kernel, grid=(batch, nb),
        in_specs=[_DST_SPEC, pl.BlockSpec((GROUP_W, qb), lambda b, i: (0, b * nb + i)),
                  kspec(2), kspec(1), kspec(0), vspec(2), vspec(1), vspec(0), _full_spec(biasm_t.shape)],
        out_specs=pl.BlockSpec((qb, 256), lambda b, i: (b * nb + i, 0)),
        out_shape=jax.ShapeDtypeStruct(dst.shape, dst.dtype), input_output_aliases={0: 0},
        scratch_shapes=[pltpu.VMEM((2, 3 * qb, qb), F32)],
        compiler_params=_cparams(("parallel", "parallel")), name="attn_a_prompt",
    )(dst, aqt, akh, akh, akh, avt, avt, avt, biasm_t)


def _attn_a_sample_kernel(dst_ref, q_ref, ck_ref, cv_ref, kvn_ref, bias_ref, o_ref):
    kvn = kvn_ref[...]
    kb = jnp.concatenate([ck_ref[...], kvn[:, :GROUP_W]], axis=0).astype(BF16)
    vb = jnp.concatenate([cv_ref[...], kvn[:, GROUP_W:]], axis=0).astype(BF16)
    _band_heads(q_ref[...], kb, vb, bias_ref, None, o_ref)


def _attn_a_sample(dst, aq, akv, cache_k, cache_v, bias, layer, row0, n_seq, seq):
    blk0 = row0 // seq
    cspec = pl.BlockSpec((None, None, A_SPAN, 256), lambda b: (layer, b, 0, 0))
    return pl.pallas_call(
        _attn_a_sample_kernel, grid=(n_seq,),
        in_specs=[_DST_SPEC, pl.BlockSpec((seq, 256), lambda b: (blk0 + b, 0)), cspec, cspec,
                  pl.BlockSpec((seq, 512), lambda b: (blk0 + b, 0)), _full_spec(bias.shape)],
        out_specs=pl.BlockSpec((seq, 256), lambda b: (blk0 + b, 0)),
        out_shape=jax.ShapeDtypeStruct(dst.shape, dst.dtype), input_output_aliases={0: 0},
        compiler_params=_cparams(("parallel",)), name="attn_a_sample",
    )(dst, aq, cache_k, cache_v, akv, bias)


def _ssd_kernel(dst_ref, z_ref, xbc_ref, g_ref, conv0_ref, h0_ref, cw_ref, cb_ref, dtb_ref, alog_ref, dsk_ref, ng_ref,
                y_ref, hout_ref, xpad_ref, h_ref):
    L = z_ref.shape[0]
    c = pl.program_id(1)

    @pl.when(c == 0)
    def _():
        xpad_ref[...] = conv0_ref[...]
        h_ref[...] = h0_ref[...]

    xbc = xbc_ref[...]
    xfull = jnp.concatenate([xpad_ref[...], xbc], axis=0)
    conv = cb_ref[...] + xbc * cw_ref[B_CONV - 1:B_CONV, :]
    for j in range(B_CONV - 1):
        conv = conv + pltpu.roll(xfull, B_CONV - 1 - j, axis=0)[8:8 + L, :] * cw_ref[j:j + 1, :]
    xpad_ref[...] = xbc[L - 8:L, :]
    u = conv * jax.nn.sigmoid(conv)

    g = g_ref[...] + dtb_ref[...]
    dt = jnp.maximum(g, 0.0) + _log1p_exp_neg_abs(g)
    da = dt * (-jnp.exp(alog_ref[...]))
    row = lax.broadcasted_iota(jnp.int32, (L, L), 0)
    colm = lax.broadcasted_iota(jnp.int32, (L, L), 1)
    tri = row >= colm
    cs = _cumsum_rows(da, tri.astype(BF16))
    cs_t = cs.T

    ys = []
    gmat = [None] * B_GROUPS
    for h in range(N_HEADS):
        grp = h // (N_HEADS // B_GROUPS)
        xs = u[:, h * HD:(h + 1) * HD]
        bm = u[:, 256 + grp * B_STATE:256 + (grp + 1) * B_STATE].astype(BF16)
        cm = u[:, 384 + grp * B_STATE:384 + (grp + 1) * B_STATE].astype(BF16)
        if gmat[grp] is None:
            gmat[grp] = _dot_nt(cm, bm)
        csc = cs[:, G_DT + h:G_DT + h + 1]
        csr = cs_t[G_DT + h:G_DT + h + 1, :]
        tot = cs[L - 1:L, G_DT + h:G_DT + h + 1]
        dec = jnp.exp(jnp.where(tri, csc - csr, NEG))
        xdt = xs * dt[:, G_DT + h:G_DT + h + 1]
        hprev = h_ref[h]
        y = _dot((gmat[grp] * dec).astype(BF16), xdt.astype(BF16))
        y = y + _dot_nt(cm, hprev.astype(BF16)) * jnp.exp(csc)
        y = y + dsk_ref[0:1, h:h + 1] * xs
        ys.append(y)
        wend = (xdt * jnp.exp(tot - csc)).astype(BF16)
        h_ref[h] = jnp.exp(tot) * hprev + _dot_tn(wend, bm)
    y = jnp.concatenate(ys, axis=-1)
    z = z_ref[...]
    y = y * (z * jax.nn.sigmoid(z))
    y_ref[...] = _rms(y, ng_ref[...]).astype(y_ref.dtype)

    @pl.when(c == pl.num_programs(1) - 1)
    def _():
        hout_ref[...] = h_ref[...]


def _ssd(dst, bz, bxbc, gates, conv0, h0, cw, cb, dtb, alog, dsk, ng, row0, n_seq, seq, L):
    nc = seq // L
    blk0 = row0 // L
    rspec = lambda n: pl.BlockSpec((L, n), lambda b, c: (blk0 + b * nc + c, 0))
    return pl.pallas_call(
        _ssd_kernel, grid=(n_seq, nc),
        in_specs=[_DST_SPEC, rspec(256), rspec(512), rspec(LANE),
                  pl.BlockSpec((None, 8, 512), lambda b, c: (b, 0, 0)),
                  pl.BlockSpec((None, N_HEADS, HD, B_STATE), lambda b, c: (b, 0, 0, 0)),
                  _full_spec((B_CONV, 512)), _full_spec((1, 512)), _full_spec((1, LANE)), _full_spec((1, LANE)),
                  _full_spec((1, LANE)), _full_spec((1, 256))],
        out_specs=(rspec(256), pl.BlockSpec((None, N_HEADS, HD, B_STATE), lambda b, c: (b, 0, 0, 0))),
        input_output_aliases={0: 0},
        out_shape=(jax.ShapeDtypeStruct(dst.shape, dst.dtype),
                   jax.ShapeDtypeStruct((n_seq, N_HEADS, HD, B_STATE), F32)),
        scratch_shapes=[pltpu.VMEM((8, 512), F32), pltpu.VMEM((N_HEADS, HD, B_STATE), F32)],
        compiler_params=_cparams(("parallel", "arbitrary")), name="ssd",
    )(dst, bz, bxbc, gates, conv0, h0, cw, cb, dtb, alog, dsk, ng)


def _ssd_prompt_kernel(dst_ref, z_ref, xbc_ref, g_ref, gt_ref, cw_ref, cb_ref, dtb_ref, dtbt_ref, alog_ref, alogt_ref,
                       dsk_ref, ng_ref, y_ref, hout_ref, xpad_ref, h_ref):
    L = z_ref.shape[0]
    c = pl.program_id(1)

    @pl.when(c == 0)
    def _():
        xpad_ref[...] = jnp.zeros_like(xpad_ref)
        h_ref[...] = jnp.zeros_like(h_ref)

    xbc = xbc_ref[...]
    xfull = jnp.concatenate([xpad_ref[...], xbc], axis=0)
    conv = cb_ref[...] + xbc * cw_ref[B_CONV - 1:B_CONV, :]
    for j in range(B_CONV - 1):
        conv = conv + pltpu.roll(xfull, B_CONV - 1 - j, axis=0)[8:8 + L, :] * cw_ref[j:j + 1, :]
    xpad_ref[...] = xbc[L - 8:L, :]
    u = conv * jax.nn.sigmoid(conv)

    g = g_ref[...] + dtb_ref[...]
    gt = gt_ref[...] + dtbt_ref[...]
    dt = jnp.maximum(g, 0.0) + _log1p_exp_neg_abs(g)
    dtt = jnp.maximum(gt, 0.0) + _log1p_exp_neg_abs(gt)
    row = lax.broadcasted_iota(jnp.int32, (L, L), 0)
    colm = lax.broadcasted_iota(jnp.int32, (L, L), 1)
    s_le_l = row <= colm
    cs = _cumsum_rows(dt * (-jnp.exp(alog_ref[...])), (row >= colm).astype(BF16))
    cs_t = _cumsum_lanes(dtt * (-jnp.exp(alogt_ref[...])), s_le_l.astype(BF16))

    xst = u[:, :GROUP_W].T
    yts = []
    gmat_t = [None] * B_GROUPS
    for h in range(N_HEADS):
        grp = h // (N_HEADS // B_GROUPS)
        bm = u[:, 256 + grp * B_STATE:256 + (grp + 1) * B_STATE].astype(BF16)
        cm = u[:, 384 + grp * B_STATE:384 + (grp + 1) * B_STATE].astype(BF16)
        if gmat_t[grp] is None:
            gmat_t[grp] = _dot_nt(bm, cm)
        csc = cs[:, G_DT + h:G_DT + h + 1]
        csr = cs_t[G_DT + h:G_DT + h + 1, :]
        tot = cs_t[G_DT + h:G_DT + h + 1, L - 1:L]
        xs_t = xst[h * HD:(h + 1) * HD]
        xdt_t = xs_t * dtt[G_DT + h:G_DT + h + 1, :]
        dec_t = jnp.exp(jnp.where(s_le_l, csr - csc, NEG))
        hprev = h_ref[h]
        y = _dot(xdt_t.astype(BF16), (gmat_t[grp] * dec_t).astype(BF16))
        y = y + _dot_nt(hprev.astype(BF16), cm) * jnp.exp(csr)
        yts.append(y + dsk_ref[0:1, h:h + 1] * xs_t)
        wend = (xdt_t * jnp.exp(tot - csr)).astype(BF16)
        h_ref[h] = jnp.exp(tot) * hprev + _dot(wend, bm)
    y = jnp.concatenate(yts, axis=0).T
    z = z_ref[...]
    y = y * (z * jax.nn.sigmoid(z))
    y_ref[...] = _rms(y, ng_ref[...]).astype(y_ref.dtype)

    @pl.when(c == pl.num_programs(1) - 1)
    def _():
        hout_ref[...] = h_ref[...]


def _ssd_prompt(dst, bz, bxbc, gates, gt, cw, cb, dtb, dtbt, alog, alogt, dsk, ng, n_seq, seq, L):
    nc = seq // L
    rspec = lambda n: pl.BlockSpec((L, n), lambda b, c: (b * nc + c, 0))
    return pl.pallas_call(
        _ssd_prompt_kernel, grid=(n_seq, nc),
        in_specs=[_DST_SPEC, rspec(256), rspec(512), rspec(LANE),
                  pl.BlockSpec((GT_ROWS, L), lambda b, c: (0, b * nc + c)),
                  _full_spec((B_CONV, 512)), _full_spec((1, 512)), _full_spec((1, LANE)), _full_spec((GT_ROWS, 1)),
                  _full_spec((1, LANE)), _full_spec((GT_ROWS, 1)), _full_spec((1, LANE)), _full_spec((1, 256))],
        out_specs=(rspec(256), pl.BlockSpec((None, N_HEADS, HD, B_STATE), lambda b, c: (b, 0, 0, 0))),
        input_output_aliases={0: 0},
        out_shape=(jax.ShapeDtypeStruct(dst.shape, dst.dtype),
                   jax.ShapeDtypeStruct((n_seq, N_HEADS, HD, B_STATE), F32)),
        scratch_shapes=[pltpu.VMEM((8, 512), F32), pltpu.VMEM((N_HEADS, HD, B_STATE), F32)],
        compiler_params=_cparams(("parallel", "arbitrary")), name="ssd_prompt",
    )(dst, bz, bxbc, gates, gt, cw, cb, dtb, dtbt, alog, alogt, dsk, ng)


def _mlstm_kernel(dst_ref, qkv_ref, o_ref, g_ref, c0_ref, m0_ref, bias_ref, ng_ref,
                  y_ref, cout_ref, mout_ref, c_ref, m_ref):
    L = qkv_ref.shape[0]
    c = pl.program_id(1)

    @pl.when(c == 0)
    def _():
        c_ref[...] = c0_ref[...]
        m_ref[...] = m0_ref[...]

    g = g_ref[...] + bias_ref[...]
    lf = jnp.minimum(g, 0.0) - _log1p_exp_neg_abs(g)
    row = lax.broadcasted_iota(jnp.int32, (L, L), 0)
    colm = lax.broadcasted_iota(jnp.int32, (L, L), 1)
    tri = row >= colm
    bcum = _cumsum_rows(lf, tri.astype(BF16))
    bcum_t = bcum.T
    g_t = g.T
    unit = (lax.broadcasted_iota(jnp.int32, (L, HD), 1) == 0).astype(BF16)
    ng = ng_ref[...]
    og = o_ref[...]

    for h in range(N_HEADS):
        sl = slice(h * HD, (h + 1) * HD)
        q = qkv_ref[:, h * HD:(h + 1) * HD]
        k = qkv_ref[:, 256 + h * HD:256 + (h + 1) * HD]
        v = qkv_ref[:, 512 + h * HD:512 + (h + 1) * HD]
        vaug = jnp.concatenate([v, unit], axis=-1)
        bc = bcum[:, G_F + h:G_F + h + 1]
        br = bcum_t[G_F + h:G_F + h + 1, :]
        ic = g[:, G_I + h:G_I + h + 1]
        ir = g_t[G_I + h:G_I + h + 1, :]
        tot = bcum[L - 1:L, G_F + h:G_F + h + 1]
        mprev = m_ref[0:1, h:h + 1]

        dm = jnp.where(tri, bc - br + ir, NEG)
        inter = bc + mprev
        mt = jnp.maximum(inter, jnp.max(dm, axis=-1, keepdims=True))
        w_intra = jnp.exp(dm - mt)
        w_inter = jnp.exp(inter - mt)
        caug = c_ref[h]
        s = _dot_nt(q, k) * w_intra
        r = w_inter * _dot(q, caug.astype(BF16)) + _dot(s.astype(BF16), vaug)
        num = r[:, :HD]
        den = r[:, HD:HD + 1]
        hout = num / jnp.maximum(jnp.abs(den), jnp.exp(-mt))

        gend = tot - bc + ic
        mnew = jnp.maximum(tot + mprev, jnp.max(gend, axis=0, keepdims=True))
        kw = (k.astype(F32) * jnp.exp(gend - mnew)).astype(BF16)
        c_ref[h] = jnp.exp(tot + mprev - mnew) * caug + _dot_tn(kw, vaug)
        m_ref[0:1, h:h + 1] = mnew

        yh = _rms(hout, ng[:, sl]) * jax.nn.sigmoid(og[:, sl])
        y_ref[:, sl] = yh.astype(y_ref.dtype)

    @pl.when(c == pl.num_programs(1) - 1)
    def _():
        cout_ref[...] = c_ref[...]
        mout_ref[...] = m_ref[...]


def _cumsum_lanes(x, tri_bf16):
    hi = x.astype(BF16)
    r1 = x - hi.astype(F32)
    mid = r1.astype(BF16)
    lo = (r1 - mid.astype(F32)).astype(BF16)
    return _dot(hi, tri_bf16) + _dot(mid, tri_bf16) + _dot(lo, tri_bf16)


def _mlstm_prompt_kernel(dst_ref, qkv_ref, kt_ref, vt_ref, o_ref, g_ref, gt_ref, bias_ref, biast_ref, ng_ref,
                         y_ref, cout_ref, mout_ref, c_ref, m_ref):
    L = qkv_ref.shape[0]
    c = pl.program_id(1)

    @pl.when(c == 0)
    def _():
        c_ref[...] = jnp.zeros_like(c_ref)
        m_ref[...] = jnp.zeros_like(m_ref)

    g = g_ref[...] + bias_ref[...]
    gt = gt_ref[...] + biast_ref[...]
    lf = jnp.minimum(g, 0.0) - _log1p_exp_neg_abs(g)
    lft = jnp.minimum(gt, 0.0) - _log1p_exp_neg_abs(gt)
    row = lax.broadcasted_iota(jnp.int32, (L, L), 0)
    colm = lax.broadcasted_iota(jnp.int32, (L, L), 1)
    s_le_l = row <= colm
    bcum = _cumsum_rows(lf, (row >= colm).astype(BF16))
    bcum_t = _cumsum_lanes(lft, s_le_l.astype(BF16))

    yts = []
    for h in range(N_HEADS):
        q = qkv_ref[:, h * HD:(h + 1) * HD]
        k = qkv_ref[:, 256 + h * HD:256 + (h + 1) * HD]
        src = g[:, G_I + h:G_I + h + 1] - bcum[:, G_F + h:G_F + h + 1]
        bc = bcum_t[G_F + h:G_F + h + 1, :]
        ir = gt[G_I + h:G_I + h + 1, :]
        tot = bcum_t[G_F + h:G_F + h + 1, L - 1:L]
        mprev = m_ref[0:1, h:h + 1]

        dmt = jnp.where(s_le_l, bc + src, NEG)
        inter = bc + mprev
        mt = jnp.maximum(inter, jnp.max(dmt, axis=0, keepdims=True))
        w_intra = jnp.exp(dmt - mt)
        w_inter = jnp.exp(inter - mt)
        ct = c_ref[h]
        st = _dot_nt(k, q) * w_intra
        rt = w_inter * _dot_nt(ct.astype(BF16), q) + _dot(vt_ref[h], st.astype(BF16))
        hout = rt[:HD] / jnp.maximum(jnp.abs(rt[HD:HD + 1]), jnp.exp(-mt))
        yts.append(hout * lax.rsqrt(jnp.mean(hout * hout, axis=0, keepdims=True) + EPS))

        gend = tot - bc + ir
        mnew = jnp.maximum(tot + mprev, jnp.max(gend, axis=-1, keepdims=True))
        kw = (kt_ref[h].astype(F32) * jnp.exp(gend - mnew)).astype(BF16)
        c_ref[h] = jnp.exp(tot + mprev - mnew) * ct + _dot_nt(vt_ref[h], kw)
        m_ref[0:1, h:h + 1] = mnew

    y = jnp.concatenate(yts, axis=0).T
    y_ref[...] = (y * ng_ref[...] * jax.nn.sigmoid(o_ref[...])).astype(y_ref.dtype)

    @pl.when(c == pl.num_programs(1) - 1)
    def _():
        cout_ref[...] = c_ref[...]
        mout_ref[...] = m_ref[...]


def _mlstm_prompt(dst, mqkv, mkt, mvt, mo, gates, gt, bias, biast, ng, n_seq, seq, L):
    nc = seq // L
    rspec = lambda n: pl.BlockSpec((L, n), lambda b, c: (b * nc + c, 0))
    cspec = lambda *lead: pl.BlockSpec((*lead, L), lambda b, c: (0,) * len(lead) + (b * nc + c,))
    stspec = pl.BlockSpec((None, N_HEADS, LANE, HD), lambda b, c: (b, 0, 0, 0))
    mspec = pl.BlockSpec((None, 1, LANE), lambda b, c: (b, 0, 0))
    return pl.pallas_call(
        _mlstm_prompt_kernel, grid=(n_seq, nc),
        in_specs=[_DST_SPEC, rspec(768), cspec(N_HEADS, HD), cspec(N_HEADS, LANE), rspec(256), rspec(LANE),
                  cspec(GT_ROWS), _full_spec((1, LANE)), _full_spec((GT_ROWS, 1)), _full_spec((1, 256))],
        out_specs=(rspec(256), stspec, mspec), input_output_aliases={0: 0},
        out_shape=(jax.ShapeDtypeStruct(dst.shape, dst.dtype),
                   jax.ShapeDtypeStruct((n_seq, N_HEADS, LANE, HD), F32),
                   jax.ShapeDtypeStruct((n_seq, 1, LANE), F32)),
        scratch_shapes=[pltpu.VMEM((N_HEADS, LANE, HD), F32), pltpu.VMEM((1, LANE), F32)],
        compiler_params=_cparams(("parallel", "arbitrary")), name="mlstm_prompt",
    )(dst, mqkv, mkt, mvt, mo, gates, gt, bias, biast, ng)


def _mlstm(dst, mqkv, mo, gates, c0, m0, bias, ng, row0, n_seq, seq, L):
    nc = seq // L
    blk0 = row0 // L
    rspec = lambda n: pl.BlockSpec((L, n), lambda b, c: (blk0 + b * nc + c, 0))
    cspec = pl.BlockSpec((None, N_HEADS, HD, LANE), lambda b, c: (b, 0, 0, 0))
    mspec = pl.BlockSpec((None, 1, LANE), lambda b, c: (b, 0, 0))
    return pl.pallas_call(
        _mlstm_kernel, grid=(n_seq, nc),
        in_specs=[_DST_SPEC, rspec(768), rspec(256), rspec(LANE), cspec, mspec, _full_spec((1, LANE)),
                  _full_spec((1, 256))],
        out_specs=(rspec(256), cspec, mspec), input_output_aliases={0: 0},
        out_shape=(jax.ShapeDtypeStruct(dst.shape, dst.dtype),
                   jax.ShapeDtypeStruct((n_seq, N_HEADS, HD, LANE), F32),
                   jax.ShapeDtypeStruct((n_seq, 1, LANE), F32)),
        scratch_shapes=[pltpu.VMEM((N_HEADS, HD, LANE), F32), pltpu.VMEM((1, LANE), F32)],
        compiler_params=_cparams(("parallel", "arbitrary")), name="mlstm",
    )(dst, mqkv, mo, gates, c0, m0, bias, ng)


def _mla_prompt_kernel(dst_ref, q_ref, kc_ref, vt_ref, o_ref, acc_ref, m_ref, s_ref):
    tq = q_ref.shape[2]
    tk = C_TILE
    i = pl.program_id(1)
    m_ref[...] = jnp.full_like(m_ref, NEG)
    acc_ref[...] = jnp.zeros_like(acc_ref)

    def scores(j, slot):
        start = pl.multiple_of(j * tk, tk)
        kt = kc_ref[pl.ds(start, tk), :]
        for h in range(N_HEADS):
            s_ref[slot, h] = _dot(kt, q_ref[h])

    def consume(j, slot, diag_offset=None):
        for h in range(N_HEADS):
            s = s_ref[slot, h]
            if diag_offset is not None:
                krow = (diag_offset + lax.broadcasted_iota(jnp.int32, (tk, tq), 0)) // CHUNK
                qcol = lax.broadcasted_iota(jnp.int32, (tk, tq), 1) // CHUNK
                s = jnp.where(krow <= qcol, s, NEG)
            m_old = m_ref[h]
            m_new = jnp.maximum(m_old, jnp.max(s, axis=0, keepdims=True))
            p = jnp.exp2(s - m_new).astype(BF16)
            acc_ref[h] = jnp.exp2(m_old - m_new) * acc_ref[h] + _dot(vt_ref[j, h], p)
            m_ref[h] = m_new

    scores(0, 0)

    def pair(j):
        scores(j + 1, 1)
        consume(j, 0)
        scores(j + 2, 0)
        consume(j + 1, 1)

    def body_main(jj, carry):
        for u in range(C_UNROLL_PAIRS):
            pair(2 * (C_UNROLL_PAIRS * jj + u))
        return carry

    def body_rest(jj, carry):
        pair(2 * (C_UNROLL_PAIRS * n_main + jj))
        return carry

    n_main = i // C_UNROLL_PAIRS
    lax.fori_loop(0, n_main, body_main, 0)
    lax.fori_loop(0, i - C_UNROLL_PAIRS * n_main, body_rest, 0)
    scores(2 * i + 1, 1)
    consume(2 * i, 0, 0)
    consume(2 * i + 1, 1, tk)

    ys = []
    for h in range(N_HEADS):
        acc = acc_ref[h]
        ys.append(acc[:HD] / acc[HD:HD + 1])
    o_ref[...] = jnp.concatenate(ys, axis=0).T.astype(o_ref.dtype)


def _mla_prompt(dst, qt, kc, vt, batch, seq):
    tq = C_QTILE
    nq = seq // tq
    return pl.pallas_call(
        _mla_prompt_kernel, grid=(batch, nq),
        in_specs=[_DST_SPEC, pl.BlockSpec((N_HEADS, 256, tq), lambda b, i: (0, 0, b * nq + i)),
                  pl.BlockSpec((seq, 256), lambda b, i: (b, 0)),
                  pl.BlockSpec((seq // C_TILE, N_HEADS, V_ROWS, C_TILE), lambda b, i: (b, 0, 0, 0))],
        out_specs=pl.BlockSpec((tq, 256), lambda b, i: (b * nq + i, 0)),
        out_shape=jax.ShapeDtypeStruct(dst.shape, dst.dtype), input_output_aliases={0: 0},
        scratch_shapes=[pltpu.VMEM((N_HEADS, V_ROWS, tq), F32), pltpu.VMEM((N_HEADS, 1, tq), F32),
                        pltpu.VMEM((2, N_HEADS, C_TILE, tq), F32)],
        compiler_params=_cparams(("parallel", "arbitrary")), name="mla_prompt",
    )(dst, qt, kc, vt)


def _mla_sample_kernel(dst_ref, q_ref, ckv_ref, kr_ref, knew_ref, wuv_ref, o_ref):
    seq = q_ref.shape[1]
    qs = q_ref[...].reshape(N_HEADS * seq, 2 * LANE)
    kcache = jnp.concatenate([ckv_ref[...]] + [kr_ref[...]] * N_HEADS, axis=-1).astype(BF16)
    kall = jnp.concatenate([kcache, knew_ref[...]], axis=0)
    s = _dot_nt(qs, kall)
    m = jnp.max(s, axis=-1, keepdims=True)
    p = jnp.exp2(s - m)
    l = jnp.sum(p, axis=-1, keepdims=True)
    o = (_dot(p.astype(BF16), kall[:, :C_KV_LORA]) / l).astype(BF16)
    for h in range(N_HEADS):
        o_ref[:, h * HD:(h + 1) * HD] = _dot(o[h * seq:(h + 1) * seq], wuv_ref[h]).astype(o_ref.dtype)


def _mla_sample(dst, qabs, kc, cache_ckv, cache_kr, wuv, layer, row0, n_seq, seq):
    blk0 = row0 // seq
    past = cache_ckv.shape[2]
    return pl.pallas_call(
        _mla_sample_kernel, grid=(n_seq,),
        in_specs=[_DST_SPEC, pl.BlockSpec((N_HEADS, seq, 256), lambda b: (0, b, 0)),
                  pl.BlockSpec((None, None, past, C_KV_LORA), lambda b: (layer, b, 0, 0)),
                  pl.BlockSpec((None, None, past, C_ROPE), lambda b: (layer, b, 0, 0)),
                  pl.BlockSpec((seq, 256), lambda b: (blk0 + b, 0)),
                  _full_spec((N_HEADS, 128, HD))],
        out_specs=pl.BlockSpec((seq, 256), lambda b: (blk0 + b, 0)),
        out_shape=jax.ShapeDtypeStruct(dst.shape, dst.dtype), input_output_aliases={0: 0},
        compiler_params=_cparams(("parallel",)), name="mla_sample",
    )(dst, qabs, cache_ckv, cache_kr, kc, wuv)


def _outmlp_kernel(xp_ref, xs_ref, ya_ref, yb_ref, yc_ref, yd_ref, wout_ref, g2_ref, w1_ref, w2_ref, fg_ref,
                   op_ref, os_ref, *, final, n_prompt_tiles):
    i = pl.program_id(0)
    y = jnp.concatenate([ya_ref[...], yb_ref[...], yc_ref[...], yd_ref[...]], axis=-1)
    x1 = jnp.where(i < n_prompt_tiles, xp_ref[...], xs_ref[...]) + _dot(y, wout_ref[...])
    hm = _rms(x1, g2_ref[...]).astype(BF16)
    acc = x1
    for c in range(D_FF // D_MODEL):
        sl = slice(c * D_MODEL, (c + 1) * D_MODEL)
        hc = jnp.square(jnp.maximum(_dot(hm, w1_ref[:, sl]), 0.0)).astype(BF16)
        acc = acc + _dot(hc, w2_ref[sl, :])
    if final:
        acc = _rms(acc, fg_ref[...])

    @pl.when(i < n_prompt_tiles)
    def _():
        op_ref[...] = acc

    @pl.when(i >= n_prompt_tiles)
    def _():
        os_ref[...] = acc


def _outmlp(xp, xs, ya, yb, yc, yd, wout, g2, w1, w2, fg, layer, final, in_place):
    t = xp.shape[0] + xs.shape[0]
    tm = ROW_TILE
    npt = xp.shape[0] // tm
    row = lambda n: pl.BlockSpec((tm, n), lambda i: (i, 0))
    xspecs = _split_rows(npt, D_MODEL)
    return pl.pallas_call(
        functools.partial(_outmlp_kernel, final=final, n_prompt_tiles=npt), grid=(t // tm,),
        in_specs=[*xspecs, row(256), row(256), row(256), row(256), _layer_spec((D_MODEL, D_MODEL), layer),
                  _full_spec((1, D_MODEL)), _layer_spec((D_MODEL, D_FF), layer), _layer_spec((D_FF, D_MODEL), layer),
                  _full_spec((1, D_MODEL))],
        out_specs=xspecs,
        out_shape=(jax.ShapeDtypeStruct(xp.shape, F32), jax.ShapeDtypeStruct(xs.shape, F32)),
        input_output_aliases={0: 0, 1: 1} if in_place else {},
        compiler_params=_cparams(("arbitrary",)), name="outmlp",
    )(xp, xs, ya, yb, yc, yd, wout, g2, w1, w2, fg)


def _rot_cols(w):
    half = C_ROPE // 2
    return jnp.concatenate([-w[..., half:], w[..., :half]], axis=-1)


def _pad_lanes(v, n=LANE, at=0):
    out = jnp.zeros(v.shape[:-1] + (n,), v.dtype)
    return out.at[..., at:at + v.shape[-1]].set(v)


def _prep_w_in(w):
    sizes = (256, 256, 256, 256, 512, 4, 256, 128, 32, 256, 256, 256, 4, 4, 256)
    cuts = [0]
    for s in sizes:
        cuts.append(cuts[-1] + s)
    (a_q, a_k, a_v, b_z, b_xbc, b_dt, c_q, c_kv, c_kr, m_q, m_k, m_v, m_i, m_f, m_o) = (
        w[..., cuts[n]:cuts[n + 1]] for n in range(len(sizes)))
    kr4 = jnp.concatenate([c_kr] * N_HEADS, axis=-1)
    krr4 = jnp.concatenate([_rot_cols(c_kr)] * N_HEADS, axis=-1)
    gates = _pad_lanes(jnp.concatenate([b_dt, m_i, m_f], axis=-1))
    out = jnp.concatenate([a_q, a_k, a_v, b_z, b_xbc, c_q, c_kv, kr4, krr4, gates, m_q, m_k, m_v, m_o], axis=-1)
    assert out.shape[-1] == N_PROJ
    return out.astype(BF16)


def _prep_w_uq(w):
    d = w.shape[0]
    w4 = w.reshape(d, 256, N_HEADS, C_NOPE + C_ROPE)
    nope = w4[..., :C_NOPE].reshape(d, 256, N_HEADS * C_NOPE)
    rope = w4[..., C_NOPE:]
    return jnp.concatenate([nope, rope.reshape(d, 256, N_HEADS * C_ROPE),
                            _rot_cols(rope).reshape(d, 256, N_HEADS * C_ROPE)], axis=-1).astype(BF16)


def _prep_w_ukv(w):
    d = w.shape[0]
    w4 = w.reshape(d, C_KV_LORA, N_HEADS, C_NOPE + HD)
    w_uk = w4[..., :C_NOPE]
    w_uv = w4[..., C_NOPE:]
    wabs = jnp.zeros((d, N_HEADS * C_NOPE, N_HEADS * C_KV_LORA), w.dtype)
    for h in range(N_HEADS):
        wabs = wabs.at[:, h * C_NOPE:(h + 1) * C_NOPE, h * C_KV_LORA:(h + 1) * C_KV_LORA].set(
            jnp.swapaxes(w_uk[:, :, h, :], 1, 2))
    wuv = jnp.transpose(w_uv, (0, 2, 1, 3))
    wuv_all = w_uv.reshape(d, C_KV_LORA, N_HEADS * HD)
    return wabs.astype(BF16), wuv.astype(BF16), wuv_all.astype(BF16)


def _rope_tables(pos):
    half = C_ROPE // 2
    inv = jnp.exp(-math.log(ROPE_BASE) * jnp.arange(half, dtype=F32) / half)
    ang = pos.astype(F32)[:, None] * inv[None, :]
    reps = LANE // half
    return jnp.tile(jnp.cos(ang), (1, reps)), jnp.tile(jnp.sin(ang), (1, reps))


def _band_bias(table, q_rows):
    cols = q_rows + A_SPAN
    k = np.arange(q_rows + cols - 1)
    u = table[:, np.clip((q_rows - 1 - k) + A_SPAN, -A_REL_CLIP, A_REL_CLIP) + A_REL_CLIP].astype(F32)
    period = q_rows + cols
    u = jnp.pad(u, ((0, 0), (0, 1)))
    flat = jnp.tile(u, (1, q_rows))[:, q_rows - 1:q_rows - 1 + q_rows * (period - 1)]
    toep = flat.reshape(table.shape[0], q_rows, period - 1)[:, :, :cols]
    r = np.arange(q_rows)[:, None]
    s = np.arange(cols)[None, :] - (r // CHUNK) * CHUNK
    inband = (s >= 0) & (s < A_SPAN + CHUNK)
    return jnp.where(jnp.asarray(inband)[None], toep, NEG)


def kernel(x_prompt, x_sample, cache_attn_k, cache_attn_v, state_ssm_conv, state_ssm, cache_mla_ckv, cache_mla_kr,
           state_mlstm_c, state_mlstm_n, state_mlstm_m, norm1_g, w_in, attn_rel_bias, ssm_conv_w, ssm_conv_b,
           ssm_dt_bias, ssm_a_log, ssm_d, ssm_norm_g, mla_q_norm_g, mla_w_uq, mla_kv_norm_g, mla_w_ukv,
           mlstm_b_i, mlstm_b_f, mlstm_norm_g, w_out, norm2_g, mlp_w1, mlp_w2, final_g):
    batch, seq, _ = x_prompt.shape
    dbatch, dseq, _ = x_sample.shape
    depth = w_in.shape[0]
    past = cache_mla_ckv.shape[2]
    tp, ts = batch * seq, dbatch * dseq
    lp = min(REC_CHUNK_PROMPT, seq)
    assert dseq == CHUNK and past % CHUNK == 0 and cache_attn_k.shape[2] == A_SPAN
    assert seq % C_QTILE == 0 and tp % ROW_TILE == 0 and ts % ROW_TILE == 0

    w_in_r = _prep_w_in(w_in)
    wuq_r = _prep_w_uq(mla_w_uq)
    wabs, wuv, wuv_all = _prep_w_ukv(mla_w_ukv)
    w_out_b, w1_b, w2_b = w_out.astype(BF16), mlp_w1.astype(BF16), mlp_w2.astype(BF16)
    pos = jnp.concatenate([jnp.tile(jnp.arange(seq), batch), jnp.tile(past + jnp.arange(dseq), dbatch)])
    cos, sin = _rope_tables(pos)
    dtb = _pad_lanes(ssm_dt_bias, at=G_DT)[:, None, :]
    alog = _pad_lanes(ssm_a_log, at=G_DT)[:, None, :]
    dsk = _pad_lanes(ssm_d)[:, None, :]
    dtb_t = jnp.swapaxes(dtb[:, :, :GT_ROWS], 1, 2)
    alog_t = jnp.swapaxes(alog[:, :, :GT_ROWS], 1, 2)
    gate_bias =(_pad_lanes(mlstm_b_i, at=G_I) + _pad_lanes(mlstm_b_f, at=G_F))[:, None, :]
    gate_bias_t = jnp.swapaxes(gate_bias[:, :, :GT_ROWS], 1, 2)
    cache_k = cache_attn_k.reshape(depth, dbatch, A_SPAN, 256)
    cache_v = cache_attn_v.reshape(depth, dbatch, A_SPAN, 256)
    conv0_s = jnp.pad(state_ssm_conv, ((0, 0), (0, 0), (8 - (B_CONV - 1), 0), (0, 0)))
    caug0_s = jnp.concatenate([state_mlstm_c, state_mlstm_n[..., None],
                               jnp.zeros(state_mlstm_c.shape[:-1] + (LANE - HD - 1,), F32)], axis=-1)
    m0_s = _pad_lanes(state_mlstm_m)[:, :, None, :]

    xp, xs = x_prompt.reshape(tp, D_MODEL), x_sample.reshape(ts, D_MODEL)
    outs = [[] for _ in range(18)]
    ya, yb, yc, yd = (jnp.zeros((tp + ts, GROUP_W), BF16) for _ in range(4))
    fg = final_g[None, :]
    for l in range(depth):
        (aq, akv, bz, bxbc, qabs, qt, ckv, kr, kc, vt, mqkv, mo, gates, mkt, mvt, gt, aqt, akh, avt) = _inproj(
            xp, xs, norm1_g[l][None], w_in_r, mla_q_norm_g[l][None], mla_kv_norm_g[l][None], wuq_r, wabs,
            wuv_all, cos, sin, l)

        rel = attn_rel_bias[l] * math.log2(math.e)
        ya = _attn_a_prompt(ya, aqt, akh, avt, jnp.swapaxes(_band_bias(rel, A_QBLOCK), 1, 2), batch, seq)
        ya = _attn_a_sample(ya, aq, akv, cache_k, cache_v, _band_bias(rel, CHUNK), l, tp, dbatch, dseq)

        ssd_w = (ssm_conv_w[l], ssm_conv_b[l][None], dtb[l], alog[l], dsk[l], ssm_norm_g[l][None])
        yb, h_p = _ssd_prompt(yb, bz, bxbc, gates, gt, ssm_conv_w[l], ssm_conv_b[l][None], dtb[l], dtb_t[l], alog[l],
                              alog_t[l], dsk[l], ssm_norm_g[l][None], batch, seq, lp)
        yb, h_s = _ssd(yb, bz, bxbc, gates, conv0_s[l], state_ssm[l], *ssd_w, tp, dbatch, dseq, dseq)

        yc = _mla_prompt(yc, qt, kc, vt, batch, seq)
        yc = _mla_sample(yc, qabs, kc, cache_mla_ckv, cache_mla_kr, wuv[l], l, tp, dbatch, dseq)

        ml_w = (gate_bias[l], mlstm_norm_g[l][None])
        yd, ct_p, m_p = _mlstm_prompt(yd, mqkv, mkt, mvt, mo, gates, gt, gate_bias[l], gate_bias_t[l],
                                      mlstm_norm_g[l][None], batch, seq, lp)
        yd, c_s, m_s = _mlstm(yd, mqkv, mo, gates, caug0_s[l], m0_s[l], *ml_w, tp, dbatch, dseq, dseq)

        xp, xs = _outmlp(xp, xs, ya, yb, yc, yd, w_out_b, norm2_g[l][None], w1_b, w2_b, fg, l, l == depth - 1, l > 0)

        keep = min(A_SPAN, seq)
        tail = lambda a, n: jnp.stack([a[(b + 1) * seq - n:(b + 1) * seq] for b in range(batch)])
        akv_p = tail(akv, keep).reshape(batch, keep, 2, N_HEADS, HD)
        akv_s = akv[tp:].reshape(dbatch, dseq, 2, N_HEADS, HD)
        new = (akv_p[:, :, 0], akv_p[:, :, 1], tail(bxbc, B_CONV - 1), h_p,
               ckv[:tp].reshape(batch, seq, C_KV_LORA), kr[:tp].reshape(batch, seq, C_ROPE),
               jnp.swapaxes(ct_p[:, :, :HD, :], 2, 3), ct_p[:, :, HD, :], m_p[:, 0, :N_HEADS],
               jnp.concatenate([cache_attn_k[l][:, dseq:], akv_s[:, :, 0]], axis=1),
               jnp.concatenate([cache_attn_v[l][:, dseq:], akv_s[:, :, 1]], axis=1),
               bxbc[tp:].reshape(dbatch, dseq, B_CONV_DIM)[:, dseq - (B_CONV - 1):], h_s,
               ckv[tp:].reshape(dbatch, dseq, C_KV_LORA), kr[tp:].reshape(dbatch, dseq, C_ROPE),
               c_s[..., :HD], c_s[..., HD], m_s[:, 0, :N_HEADS])
        for dst, v in zip(outs, new):
            dst.append(v)

    y_prompt = xp.reshape(batch, seq, D_MODEL)
    y_sample = xs.reshape(dbatch, dseq, D_MODEL)
    return (y_prompt, y_sample) + tuple(jnp.stack(v) for v in outs)
```

```python
import functools
import math

import jax
import jax.numpy as jnp
import numpy as np
from jax import lax
from jax.experimental import pallas as pl
from jax.experimental.pallas import tpu as pltpu

F32 = jnp.float32
BF16 = jnp.bfloat16
EPS = 1e-6
NEG = -1e30
ROPE_BASE = 10000.0

CHUNK = 64
D_MODEL = 1024
GROUP_W = 256
D_FF = 4096
N_HEADS = 4
HD = 64
A_BAND_CHUNKS = 8
A_SPAN = A_BAND_CHUNKS * CHUNK
A_REL_CLIP = 128
B_GROUPS = 2
B_STATE = 64
B_CONV = 4
B_CONV_DIM = 512
C_NOPE = 64
C_ROPE = 32
C_KV_LORA = 128
A_QSCALE = HD ** -0.5 * math.log2(math.e)
C_SCALE = (C_NOPE + C_ROPE) ** -0.5
C_QSCALE = C_SCALE * math.log2(math.e)
V_ROWS = 80
LANE = 128
VMEM_LIMIT = 56 * 1024 * 1024

OFF_A, OFF_BZ, OFF_XBC, OFF_CQ, OFF_CKV, OFF_KR, OFF_KRR, OFF_G, OFF_MQKV, OFF_MO, N_PROJ = (
    0, 768, 1024, 1536, 1792, 1920, 2048, 2176, 2304, 3072, 3328)
G_DT, G_I, G_F = 0, 4, 8
GT_ROWS = 16

ROW_TILE = 512
A_QBLOCK = 256
C_TILE = 256
C_QTILE = 2 * C_TILE
C_UNROLL_PAIRS = 4
REC_CHUNK_PROMPT = 256
REC_STEP_CHUNKS = 4


def _cparams(sem):
    return pltpu.CompilerParams(dimension_semantics=sem, vmem_limit_bytes=VMEM_LIMIT)


def _rms(x, g):
    return x * lax.rsqrt(jnp.mean(x * x, axis=-1, keepdims=True) + EPS) * g


def _dot(a, b):
    return jnp.dot(a, b, preferred_element_type=F32)


def _dot_nt(a, b):
    return lax.dot_general(a, b, (((1,), (1,)), ((), ())), preferred_element_type=F32)


def _dot_tn(a, b):
    return lax.dot_general(a, b, (((0,), (0,)), ((), ())), preferred_element_type=F32)


def _cumsum_rows(x, tri_bf16):
    hi = x.astype(BF16)
    r1 = x - hi.astype(F32)
    mid = r1.astype(BF16)
    lo = (r1 - mid.astype(F32)).astype(BF16)
    return _dot(tri_bf16, hi) + _dot(tri_bf16, mid) + _dot(tri_bf16, lo)


def _log1p_exp_neg_abs(x):
    return jnp.log1p(jnp.exp(-jnp.abs(x)))


def _full_spec(shape):
    n = len(shape)
    return pl.BlockSpec(shape, lambda *_: (0,) * n)


_DST_SPEC = pl.BlockSpec(memory_space=pl.ANY)


def _const_spec(shape):
    n = len(shape)
    return pl.BlockSpec(shape, lambda *_: (0,) * n, pipeline_mode=pl.Buffered(1))


def _layer_spec(shape, layer):
    n = len(shape)
    return pl.BlockSpec((None,) + tuple(shape), lambda *_: (layer,) + (0,) * n, pipeline_mode=pl.Buffered(1))


def _split_rows(n_prompt_tiles, width):
    npt = n_prompt_tiles
    return (pl.BlockSpec((ROW_TILE, width), lambda i: (jnp.minimum(i, npt - 1), 0)),
            pl.BlockSpec((ROW_TILE, width), lambda i: (jnp.maximum(i - npt, 0), 0)))


def _inproj_kernel(xp_ref, xs_ref, g1_ref, w_ref, qg_ref, kvg_ref, wuq_ref, wabs_ref, wuv_ref, cos_ref, sin_ref,
                   aq_ref, akv_ref, bz_ref, bxbc_ref, qabs_ref, qt_ref, ckv_ref, kr_ref, kc_ref, vt_ref,
                   mqkv_ref, mo_ref, gates_ref, mkt_ref, mvt_ref, gt_ref, aqt_ref, akh_ref, avt_ref, *,
                   n_prompt_tiles):
    tm = xp_ref.shape[0]
    is_prompt = pl.program_id(0) < n_prompt_tiles
    x = jnp.where(is_prompt, xp_ref[...], xs_ref[...])
    hn = _rms(x, g1_ref[...]).astype(BF16)

    def seg(a, b):
        return _dot(hn, w_ref[:, a:b])

    a = seg(OFF_A, OFF_BZ)
    aq = a[:, :GROUP_W] * A_QSCALE
    aq_ref[...] = aq.astype(BF16)
    akv_ref[...] = a[:, GROUP_W:]
    aqt_ref[...] = aq.T.astype(BF16)
    avt_ref[...] = a[:, 2 * GROUP_W:].T.astype(BF16)
    for h in range(N_HEADS):
        akh_ref[h] = a[:, GROUP_W + h * HD:GROUP_W + (h + 1) * HD].astype(BF16)
    bz_ref[...] = seg(OFF_BZ, OFF_XBC)
    bxbc_ref[...] = seg(OFF_XBC, OFF_CQ)

    cos = cos_ref[...]
    sin = sin_ref[...]
    cqn = _rms(seg(OFF_CQ, OFF_CKV), qg_ref[...]).astype(BF16)
    qf = _dot(cqn, wuq_ref[...])
    qrope = (qf[:, 256:384] * cos + qf[:, 384:512] * sin) * C_QSCALE
    qlat = _dot(qf[:, :256].astype(BF16), wabs_ref[...]) * C_QSCALE
    lane_head = lax.broadcasted_iota(jnp.int32, (tm, LANE), 1) // C_ROPE
    qhs = [jnp.concatenate([qlat[:, h * LANE:(h + 1) * LANE], jnp.where(lane_head == h, qrope, 0.0)], axis=-1)
           for h in range(N_HEADS)]

    ckv_kr = seg(OFF_CKV, OFF_KRR)
    krr_g = seg(OFF_KRR, OFF_MQKV)
    g = krr_g[:, LANE:]
    ckv = _rms(ckv_kr[:, :C_KV_LORA], kvg_ref[...])
    kr4 = ckv_kr[:, C_KV_LORA:] * cos + krr_g[:, :LANE] * sin
    ckv_ref[...] = ckv
    kr_ref[...] = kr4[:, :C_ROPE]
    kc_ref[...] = jnp.concatenate([ckv, kr4], axis=-1).astype(BF16)

    m = seg(OFF_MQKV, OFF_MO)
    mk = m[:, 256:512] * HD ** -0.5
    mqkv_ref[:, 0:256] = m[:, 0:256].astype(BF16)
    mqkv_ref[:, 256:512] = mk.astype(BF16)
    mqkv_ref[:, 512:768] = m[:, 512:768].astype(BF16)
    mo_ref[...] = seg(OFF_MO, N_PROJ)
    gates_ref[...] = g

    for h in range(N_HEADS):
        qt_ref[h] = qhs[h].T.astype(BF16)
    vall = _dot(ckv.astype(BF16), wuv_ref[...])
    ones_c = (lax.broadcasted_iota(jnp.int32, (V_ROWS - HD, C_TILE), 0) == 0).astype(BF16)
    for s in range(tm // C_TILE):
        vt = vall[s * C_TILE:(s + 1) * C_TILE, :].T
        for h in range(N_HEADS):
            vt_ref[s, h, 0:HD, :] = vt[h * HD:(h + 1) * HD].astype(BF16)
            vt_ref[s, h, HD:V_ROWS, :] = ones_c
    mkt = mk.T
    mvt = m[:, 512:768].T
    ones_v = (lax.broadcasted_iota(jnp.int32, (LANE - HD, tm), 0) == 0).astype(BF16)
    for h in range(N_HEADS):
        mkt_ref[h] = mkt[h * HD:(h + 1) * HD].astype(BF16)
        mvt_ref[h, 0:HD, :] = mvt[h * HD:(h + 1) * HD].astype(BF16)
        mvt_ref[h, HD:LANE, :] = ones_v
    gt_ref[...] = g.T[0:GT_ROWS]

    @pl.when(jnp.logical_not(is_prompt))
    def _():
        for h in range(N_HEADS):
            qabs_ref[h] = qhs[h].astype(BF16)


def _inproj(xp, xs, g1, w, qg, kvg, wuq, wabs, wuv_all, cos, sin, layer):
    tp, ts = xp.shape[0], xs.shape[0]
    t = tp + ts
    tm = ROW_TILE
    npt = tp // tm
    col = lambda *lead: pl.BlockSpec((*lead, tm), lambda i: (0,) * len(lead) + (i,))
    row = lambda n: pl.BlockSpec((tm, n), lambda i: (i, 0))
    out_shape = (
        jax.ShapeDtypeStruct((t, 256), BF16),
        jax.ShapeDtypeStruct((t, 512), F32),
        jax.ShapeDtypeStruct((t, 256), F32),
        jax.ShapeDtypeStruct((t, 512), F32),
        jax.ShapeDtypeStruct((N_HEADS, ts, 256), BF16),
        jax.ShapeDtypeStruct((N_HEADS, 256, t), BF16),
        jax.ShapeDtypeStruct((t, 128), F32),
        jax.ShapeDtypeStruct((t, C_ROPE), F32),
        jax.ShapeDtypeStruct((t, 256), BF16),
        jax.ShapeDtypeStruct((t // C_TILE, N_HEADS, V_ROWS, C_TILE), BF16),
        jax.ShapeDtypeStruct((t, 768), BF16),
        jax.ShapeDtypeStruct((t, 256), F32),
        jax.ShapeDtypeStruct((t, LANE), F32),
        jax.ShapeDtypeStruct((N_HEADS, HD, t), BF16),
        jax.ShapeDtypeStruct((N_HEADS, LANE, t), BF16),
        jax.ShapeDtypeStruct((GT_ROWS, t), F32),
        jax.ShapeDtypeStruct((GROUP_W, t), BF16),
        jax.ShapeDtypeStruct((N_HEADS, t, HD), BF16),
        jax.ShapeDtypeStruct((GROUP_W, t), BF16),
    )
    out_specs = (
        row(256), row(512), row(256), row(512),
        pl.BlockSpec((N_HEADS, tm, 256), lambda i: (0, jnp.maximum(i - npt, 0), 0)),
        col(N_HEADS, 256),
        row(128), row(C_ROPE), row(256),
        pl.BlockSpec((tm // C_TILE, N_HEADS, V_ROWS, C_TILE), lambda i: (i, 0, 0, 0)),
        row(768), row(256), row(LANE),
        col(N_HEADS, HD), col(N_HEADS, LANE), col(GT_ROWS),
        col(GROUP_W), pl.BlockSpec((N_HEADS, tm, HD), lambda i: (0, i, 0)), col(GROUP_W),
    )
    in_specs = [*_split_rows(npt, D_MODEL), _full_spec((1, D_MODEL)), _layer_spec((D_MODEL, N_PROJ), layer),
                _full_spec((1, 256)), _full_spec((1, 128)), _layer_spec((256, 512), layer),
                _layer_spec((256, 512), layer), _layer_spec((C_KV_LORA, 256), layer), row(LANE), row(LANE)]
    return pl.pallas_call(
        functools.partial(_inproj_kernel, n_prompt_tiles=npt), grid=(t // tm,), in_specs=in_specs,
        out_specs=out_specs, out_shape=out_shape,
        compiler_params=_cparams(("arbitrary",)), name="inproj",
    )(xp, xs, g1, w, qg, kvg, wuq, wabs, wuv_all, cos, sin)


def _band_heads(q, kb, vb, bias_ref, valid, o_ref):
    for h in range(N_HEADS):
        sl = slice(h * HD, (h + 1) * HD)
        s = _dot_nt(q[:, sl], kb[:, sl]) + bias_ref[h]
        if valid is not None:
            s = jnp.where(valid, s, NEG)
        m = jnp.max(s, axis=-1, keepdims=True)
        p = jnp.exp2(s - m)
        l = jnp.sum(p, axis=-1, keepdims=True)
        o = _dot(p.astype(BF16), vb[:, sl]) / l
        o_ref[:, sl] = o.astype(o_ref.dtype)


def _attn_a_prompt_kernel(dst_ref, qt_ref, k0_ref, k1_ref, k2_ref, v0_ref, v1_ref, v2_ref, bias_ref, o_ref, s_ref):
    i = pl.program_id(1)
    nk = 3 * A_QBLOCK
    vt = jnp.concatenate([v0_ref[...], v1_ref[...], v2_ref[...]], axis=1)
    ones = (lax.broadcasted_iota(jnp.int32, (V_ROWS - HD, nk), 0) == 0).astype(BF16)

    def scores(h):
        kh = jnp.concatenate([k0_ref[h], k1_ref[h], k2_ref[h]], axis=0)
        s_ref[h % 2] = _dot(kh, qt_ref[h * HD:(h + 1) * HD, :]) + bias_ref[h]

    def heads(masked):
        outs = []
        scores(0)
        for h in range(N_HEADS):
            if h + 1 < N_HEADS:
                scores(h + 1)
            s = s_ref[h % 2]
            if masked:
                krow = lax.broadcasted_iota(jnp.int32, (nk, A_QBLOCK), 0)
                s = jnp.where(krow >= (2 - i) * A_QBLOCK, s, NEG)
            m = jnp.max(s, axis=0, keepdims=True)
            p = jnp.exp2(s - m).astype(BF16)
            vaug = jnp.concatenate([vt[h * HD:(h + 1) * HD, :], ones], axis=0)
            o = _dot(vaug, p)
            outs.append(o[:HD] / o[HD:HD + 1])
        o_ref[...] = jnp.concatenate(outs, axis=0).T.astype(o_ref.dtype)

    @pl.when(i < 2)
    def _():
        heads(True)

    @pl.when(i >= 2)
    def _():
        heads(False)


def _attn_a_prompt(dst, aqt, akh, avt, biasm_t, batch, seq):
    nb = seq // A_QBLOCK
    qb = A_QBLOCK
    back = lambda b, i, n: b * nb + jnp.maximum(i - n, 0)
    kspec = lambda n: pl.BlockSpec((N_HEADS, qb, HD), lambda b, i: (0, back(b, i, n), 0))
    vspec = lambda n: pl.BlockSpec((GROUP_W, qb), lambda b, i: (0, back(b, i, n)))
    return pl.pallas_call(
        _attn_a_prompt_kernel, grid=(batch, nb),
        in_specs=[_DST_SPEC, pl.BlockSpec((GROUP_W, qb), lambda b, i: (0, b * nb + i)),
                  kspec(2), kspec(1), kspec(0), vspec(2), vspec(1), vspec(0), _full_spec(biasm_t.shape)],
        out_specs=pl.BlockSpec((qb, 256), lambda b, i: (b * nb + i, 0)),
        out_shape=jax.ShapeDtypeStruct(dst.shape, dst.dtype), input_output_aliases={0: 0},
        scratch_shapes=[pltpu.VMEM((2, 3 * qb, qb), F32)],
        compiler_params=_cparams(("parallel", "parallel")), name="attn_a_prompt",
    )(dst, aqt, akh, akh, akh, avt, avt, avt, biasm_t)


def _attn_a_sample_kernel(dst_ref, q_ref, ck_ref, cv_ref, kvn_ref, bias_ref, o_ref):
    kvn = kvn_ref[...]
    kb = jnp.concatenate([ck_ref[...], kvn[:, :GROUP_W]], axis=0).astype(BF16)
    vb = jnp.concatenate([cv_ref[...], kvn[:, GROUP_W:]], axis=0).astype(BF16)
    _band_heads(q_ref[...], kb, vb, bias_ref, None, o_ref)


def _attn_a_sample(dst, aq, akv, cache_k, cache_v, bias, layer, row0, n_seq, seq):
    blk0 = row0 // seq
    cspec = pl.BlockSpec((None, None, A_SPAN, 256), lambda b: (layer, b, 0, 0))
    return pl.pallas_call(
        _attn_a_sample_kernel, grid=(n_seq,),
        in_specs=[_DST_SPEC, pl.BlockSpec((seq, 256), lambda b: (blk0 + b, 0)), cspec, cspec,
                  pl.BlockSpec((seq, 512), lambda b: (blk0 + b, 0)), _full_spec(bias.shape)],
        out_specs=pl.BlockSpec((seq, 256), lambda b: (blk0 + b, 0)),
        out_shape=jax.ShapeDtypeStruct(dst.shape, dst.dtype), input_output_aliases={0: 0},
        compiler_params=_cparams(("parallel",)), name="attn_a_sample",
    )(dst, aq, cache_k, cache_v, akv, bias)


def _ssd_kernel(dst_ref, z_ref, xbc_ref, g_ref, conv0_ref, h0_ref, cw_ref, cb_ref, dtb_ref, alog_ref, dsk_ref, ng_ref,
                y_ref, hout_ref, xpad_ref, h_ref):
    L = z_ref.shape[0]
    c = pl.program_id(1)

    @pl.when(c == 0)
    def _():
        xpad_ref[...] = conv0_ref[...]
        h_ref[...] = h0_ref[...]

    xbc = xbc_ref[...]
    xfull = jnp.concatenate([xpad_ref[...], xbc], axis=0)
    conv = cb_ref[...] + xbc * cw_ref[B_CONV - 1:B_CONV, :]
    for j in range(B_CONV - 1):
        conv = conv + pltpu.roll(xfull, B_CONV - 1 - j, axis=0)[8:8 + L, :] * cw_ref[j:j + 1, :]
    xpad_ref[...] = xbc[L - 8:L, :]
    u = conv * jax.nn.sigmoid(conv)

    g = g_ref[...] + dtb_ref[...]
    dt = jnp.maximum(g, 0.0) + _log1p_exp_neg_abs(g)
    da = dt * (-jnp.exp(alog_ref[...]))
    row = lax.broadcasted_iota(jnp.int32, (L, L), 0)
    colm = lax.broadcasted_iota(jnp.int32, (L, L), 1)
    tri = row >= colm
    cs = _cumsum_rows(da, tri.astype(BF16))
    cs_t = cs.T

    ys = []
    gmat = [None] * B_GROUPS
    for h in range(N_HEADS):
        grp = h // (N_HEADS // B_GROUPS)
        xs = u[:, h * HD:(h + 1) * HD]
        bm = u[:, 256 + grp * B_STATE:256 + (grp + 1) * B_STATE].astype(BF16)
        cm = u[:, 384 + grp * B_STATE:384 + (grp + 1) * B_STATE].astype(BF16)
        if gmat[grp] is None:
            gmat[grp] = _dot_nt(cm, bm)
        csc = cs[:, G_DT + h:G_DT + h + 1]
        csr = cs_t[G_DT + h:G_DT + h + 1, :]
        tot = cs[L - 1:L, G_DT + h:G_DT + h + 1]
        dec = jnp.exp(jnp.where(tri, csc - csr, NEG))
        xdt = xs * dt[:, G_DT + h:G_DT + h + 1]
        hprev = h_ref[h]
        y = _dot((gmat[grp] * dec).astype(BF16), xdt.astype(BF16))
        y = y + _dot_nt(cm, hprev.astype(BF16)) * jnp.exp(csc)
        y = y + dsk_ref[0:1, h:h + 1] * xs
        ys.append(y)
        wend = (xdt * jnp.exp(tot - csc)).astype(BF16)
        h_ref[h] = jnp.exp(tot) * hprev + _dot_tn(wend, bm)
    y = jnp.concatenate(ys, axis=-1)
    z = z_ref[...]
    y = y * (z * jax.nn.sigmoid(z))
    y_ref[...] = _rms(y, ng_ref[...]).astype(y_ref.dtype)

    @pl.when(c == pl.num_programs(1) - 1)
    def _():
        hout_ref[...] = h_ref[...]


def _ssd(dst, bz, bxbc, gates, conv0, h0, cw, cb, dtb, alog, dsk, ng, row0, n_seq, seq, L):
    nc = seq // L
    blk0 = row0 // L
    rspec = lambda n: pl.BlockSpec((L, n), lambda b, c: (blk0 + b * nc + c, 0))
    return pl.pallas_call(
        _ssd_kernel, grid=(n_seq, nc),
        in_specs=[_DST_SPEC, rspec(256), rspec(512), rspec(LANE),
                  pl.BlockSpec((None, 8, 512), lambda b, c: (b, 0, 0)),
                  pl.BlockSpec((None, N_HEADS, HD, B_STATE), lambda b, c: (b, 0, 0, 0)),
                  _full_spec((B_CONV, 512)), _full_spec((1, 512)), _full_spec((1, LANE)), _full_spec((1, LANE)),
                  _full_spec((1, LANE)), _full_spec((1, 256))],
        out_specs=(rspec(256), pl.BlockSpec((None, N_HEADS, HD, B_STATE), lambda b, c: (b, 0, 0, 0))),
        input_output_aliases={0: 0},
        out_shape=(jax.ShapeDtypeStruct(dst.shape, dst.dtype),
                   jax.ShapeDtypeStruct((n_seq, N_HEADS, HD, B_STATE), F32)),
        scratch_shapes=[pltpu.VMEM((8, 512), F32), pltpu.VMEM((N_HEADS, HD, B_STATE), F32)],
        compiler_params=_cparams(("parallel", "arbitrary")), name="ssd",
    )(dst, bz, bxbc, gates, conv0, h0, cw, cb, dtb, alog, dsk, ng)


def _ssd_prompt_kernel(dst_ref, z_ref, xbc_ref, g_ref, gt_ref, cw_ref, cb_ref, dtb_ref, dtbt_ref, alog_ref, alogt_ref,
                       dsk_ref, ng_ref, y_ref, hout_ref, xpad_ref, h_ref):
    L = z_ref.shape[0]
    c = pl.program_id(1)

    @pl.when(c == 0)
    def _():
        xpad_ref[...] = jnp.zeros_like(xpad_ref)
        h_ref[...] = jnp.zeros_like(h_ref)

    xbc = xbc_ref[...]
    xfull = jnp.concatenate([xpad_ref[...], xbc], axis=0)
    conv = cb_ref[...] + xbc * cw_ref[B_CONV - 1:B_CONV, :]
    for j in range(B_CONV - 1):
        conv = conv + pltpu.roll(xfull, B_CONV - 1 - j, axis=0)[8:8 + L, :] * cw_ref[j:j + 1, :]
    xpad_ref[...] = xbc[L - 8:L, :]
    u_all = conv * jax.nn.sigmoid(conv)

    Lc = REC_CHUNK_PROMPT
    row = lax.broadcasted_iota(jnp.int32, (Lc, Lc), 0)
    colm = lax.broadcasted_iota(jnp.int32, (Lc, Lc), 1)
    s_le_l = row <= colm
    tri_col = (row >= colm).astype(BF16)
    tri_row = s_le_l.astype(BF16)
    hs = [h_ref[h] for h in range(N_HEADS)]

    for r0 in range(0, L, Lc):
        rows = slice(r0, r0 + Lc)
        u = u_all[rows, :]
        g = g_ref[rows, :] + dtb_ref[...]
        gt = gt_ref[:, rows] + dtbt_ref[...]
        dt = jnp.maximum(g, 0.0) + _log1p_exp_neg_abs(g)
        dtt = jnp.maximum(gt, 0.0) + _log1p_exp_neg_abs(gt)
        cs = _cumsum_rows(dt * (-jnp.exp(alog_ref[...])), tri_col)
        cs_t = _cumsum_lanes(dtt * (-jnp.exp(alogt_ref[...])), tri_row)

        xst = u[:, :GROUP_W].T
        yts = []
        gmat_t = [None] * B_GROUPS
        for h in range(N_HEADS):
            grp = h // (N_HEADS // B_GROUPS)
            bm = u[:, 256 + grp * B_STATE:256 + (grp + 1) * B_STATE].astype(BF16)
            cm = u[:, 384 + grp * B_STATE:384 + (grp + 1) * B_STATE].astype(BF16)
            if gmat_t[grp] is None:
                gmat_t[grp] = _dot_nt(bm, cm)
            csc = cs[:, G_DT + h:G_DT + h + 1]
            csr = cs_t[G_DT + h:G_DT + h + 1, :]
            tot = cs_t[G_DT + h:G_DT + h + 1, Lc - 1:Lc]
            xs_t = xst[h * HD:(h + 1) * HD]
            xdt_t = xs_t * dtt[G_DT + h:G_DT + h + 1, :]
            dec_t = jnp.exp(jnp.where(s_le_l, csr - csc, NEG))
            hprev = hs[h]
            y = _dot(xdt_t.astype(BF16), (gmat_t[grp] * dec_t).astype(BF16))
            y = y + _dot_nt(hprev.astype(BF16), cm) * jnp.exp(csr)
            yts.append(y + dsk_ref[0:1, h:h + 1] * xs_t)
            wend = (xdt_t * jnp.exp(tot - csr)).astype(BF16)
            hs[h] = jnp.exp(tot) * hprev + _dot(wend, bm)
        y = jnp.concatenate(yts, axis=0).T
        z = z_ref[rows, :]
        y = y * (z * jax.nn.sigmoid(z))
        y_ref[rows, :] = _rms(y, ng_ref[...]).astype(y_ref.dtype)

    for h in range(N_HEADS):
        h_ref[h] = hs[h]

    @pl.when(c == pl.num_programs(1) - 1)
    def _():
        hout_ref[...] = h_ref[...]


def _ssd_prompt(dst, bz, bxbc, gates, gt, cw, cb, dtb, dtbt, alog, alogt, dsk, ng, n_seq, seq, L):
    nc = seq // L
    rspec = lambda n: pl.BlockSpec((L, n), lambda b, c: (b * nc + c, 0))
    return pl.pallas_call(
        _ssd_prompt_kernel, grid=(n_seq, nc),
        in_specs=[_DST_SPEC, rspec(256), rspec(512), rspec(LANE),
                  pl.BlockSpec((GT_ROWS, L), lambda b, c: (0, b * nc + c)),
                  _full_spec((B_CONV, 512)), _full_spec((1, 512)), _full_spec((1, LANE)), _full_spec((GT_ROWS, 1)),
                  _full_spec((1, LANE)), _full_spec((GT_ROWS, 1)), _full_spec((1, LANE)), _full_spec((1, 256))],
        out_specs=(rspec(256), pl.BlockSpec((None, N_HEADS, HD, B_STATE), lambda b, c: (b, 0, 0, 0))),
        input_output_aliases={0: 0},
        out_shape=(jax.ShapeDtypeStruct(dst.shape, dst.dtype),
                   jax.ShapeDtypeStruct((n_seq, N_HEADS, HD, B_STATE), F32)),
        scratch_shapes=[pltpu.VMEM((8, 512), F32), pltpu.VMEM((N_HEADS, HD, B_STATE), F32)],
        compiler_params=_cparams(("parallel", "arbitrary")), name="ssd_prompt",
    )(dst, bz, bxbc, gates, gt, cw, cb, dtb, dtbt, alog, alogt, dsk, ng)


def _mlstm_kernel(dst_ref, qkv_ref, o_ref, g_ref, c0_ref, m0_ref, bias_ref, ng_ref,
                  y_ref, cout_ref, mout_ref, c_ref, m_ref):
    L = qkv_ref.shape[0]
    c = pl.program_id(1)

    @pl.when(c == 0)
    def _():
        c_ref[...] = c0_ref[...]
        m_ref[...] = m0_ref[...]

    g = g_ref[...] + bias_ref[...]
    lf = jnp.minimum(g, 0.0) - _log1p_exp_neg_abs(g)
    row = lax.broadcasted_iota(jnp.int32, (L, L), 0)
    colm = lax.broadcasted_iota(jnp.int32, (L, L), 1)
    tri = row >= colm
    bcum = _cumsum_rows(lf, tri.astype(BF16))
    bcum_t = bcum.T
    g_t = g.T
    unit = (lax.broadcasted_iota(jnp.int32, (L, HD), 1) == 0).astype(BF16)
    ng = ng_ref[...]
    og = o_ref[...]

    for h in range(N_HEADS):
        sl = slice(h * HD, (h + 1) * HD)
        q = qkv_ref[:, h * HD:(h + 1) * HD]
        k = qkv_ref[:, 256 + h * HD:256 + (h + 1) * HD]
        v = qkv_ref[:, 512 + h * HD:512 + (h + 1) * HD]
        vaug = jnp.concatenate([v, unit], axis=-1)
        bc = bcum[:, G_F + h:G_F + h + 1]
        br = bcum_t[G_F + h:G_F + h + 1, :]
        ic = g[:, G_I + h:G_I + h + 1]
        ir = g_t[G_I + h:G_I + h + 1, :]
        tot = bcum[L - 1:L, G_F + h:G_F + h + 1]
        mprev = m_ref[0:1, h:h + 1]

        dm = jnp.where(tri, bc - br + ir, NEG)
        inter = bc + mprev
        mt = jnp.maximum(inter, jnp.max(dm, axis=-1, keepdims=True))
        w_intra = jnp.exp(dm - mt)
        w_inter = jnp.exp(inter - mt)
        caug = c_ref[h]
        s = _dot_nt(q, k) * w_intra
        r = w_inter * _dot(q, caug.astype(BF16)) + _dot(s.astype(BF16), vaug)
        num = r[:, :HD]
        den = r[:, HD:HD + 1]
        hout = num / jnp.maximum(jnp.abs(den), jnp.exp(-mt))

        gend = tot - bc + ic
        mnew = jnp.maximum(tot + mprev, jnp.max(gend, axis=0, keepdims=True))
        kw = (k.astype(F32) * jnp.exp(gend - mnew)).astype(BF16)
        c_ref[h] = jnp.exp(tot + mprev - mnew) * caug + _dot_tn(kw, vaug)
        m_ref[0:1, h:h + 1] = mnew

        yh = _rms(hout, ng[:, sl]) * jax.nn.sigmoid(og[:, sl])
        y_ref[:, sl] = yh.astype(y_ref.dtype)

    @pl.when(c == pl.num_programs(1) - 1)
    def _():
        cout_ref[...] = c_ref[...]
        mout_ref[...] = m_ref[...]


def _cumsum_lanes(x, tri_bf16):
    hi = x.astype(BF16)
    r1 = x - hi.astype(F32)
    mid = r1.astype(BF16)
    lo = (r1 - mid.astype(F32)).astype(BF16)
    return _dot(hi, tri_bf16) + _dot(mid, tri_bf16) + _dot(lo, tri_bf16)


def _mlstm_prompt_kernel(dst_ref, qkv_ref, kt_ref, vt_ref, o_ref, g_ref, gt_ref, bias_ref, biast_ref, ng_ref,
                         y_ref, cout_ref, mout_ref, c_ref, m_ref):
    L = REC_CHUNK_PROMPT
    c = pl.program_id(1)

    @pl.when(c == 0)
    def _():
        c_ref[...] = jnp.zeros_like(c_ref)
        m_ref[...] = jnp.zeros_like(m_ref)

    row = lax.broadcasted_iota(jnp.int32, (L, L), 0)
    colm = lax.broadcasted_iota(jnp.int32, (L, L), 1)
    s_le_l = row <= colm
    tri_col = (row >= colm).astype(BF16)
    tri_row = s_le_l.astype(BF16)
    cts = [c_ref[h] for h in range(N_HEADS)]
    ms = [m_ref[0:1, h:h + 1] for h in range(N_HEADS)]

    for r0 in range(0, qkv_ref.shape[0], L):
        rows = slice(r0, r0 + L)
        g = g_ref[rows, :] + bias_ref[...]
        gt = gt_ref[:, rows] + biast_ref[...]
        lf = jnp.minimum(g, 0.0) - _log1p_exp_neg_abs(g)
        lft = jnp.minimum(gt, 0.0) - _log1p_exp_neg_abs(gt)
        bcum = _cumsum_rows(lf, tri_col)
        bcum_t = _cumsum_lanes(lft, tri_row)

        yts = []
        for h in range(N_HEADS):
            q = qkv_ref[rows, h * HD:(h + 1) * HD]
            k = qkv_ref[rows, 256 + h * HD:256 + (h + 1) * HD]
            vt = vt_ref[h, :, rows]
            src = g[:, G_I + h:G_I + h + 1] - bcum[:, G_F + h:G_F + h + 1]
            bc = bcum_t[G_F + h:G_F + h + 1, :]
            ir = gt[G_I + h:G_I + h + 1, :]
            tot = bcum_t[G_F + h:G_F + h + 1, L - 1:L]
            mprev = ms[h]

            dmt = jnp.where(s_le_l, bc + src, NEG)
            inter = bc + mprev
            mt = jnp.maximum(inter, jnp.max(dmt, axis=0, keepdims=True))
            w_intra = jnp.exp(dmt - mt)
            w_inter = jnp.exp(inter - mt)
            ct = cts[h]
            st = _dot_nt(k, q) * w_intra
            rt = w_inter * _dot_nt(ct.astype(BF16), q) + _dot(vt, st.astype(BF16))
            hout = rt[:HD] / jnp.maximum(jnp.abs(rt[HD:HD + 1]), jnp.exp(-mt))
            yts.append(hout * lax.rsqrt(jnp.mean(hout * hout, axis=0, keepdims=True) + EPS))

            gend = tot - bc + ir
            mnew = jnp.maximum(tot + mprev, jnp.max(gend, axis=-1, keepdims=True))
            kw = (kt_ref[h, :, rows].astype(F32) * jnp.exp(gend - mnew)).astype(BF16)
            cts[h] = jnp.exp(tot + mprev - mnew) * ct + _dot_nt(vt, kw)
            ms[h] = mnew

        y = jnp.concatenate(yts, axis=0).T
        y_ref[rows, :] = (y * ng_ref[...] * jax.nn.sigmoid(o_ref[rows, :])).astype(y_ref.dtype)

    for h in range(N_HEADS):
        c_ref[h] = cts[h]
        m_ref[0:1, h:h + 1] = ms[h]

    @pl.when(c == pl.num_programs(1) - 1)
    def _():
        cout_ref[...] = c_ref[...]
        mout_ref[...] = m_ref[...]


def _mlstm_prompt(dst, mqkv, mkt, mvt, mo, gates, gt, bias, biast, ng, n_seq, seq, L):
    nc = seq // L
    rspec = lambda n: pl.BlockSpec((L, n), lambda b, c: (b * nc + c, 0))
    cspec = lambda *lead: pl.BlockSpec((*lead, L), lambda b, c: (0,) * len(lead) + (b * nc + c,))
    stspec = pl.BlockSpec((None, N_HEADS, LANE, HD), lambda b, c: (b, 0, 0, 0))
    mspec = pl.BlockSpec((None, 1, LANE), lambda b, c: (b, 0, 0))
    return pl.pallas_call(
        _mlstm_prompt_kernel, grid=(n_seq, nc),
        in_specs=[_DST_SPEC, rspec(768), cspec(N_HEADS, HD), cspec(N_HEADS, LANE), rspec(256), rspec(LANE),
                  cspec(GT_ROWS), _full_spec((1, LANE)), _full_spec((GT_ROWS, 1)), _full_spec((1, 256))],
        out_specs=(rspec(256), stspec, mspec), input_output_aliases={0: 0},
        out_shape=(jax.ShapeDtypeStruct(dst.shape, dst.dtype),
                   jax.ShapeDtypeStruct((n_seq, N_HEADS, LANE, HD), F32),
                   jax.ShapeDtypeStruct((n_seq, 1, LANE), F32)),
        scratch_shapes=[pltpu.VMEM((N_HEADS, LANE, HD), F32), pltpu.VMEM((1, LANE), F32)],
        compiler_params=_cparams(("parallel", "arbitrary")), name="mlstm_prompt",
    )(dst, mqkv, mkt, mvt, mo, gates, gt, bias, biast, ng)


def _mlstm(dst, mqkv, mo, gates, c0, m0, bias, ng, row0, n_seq, seq, L):
    nc = seq // L
    blk0 = row0 // L
    rspec = lambda n: pl.BlockSpec((L, n), lambda b, c: (blk0 + b * nc + c, 0))
    cspec = pl.BlockSpec((None, N_HEADS, HD, LANE), lambda b, c: (b, 0, 0, 0))
    mspec = pl.BlockSpec((None, 1, LANE), lambda b, c: (b, 0, 0))
    return pl.pallas_call(
        _mlstm_kernel, grid=(n_seq, nc),
        in_specs=[_DST_SPEC, rspec(768), rspec(256), rspec(LANE), cspec, mspec, _full_spec((1, LANE)),
                  _full_spec((1, 256))],
        out_specs=(rspec(256), cspec, mspec), input_output_aliases={0: 0},
        out_shape=(jax.ShapeDtypeStruct(dst.shape, dst.dtype),
                   jax.ShapeDtypeStruct((n_seq, N_HEADS, HD, LANE), F32),
                   jax.ShapeDtypeStruct((n_seq, 1, LANE), F32)),
        scratch_shapes=[pltpu.VMEM((N_HEADS, HD, LANE), F32), pltpu.VMEM((1, LANE), F32)],
        compiler_params=_cparams(("parallel", "arbitrary")), name="mlstm",
    )(dst, mqkv, mo, gates, c0, m0, bias, ng)


def _mla_prompt_kernel(dst_ref, q_ref, kc_ref, vt_ref, o_ref, acc_ref, m_ref, s_ref):
    tq = q_ref.shape[2]
    tk = C_TILE
    i = pl.program_id(1)
    m_ref[...] = jnp.full_like(m_ref, NEG)
    acc_ref[...] = jnp.zeros_like(acc_ref)

    def scores(j, slot):
        start = pl.multiple_of(j * tk, tk)
        kt = kc_ref[pl.ds(start, tk), :]
        for h in range(N_HEADS):
            s_ref[slot, h] = _dot(kt, q_ref[h])

    def consume(j, slot, diag_offset=None):
        for h in range(N_HEADS):
            for c0 in range(0, tq, tk):
                cols = slice(c0, c0 + tk)
                s = s_ref[slot, h, :, cols]
                if diag_offset is not None:
                    krow = (diag_offset + lax.broadcasted_iota(jnp.int32, (tk, tk), 0)) // CHUNK
                    qcol = (c0 + lax.broadcasted_iota(jnp.int32, (tk, tk), 1)) // CHUNK
                    s = jnp.where(krow <= qcol, s, NEG)
                m_old = m_ref[h, :, cols]
                m_new = jnp.maximum(m_old, jnp.max(s, axis=0, keepdims=True))
                p = jnp.exp2(s - m_new).astype(BF16)
                acc_ref[h, :, cols] = jnp.exp2(m_old - m_new) * acc_ref[h, :, cols] + _dot(vt_ref[j, h], p)
                m_ref[h, :, cols] = m_new

    scores(0, 0)

    def pair(j):
        scores(j + 1, 1)
        consume(j, 0)
        scores(j + 2, 0)
        consume(j + 1, 1)

    def body_main(jj, carry):
        for u in range(C_UNROLL_PAIRS):
            pair(2 * (C_UNROLL_PAIRS * jj + u))
        return carry

    def body_rest(jj, carry):
        pair(2 * (C_UNROLL_PAIRS * n_main + jj))
        return carry

    n_main = i // C_UNROLL_PAIRS
    lax.fori_loop(0, n_main, body_main, 0)
    lax.fori_loop(0, i - C_UNROLL_PAIRS * n_main, body_rest, 0)
    scores(2 * i + 1, 1)
    consume(2 * i, 0, 0)
    consume(2 * i + 1, 1, tk)

    ys = []
    for h in range(N_HEADS):
        acc = acc_ref[h]
        ys.append(acc[:HD] / acc[HD:HD + 1])
    o_ref[...] = jnp.concatenate(ys, axis=0).T.astype(o_ref.dtype)


def _mla_prompt(dst, qt, kc, vt, batch, seq):
    tq = C_QTILE
    nq = seq // tq
    return pl.pallas_call(
        _mla_prompt_kernel, grid=(batch, nq),
        in_specs=[_DST_SPEC, pl.BlockSpec((N_HEADS, 256, tq), lambda b, i: (0, 0, b * nq + i)),
                  pl.BlockSpec((seq, 256), lambda b, i: (b, 0)),
                  pl.BlockSpec((seq // C_TILE, N_HEADS, V_ROWS, C_TILE), lambda b, i: (b, 0, 0, 0))],
        out_specs=pl.BlockSpec((tq, 256), lambda b, i: (b * nq + i, 0)),
        out_shape=jax.ShapeDtypeStruct(dst.shape, dst.dtype), input_output_aliases={0: 0},
        scratch_shapes=[pltpu.VMEM((N_HEADS, V_ROWS, tq), F32), pltpu.VMEM((N_HEADS, 1, tq), F32),
                        pltpu.VMEM((2, N_HEADS, C_TILE, tq), F32)],
        compiler_params=_cparams(("parallel", "arbitrary")), name="mla_prompt",
    )(dst, qt, kc, vt)


def _mla_sample_kernel(dst_ref, q_ref, ckv_ref, kr_ref, knew_ref, wuv_ref, o_ref):
    seq = q_ref.shape[1]
    qs = q_ref[...].reshape(N_HEADS * seq, 2 * LANE)
    kcache = jnp.concatenate([ckv_ref[...]] + [kr_ref[...]] * N_HEADS, axis=-1).astype(BF16)
    kall = jnp.concatenate([kcache, knew_ref[...]], axis=0)
    s = _dot_nt(qs, kall)
    m = jnp.max(s, axis=-1, keepdims=True)
    p = jnp.exp2(s - m)
    l = jnp.sum(p, axis=-1, keepdims=True)
    o = (_dot(p.astype(BF16), kall[:, :C_KV_LORA]) / l).astype(BF16)
    for h in range(N_HEADS):
        o_ref[:, h * HD:(h + 1) * HD] = _dot(o[h * seq:(h + 1) * seq], wuv_ref[h]).astype(o_ref.dtype)


def _mla_sample(dst, qabs, kc, cache_ckv, cache_kr, wuv, layer, row0, n_seq, seq):
    blk0 = row0 // seq
    past = cache_ckv.shape[2]
    return pl.pallas_call(
        _mla_sample_kernel, grid=(n_seq,),
        in_specs=[_DST_SPEC, pl.BlockSpec((N_HEADS, seq, 256), lambda b: (0, b, 0)),
                  pl.BlockSpec((None, None, past, C_KV_LORA), lambda b: (layer, b, 0, 0)),
                  pl.BlockSpec((None, None, past, C_ROPE), lambda b: (layer, b, 0, 0)),
                  pl.BlockSpec((seq, 256), lambda b: (blk0 + b, 0)),
                  _full_spec((N_HEADS, 128, HD))],
        out_specs=pl.BlockSpec((seq, 256), lambda b: (blk0 + b, 0)),
        out_shape=jax.ShapeDtypeStruct(dst.shape, dst.dtype), input_output_aliases={0: 0},
        compiler_params=_cparams(("parallel",)), name="mla_sample",
    )(dst, qabs, cache_ckv, cache_kr, kc, wuv)


def _outmlp_kernel(xp_ref, xs_ref, ya_ref, yb_ref, yc_ref, yd_ref, wout_ref, g2_ref, w1_ref, w2_ref, fg_ref,
                   op_ref, os_ref, *, final, n_prompt_tiles):
    i = pl.program_id(0)
    y = jnp.concatenate([ya_ref[...], yb_ref[...], yc_ref[...], yd_ref[...]], axis=-1)
    x1 = jnp.where(i < n_prompt_tiles, xp_ref[...], xs_ref[...]) + _dot(y, wout_ref[...])
    hm = _rms(x1, g2_ref[...]).astype(BF16)
    acc = x1
    for c in range(D_FF // D_MODEL):
        sl = slice(c * D_MODEL, (c + 1) * D_MODEL)
        hc = jnp.square(jnp.maximum(_dot(hm, w1_ref[:, sl]), 0.0)).astype(BF16)
        acc = acc + _dot(hc, w2_ref[sl, :])
    if final:
        acc = _rms(acc, fg_ref[...])

    @pl.when(i < n_prompt_tiles)
    def _():
        op_ref[...] = acc

    @pl.when(i >= n_prompt_tiles)
    def _():
        os_ref[...] = acc


def _outmlp(xp, xs, ya, yb, yc, yd, wout, g2, w1, w2, fg, layer, final, in_place):
    t = xp.shape[0] + xs.shape[0]
    tm = ROW_TILE
    npt = xp.shape[0] // tm
    row = lambda n: pl.BlockSpec((tm, n), lambda i: (i, 0))
    xspecs = _split_rows(npt, D_MODEL)
    return pl.pallas_call(
        functools.partial(_outmlp_kernel, final=final, n_prompt_tiles=npt), grid=(t // tm,),
        in_specs=[*xspecs, row(256), row(256), row(256), row(256), _layer_spec((D_MODEL, D_MODEL), layer),
                  _full_spec((1, D_MODEL)), _layer_spec((D_MODEL, D_FF), layer), _layer_spec((D_FF, D_MODEL), layer),
                  _full_spec((1, D_MODEL))],
        out_specs=xspecs,
        out_shape=(jax.ShapeDtypeStruct(xp.shape, F32), jax.ShapeDtypeStruct(xs.shape, F32)),
        input_output_aliases={0: 0, 1: 1} if in_place else {},
        compiler_params=_cparams(("arbitrary",)), name="outmlp",
    )(xp, xs, ya, yb, yc, yd, wout, g2, w1, w2, fg)


def _rot_cols(w):
    half = C_ROPE // 2
    return jnp.concatenate([-w[..., half:], w[..., :half]], axis=-1)


def _pad_lanes(v, n=LANE, at=0):
    out = jnp.zeros(v.shape[:-1] + (n,), v.dtype)
    return out.at[..., at:at + v.shape[-1]].set(v)


def _prep_w_in(w):
    sizes = (256, 256, 256, 256, 512, 4, 256, 128, 32, 256, 256, 256, 4, 4, 256)
    cuts = [0]
    for s in sizes:
        cuts.append(cuts[-1] + s)
    (a_q, a_k, a_v, b_z, b_xbc, b_dt, c_q, c_kv, c_kr, m_q, m_k, m_v, m_i, m_f, m_o) = (
        w[..., cuts[n]:cuts[n + 1]] for n in range(len(sizes)))
    kr4 = jnp.concatenate([c_kr] * N_HEADS, axis=-1)
    krr4 = jnp.concatenate([_rot_cols(c_kr)] * N_HEADS, axis=-1)
    gates = _pad_lanes(jnp.concatenate([b_dt, m_i, m_f], axis=-1))
    out = jnp.concatenate([a_q, a_k, a_v, b_z, b_xbc, c_q, c_kv, kr4, krr4, gates, m_q, m_k, m_v, m_o], axis=-1)
    assert out.shape[-1] == N_PROJ
    return out.astype(BF16)


def _prep_w_uq(w):
    d = w.shape[0]
    w4 = w.reshape(d, 256, N_HEADS, C_NOPE + C_ROPE)
    nope = w4[..., :C_NOPE].reshape(d, 256, N_HEADS * C_NOPE)
    rope = w4[..., C_NOPE:]
    return jnp.concatenate([nope, rope.reshape(d, 256, N_HEADS * C_ROPE),
                            _rot_cols(rope).reshape(d, 256, N_HEADS * C_ROPE)], axis=-1).astype(BF16)


def _prep_w_ukv(w):
    d = w.shape[0]
    w4 = w.reshape(d, C_KV_LORA, N_HEADS, C_NOPE + HD)
    w_uk = w4[..., :C_NOPE]
    w_uv = w4[..., C_NOPE:]
    wabs = jnp.zeros((d, N_HEADS * C_NOPE, N_HEADS * C_KV_LORA), w.dtype)
    for h in range(N_HEADS):
        wabs = wabs.at[:, h * C_NOPE:(h + 1) * C_NOPE, h * C_KV_LORA:(h + 1) * C_KV_LORA].set(
            jnp.swapaxes(w_uk[:, :, h, :], 1, 2))
    wuv = jnp.transpose(w_uv, (0, 2, 1, 3))
    wuv_all = w_uv.reshape(d, C_KV_LORA, N_HEADS * HD)
    return wabs.astype(BF16), wuv.astype(BF16), wuv_all.astype(BF16)


def _rope_tables(pos):
    half = C_ROPE // 2
    inv = jnp.exp(-math.log(ROPE_BASE) * jnp.arange(half, dtype=F32) / half)
    ang = pos.astype(F32)[:, None] * inv[None, :]
    reps = LANE // half
    return jnp.tile(jnp.cos(ang), (1, reps)), jnp.tile(jnp.sin(ang), (1, reps))


def _band_bias(table, q_rows):
    cols = q_rows + A_SPAN
    k = np.arange(q_rows + cols - 1)
    u = table[:, np.clip((q_rows - 1 - k) + A_SPAN, -A_REL_CLIP, A_REL_CLIP) + A_REL_CLIP].astype(F32)
    period = q_rows + cols
    u = jnp.pad(u, ((0, 0), (0, 1)))
    flat = jnp.tile(u, (1, q_rows))[:, q_rows - 1:q_rows - 1 + q_rows * (period - 1)]
    toep = flat.reshape(table.shape[0], q_rows, period - 1)[:, :, :cols]
    r = np.arange(q_rows)[:, None]
    s = np.arange(cols)[None, :] - (r // CHUNK) * CHUNK
    inband = (s >= 0) & (s < A_SPAN + CHUNK)
    return jnp.where(jnp.asarray(inband)[None], toep, NEG)


def kernel(x_prompt, x_sample, cache_attn_k, cache_attn_v, state_ssm_conv, state_ssm, cache_mla_ckv, cache_mla_kr,
           state_mlstm_c, state_mlstm_n, state_mlstm_m, norm1_g, w_in, attn_rel_bias, ssm_conv_w, ssm_conv_b,
           ssm_dt_bias, ssm_a_log, ssm_d, ssm_norm_g, mla_q_norm_g, mla_w_uq, mla_kv_norm_g, mla_w_ukv,
           mlstm_b_i, mlstm_b_f, mlstm_norm_g, w_out, norm2_g, mlp_w1, mlp_w2, final_g):
    batch, seq, _ = x_prompt.shape
    dbatch, dseq, _ = x_sample.shape
    depth = w_in.shape[0]
    past = cache_mla_ckv.shape[2]
    tp, ts = batch * seq, dbatch * dseq
    lp = REC_CHUNK_PROMPT * REC_STEP_CHUNKS
    assert seq % lp == 0
    assert dseq == CHUNK and past % CHUNK == 0 and cache_attn_k.shape[2] == A_SPAN
    assert seq % C_QTILE == 0 and tp % ROW_TILE == 0 and ts % ROW_TILE == 0

    w_in_r = _prep_w_in(w_in)
    wuq_r = _prep_w_uq(mla_w_uq)
    wabs, wuv, wuv_all = _prep_w_ukv(mla_w_ukv)
    w_out_b, w1_b, w2_b = w_out.astype(BF16), mlp_w1.astype(BF16), mlp_w2.astype(BF16)
    pos = jnp.concatenate([jnp.tile(jnp.arange(seq), batch), jnp.tile(past + jnp.arange(dseq), dbatch)])
    cos, sin = _rope_tables(pos)
    dtb = _pad_lanes(ssm_dt_bias, at=G_DT)[:, None, :]
    alog = _pad_lanes(ssm_a_log, at=G_DT)[:, None, :]
    dsk = _pad_lanes(ssm_d)[:, None, :]
    dtb_t = jnp.swapaxes(dtb[:, :, :GT_ROWS], 1, 2)
    alog_t = jnp.swapaxes(alog[:, :, :GT_ROWS], 1, 2)
    gate_bias =(_pad_lanes(mlstm_b_i, at=G_I) + _pad_lanes(mlstm_b_f, at=G_F))[:, None, :]
    gate_bias_t = jnp.swapaxes(gate_bias[:, :, :GT_ROWS], 1, 2)
    cache_k = cache_attn_k.reshape(depth, dbatch, A_SPAN, 256)
    cache_v = cache_attn_v.reshape(depth, dbatch, A_SPAN, 256)
    conv0_s = jnp.pad(state_ssm_conv, ((0, 0), (0, 0), (8 - (B_CONV - 1), 0), (0, 0)))
    caug0_s = jnp.concatenate([state_mlstm_c, state_mlstm_n[..., None],
                               jnp.zeros(state_mlstm_c.shape[:-1] + (LANE - HD - 1,), F32)], axis=-1)
    m0_s = _pad_lanes(state_mlstm_m)[:, :, None, :]

    xp, xs = x_prompt.reshape(tp, D_MODEL), x_sample.reshape(ts, D_MODEL)
    outs = [[] for _ in range(18)]
    ya, yb, yc, yd = (jnp.zeros((tp + ts, GROUP_W), BF16) for _ in range(4))
    fg = final_g[None, :]
    for l in range(depth):
        (aq, akv, bz, bxbc, qabs, qt, ckv, kr, kc, vt, mqkv, mo, gates, mkt, mvt, gt, aqt, akh, avt) = _inproj(
            xp, xs, norm1_g[l][None], w_in_r, mla_q_norm_g[l][None], mla_kv_norm_g[l][None], wuq_r, wabs,
            wuv_all, cos, sin, l)

        rel = attn_rel_bias[l] * math.log2(math.e)
        ya = _attn_a_prompt(ya, aqt, akh, avt, jnp.swapaxes(_band_bias(rel, A_QBLOCK), 1, 2), batch, seq)
        ya = _attn_a_sample(ya, aq, akv, cache_k, cache_v, _band_bias(rel, CHUNK), l, tp, dbatch, dseq)

        ssd_w = (ssm_conv_w[l], ssm_conv_b[l][None], dtb[l], alog[l], dsk[l], ssm_norm_g[l][None])
        yb, h_p = _ssd_prompt(yb, bz, bxbc, gates, gt, ssm_conv_w[l], ssm_conv_b[l][None], dtb[l], dtb_t[l], alog[l],
                              alog_t[l], dsk[l], ssm_norm_g[l][None], batch, seq, lp)
        yb, h_s = _ssd(yb, bz, bxbc, gates, conv0_s[l], state_ssm[l], *ssd_w, tp, dbatch, dseq, dseq)

        yc = _mla_prompt(yc, qt, kc, vt, batch, seq)
        yc = _mla_sample(yc, qabs, kc, cache_mla_ckv, cache_mla_kr, wuv[l], l, tp, dbatch, dseq)

        ml_w = (gate_bias[l], mlstm_norm_g[l][None])
        yd, ct_p, m_p = _mlstm_prompt(yd, mqkv, mkt, mvt, mo, gates, gt, gate_bias[l], gate_bias_t[l],
                                      mlstm_norm_g[l][None], batch, seq, lp)
        yd, c_s, m_s = _mlstm(yd, mqkv, mo, gates, caug0_s[l], m0_s[l], *ml_w, tp, dbatch, dseq, dseq)

        xp, xs = _outmlp(xp, xs, ya, yb, yc, yd, w_out_b, norm2_g[l][None], w1_b, w2_b, fg, l, l == depth - 1, l > 0)

        keep = min(A_SPAN, seq)
        tail = lambda a, n: jnp.stack([a[(b + 1) * seq - n:(b + 1) * seq] for b in range(batch)])
        akv_p = tail(akv, keep).reshape(batch, keep, 2, N_HEADS, HD)
        akv_s = akv[tp:].reshape(dbatch, dseq, 2, N_HEADS, HD)
        new = (akv_p[:, :, 0], akv_p[:, :, 1], tail(bxbc, B_CONV - 1), h_p,
               ckv[:tp].reshape(batch, seq, C_KV_LORA), kr[:tp].reshape(batch, seq, C_ROPE),
               jnp.swapaxes(ct_p[:, :, :HD, :], 2, 3), ct_p[:, :, HD, :], m_p[:, 0, :N_HEADS],
               jnp.concatenate([cache_attn_k[l][:, dseq:], akv_s[:, :, 0]], axis=1),
               jnp.concatenate([cache_attn_v[l][:, dseq:], akv_s[:, :, 1]], axis=1),
               bxbc[tp:].reshape(dbatch, dseq, B_CONV_DIM)[:, dseq - (B_CONV - 1):], h_s,
               ckv[tp:].reshape(dbatch, dseq, C_KV_LORA), kr[tp:].reshape(dbatch, dseq, C_ROPE),
               c_s[..., :HD], c_s[..., HD], m_s[:, 0, :N_HEADS])
        for dst, v in zip(outs, new):
            dst.append(v)

    y_prompt = xp.reshape(batch, seq, D_MODEL)
    y_sample = xs.reshape(dbatch, dseq, D_MODEL)
    return (y_prompt, y_sample) + tuple(jnp.stack(v) for v in outs)
```

```python
import functools
import math

import jax
import jax.numpy as jnp
import numpy as np
from jax import lax
from jax.experimental import pallas as pl
from jax.experimental.pallas import tpu as pltpu

F32 = jnp.float32
BF16 = jnp.bfloat16
EPS = 1e-6
NEG = -1e30
ROPE_BASE = 10000.0

CHUNK = 64
D_MODEL = 1024
GROUP_W = 256
D_FF = 4096
N_HEADS = 4
HD = 64
A_BAND_CHUNKS = 8
A_SPAN = A_BAND_CHUNKS * CHUNK
A_REL_CLIP = 128
B_GROUPS = 2
B_STATE = 64
B_CONV = 4
B_CONV_DIM = 512
C_NOPE = 64
C_ROPE = 32
C_KV_LORA = 128
A_QSCALE = HD ** -0.5 * math.log2(math.e)
C_SCALE = (C_NOPE + C_ROPE) ** -0.5
C_QSCALE = C_SCALE * math.log2(math.e)
V_ROWS = 80
LANE = 128
VMEM_LIMIT = 56 * 1024 * 1024

OFF_A, OFF_BZ, OFF_XBC, OFF_CQ, OFF_CKV, OFF_KR, OFF_KRR, OFF_G, OFF_MQKV, OFF_MO, N_PROJ = (
    0, 768, 1024, 1536, 1792, 1920, 2048, 2176, 2304, 3072, 3328)
G_DT, G_I, G_F = 0, 4, 8
GT_ROWS = 16

ROW_TILE = 512
A_QBLOCK = 256
C_TILE = 256
C_QTILE = 2 * C_TILE
C_UNROLL_PAIRS = 4
REC_CHUNK_PROMPT = 256
REC_STEP_CHUNKS = 4


def _cparams(sem):
    return pltpu.CompilerParams(dimension_semantics=sem, vmem_limit_bytes=VMEM_LIMIT)


def _rms(x, g):
    return x * lax.rsqrt(jnp.mean(x * x, axis=-1, keepdims=True) + EPS) * g


def _dot(a, b):
    return jnp.dot(a, b, preferred_element_type=F32)


def _dot_nt(a, b):
    return lax.dot_general(a, b, (((1,), (1,)), ((), ())), preferred_element_type=F32)


def _dot_tn(a, b):
    return lax.dot_general(a, b, (((0,), (0,)), ((), ())), preferred_element_type=F32)


def _cumsum_rows(x, tri_bf16):
    hi = x.astype(BF16)
    r1 = x - hi.astype(F32)
    mid = r1.astype(BF16)
    lo = (r1 - mid.astype(F32)).astype(BF16)
    return _dot(tri_bf16, hi) + _dot(tri_bf16, mid) + _dot(tri_bf16, lo)


def _log1p_exp_neg_abs(x):
    return jnp.log1p(jnp.exp(-jnp.abs(x)))


def _full_spec(shape):
    n = len(shape)
    return pl.BlockSpec(shape, lambda *_: (0,) * n)


_DST_SPEC = pl.BlockSpec(memory_space=pl.ANY)


def _const_spec(shape):
    n = len(shape)
    return pl.BlockSpec(shape, lambda *_: (0,) * n, pipeline_mode=pl.Buffered(1))


def _layer_spec(shape, layer):
    n = len(shape)
    return pl.BlockSpec((None,) + tuple(shape), lambda *_: (layer,) + (0,) * n, pipeline_mode=pl.Buffered(1))


def _split_rows(n_prompt_tiles, width):
    npt = n_prompt_tiles
    return (pl.BlockSpec((ROW_TILE, width), lambda i: (jnp.minimum(i, npt - 1), 0)),
            pl.BlockSpec((ROW_TILE, width), lambda i: (jnp.maximum(i - npt, 0), 0)))


def _inproj_kernel(xp_ref, xs_ref, g1_ref, w_ref, qg_ref, kvg_ref, wuq_ref, wabs_ref, wuv_ref, cos_ref, sin_ref,
                   aq_ref, akv_ref, bz_ref, bxbc_ref, qabs_ref, qt_ref, ckvp_ref, ckvs_ref, krp_ref, krs_ref, kc_ref, vt_ref,
                   mqkv_ref, mo_ref, gates_ref, mkt_ref, mvt_ref, gt_ref, aqt_ref, akh_ref, avt_ref, *,
                   n_prompt_tiles):
    tm = xp_ref.shape[0]
    is_prompt = pl.program_id(0) < n_prompt_tiles
    x = jnp.where(is_prompt, xp_ref[...], xs_ref[...])
    hn = _rms(x, g1_ref[...]).astype(BF16)

    def seg(a, b):
        return _dot(hn, w_ref[:, a:b])

    a = seg(OFF_A, OFF_BZ)
    aq = a[:, :GROUP_W] * A_QSCALE
    aq_ref[...] = aq.astype(BF16)
    akv_ref[...] = a[:, GROUP_W:]
    aqt_ref[...] = aq.T.astype(BF16)
    avt_ref[...] = a[:, 2 * GROUP_W:].T.astype(BF16)
    for h in range(N_HEADS):
        akh_ref[h] = a[:, GROUP_W + h * HD:GROUP_W + (h + 1) * HD].astype(BF16)
    bz_ref[...] = seg(OFF_BZ, OFF_XBC)
    bxbc_ref[...] = seg(OFF_XBC, OFF_CQ)

    cos = cos_ref[...]
    sin = sin_ref[...]
    cqn = _rms(seg(OFF_CQ, OFF_CKV), qg_ref[...]).astype(BF16)
    qf = _dot(cqn, wuq_ref[...])
    qrope = (qf[:, 256:384] * cos + qf[:, 384:512] * sin) * C_QSCALE
    qlat = _dot(qf[:, :256].astype(BF16), wabs_ref[...]) * C_QSCALE
    lane_head = lax.broadcasted_iota(jnp.int32, (tm, LANE), 1) // C_ROPE
    qhs = [jnp.concatenate([qlat[:, h * LANE:(h + 1) * LANE], jnp.where(lane_head == h, qrope, 0.0)], axis=-1)
           for h in range(N_HEADS)]

    ckv_kr = seg(OFF_CKV, OFF_KRR)
    krr_g = seg(OFF_KRR, OFF_MQKV)
    g = krr_g[:, LANE:]
    ckv = _rms(ckv_kr[:, :C_KV_LORA], kvg_ref[...])
    kr4 = ckv_kr[:, C_KV_LORA:] * cos + krr_g[:, :LANE] * sin
    kc_ref[...] = jnp.concatenate([ckv, kr4], axis=-1).astype(BF16)

    m = seg(OFF_MQKV, OFF_MO)
    mk = m[:, 256:512] * HD ** -0.5
    mqkv_ref[:, 0:256] = m[:, 0:256].astype(BF16)
    mqkv_ref[:, 256:512] = mk.astype(BF16)
    mqkv_ref[:, 512:768] = m[:, 512:768].astype(BF16)
    mo_ref[...] = seg(OFF_MO, N_PROJ)
    gates_ref[...] = g

    for h in range(N_HEADS):
        qt_ref[h] = qhs[h].T.astype(BF16)
    vall = _dot(ckv.astype(BF16), wuv_ref[...])
    ones_c = (lax.broadcasted_iota(jnp.int32, (V_ROWS - HD, C_TILE), 0) == 0).astype(BF16)
    for s in range(tm // C_TILE):
        vt = vall[s * C_TILE:(s + 1) * C_TILE, :].T
        for h in range(N_HEADS):
            vt_ref[s, h, 0:HD, :] = vt[h * HD:(h + 1) * HD].astype(BF16)
            vt_ref[s, h, HD:V_ROWS, :] = ones_c
    mkt = mk.T
    mvt = m[:, 512:768].T
    ones_v = (lax.broadcasted_iota(jnp.int32, (LANE - HD, tm), 0) == 0).astype(BF16)
    for h in range(N_HEADS):
        mkt_ref[h] = mkt[h * HD:(h + 1) * HD].astype(BF16)
        mvt_ref[h, 0:HD, :] = mvt[h * HD:(h + 1) * HD].astype(BF16)
        mvt_ref[h, HD:LANE, :] = ones_v
    gt_ref[...] = g.T[0:GT_ROWS]

    @pl.when(is_prompt)
    def _():
        ckvp_ref[...] = ckv
        krp_ref[...] = kr4[:, :C_ROPE]

    @pl.when(jnp.logical_not(is_prompt))
    def _():
        ckvs_ref[...] = ckv
        krs_ref[...] = kr4[:, :C_ROPE]
        for h in range(N_HEADS):
            qabs_ref[h] = qhs[h].astype(BF16)


def _inproj(xp, xs, g1, w, qg, kvg, wuq, wabs, wuv_all, cos, sin, layer):
    tp, ts = xp.shape[0], xs.shape[0]
    t = tp + ts
    tm = ROW_TILE
    npt = tp // tm
    col = lambda *lead: pl.BlockSpec((*lead, tm), lambda i: (0,) * len(lead) + (i,))
    row = lambda n: pl.BlockSpec((tm, n), lambda i: (i, 0))
    out_shape = (
        jax.ShapeDtypeStruct((t, 256), BF16),
        jax.ShapeDtypeStruct((t, 512), F32),
        jax.ShapeDtypeStruct((t, 256), F32),
        jax.ShapeDtypeStruct((t, 512), F32),
        jax.ShapeDtypeStruct((N_HEADS, ts, 256), BF16),
        jax.ShapeDtypeStruct((N_HEADS, 256, t), BF16),
        jax.ShapeDtypeStruct((tp, 128), F32),
        jax.ShapeDtypeStruct((ts, 128), F32),
        jax.ShapeDtypeStruct((tp, C_ROPE), F32),
        jax.ShapeDtypeStruct((ts, C_ROPE), F32),
        jax.ShapeDtypeStruct((t, 256), BF16),
        jax.ShapeDtypeStruct((t // C_TILE, N_HEADS, V_ROWS, C_TILE), BF16),
        jax.ShapeDtypeStruct((t, 768), BF16),
        jax.ShapeDtypeStruct((t, 256), F32),
        jax.ShapeDtypeStruct((t, LANE), F32),
        jax.ShapeDtypeStruct((N_HEADS, HD, t), BF16),
        jax.ShapeDtypeStruct((N_HEADS, LANE, t), BF16),
        jax.ShapeDtypeStruct((GT_ROWS, t), F32),
        jax.ShapeDtypeStruct((GROUP_W, t), BF16),
        jax.ShapeDtypeStruct((N_HEADS, t, HD), BF16),
        jax.ShapeDtypeStruct((GROUP_W, t), BF16),
    )
    out_specs = (
        row(256), row(512), row(256), row(512),
        pl.BlockSpec((N_HEADS, tm, 256), lambda i: (0, jnp.maximum(i - npt, 0), 0)),
        col(N_HEADS, 256),
        *_split_rows(npt, 128), *_split_rows(npt, C_ROPE), row(256),
        pl.BlockSpec((tm // C_TILE, N_HEADS, V_ROWS, C_TILE), lambda i: (i, 0, 0, 0)),
        row(768), row(256), row(LANE),
        col(N_HEADS, HD), col(N_HEADS, LANE), col(GT_ROWS),
        col(GROUP_W), pl.BlockSpec((N_HEADS, tm, HD), lambda i: (0, i, 0)), col(GROUP_W),
    )
    in_specs = [*_split_rows(npt, D_MODEL), _full_spec((1, D_MODEL)), _layer_spec((D_MODEL, N_PROJ), layer),
                _full_spec((1, 256)), _full_spec((1, 128)), _layer_spec((256, 512), layer),
                _layer_spec((256, 512), layer), _layer_spec((C_KV_LORA, 256), layer), row(LANE), row(LANE)]
    return pl.pallas_call(
        functools.partial(_inproj_kernel, n_prompt_tiles=npt), grid=(t // tm,), in_specs=in_specs,
        out_specs=out_specs, out_shape=out_shape,
        compiler_params=_cparams(("arbitrary",)), name="inproj",
    )(xp, xs, g1, w, qg, kvg, wuq, wabs, wuv_all, cos, sin)


def _band_heads(q, kb, vb, bias_ref, valid, o_ref):
    for h in range(N_HEADS):
        sl = slice(h * HD, (h + 1) * HD)
        s = _dot_nt(q[:, sl], kb[:, sl]) + bias_ref[h]
        if valid is not None:
            s = jnp.where(valid, s, NEG)
        m = jnp.max(s, axis=-1, keepdims=True)
        p = jnp.exp2(s - m)
        l = jnp.sum(p, axis=-1, keepdims=True)
        o = _dot(p.astype(BF16), vb[:, sl]) / l
        o_ref[:, sl] = o.astype(o_ref.dtype)


def _attn_a_prompt_kernel(dst_ref, qt_ref, k0_ref, k1_ref, k2_ref, v0_ref, v1_ref, v2_ref, bias_ref, o_ref, s_ref):
    i = pl.program_id(1)
    nk = 3 * A_QBLOCK
    vt = jnp.concatenate([v0_ref[...], v1_ref[...], v2_ref[...]], axis=1)
    ones = (lax.broadcasted_iota(jnp.int32, (V_ROWS - HD, nk), 0) == 0).astype(BF16)

    def scores(h):
        kh = jnp.concatenate([k0_ref[h], k1_ref[h], k2_ref[h]], axis=0)
        s_ref[h % 2] = _dot(kh, qt_ref[h * HD:(h + 1) * HD, :]) + bias_ref[h]

    def heads(masked):
        outs = []
        scores(0)
        for h in range(N_HEADS):
            if h + 1 < N_HEADS:
                scores(h + 1)
            s = s_ref[h % 2]
            if masked:
                krow = lax.broadcasted_iota(jnp.int32, (nk, A_QBLOCK), 0)
                s = jnp.where(krow >= (2 - i) * A_QBLOCK, s, NEG)
            m = jnp.max(s, axis=0, keepdims=True)
            p = jnp.exp2(s - m).astype(BF16)
            vaug = jnp.concatenate([vt[h * HD:(h + 1) * HD, :], ones], axis=0)
            o = _dot(vaug, p)
            outs.append(o[:HD] / o[HD:HD + 1])
        o_ref[...] = jnp.concatenate(outs, axis=0).T.astype(o_ref.dtype)

    @pl.when(i < 2)
    def _():
        heads(True)

    @pl.when(i >= 2)
    def _():
        heads(False)


def _attn_a_prompt(dst, aqt, akh, avt, biasm_t, batch, seq):
    nb = seq // A_QBLOCK
    qb = A_QBLOCK
    back = lambda b, i, n: b * nb + jnp.maximum(i - n, 0)
    kspec = lambda n: pl.BlockSpec((N_HEADS, qb, HD), lambda b, i: (0, back(b, i, n), 0))
    vspec = lambda n: pl.BlockSpec((GROUP_W, qb), lambda b, i: (0, back(b, i, n)))
    return pl.pallas_call(
        _attn_a_prompt_kernel, grid=(batch, nb),
        in_specs=[_DST_SPEC, pl.BlockSpec((GROUP_W, qb), lambda b, i: (0, b * nb + i)),
                  kspec(2), kspec(1), kspec(0), vspec(2), vspec(1), vspec(0), _full_spec(biasm_t.shape)],
        out_specs=pl.BlockSpec((qb, 256), lambda b, i: (b * nb + i, 0)),
        out_shape=jax.ShapeDtypeStruct(dst.shape, dst.dtype), input_output_aliases={0: 0},
        scratch_shapes=[pltpu.VMEM((2, 3 * qb, qb), F32)],
        compiler_params=_cparams(("parallel", "parallel")), name="attn_a_prompt",
    )(dst, aqt, akh, akh, akh, avt, avt, avt, biasm_t)


def _attn_a_sample_kernel(dst_ref, q_ref, ck_ref, cv_ref, kvn_ref, bias_ref, o_ref):
    kvn = kvn_ref[...]
    kb = jnp.concatenate([ck_ref[...], kvn[:, :GROUP_W]], axis=0).astype(BF16)
    vb = jnp.concatenate([cv_ref[...], kvn[:, GROUP_W:]], axis=0).astype(BF16)
    _band_heads(q_ref[...], kb, vb, bias_ref, None, o_ref)


def _attn_a_sample(dst, aq, akv, cache_k, cache_v, bias, layer, row0, n_seq, seq):
    blk0 = row0 // seq
    cspec = pl.BlockSpec((None, None, A_SPAN, 256), lambda b: (layer, b, 0, 0))
    return pl.pallas_call(
        _attn_a_sample_kernel, grid=(n_seq,),
        in_specs=[_DST_SPEC, pl.BlockSpec((seq, 256), lambda b: (blk0 + b, 0)), cspec, cspec,
                  pl.BlockSpec((seq, 512), lambda b: (blk0 + b, 0)), _full_spec(bias.shape)],
        out_specs=pl.BlockSpec((seq, 256), lambda b: (blk0 + b, 0)),
        out_shape=jax.ShapeDtypeStruct(dst.shape, dst.dtype), input_output_aliases={0: 0},
        compiler_params=_cparams(("parallel",)), name="attn_a_sample",
    )(dst, aq, cache_k, cache_v, akv, bias)


def _ssd_kernel(dst_ref, z_ref, xbc_ref, g_ref, conv0_ref, h0_ref, cw_ref, cb_ref, dtb_ref, alog_ref, dsk_ref, ng_ref,
                y_ref, hout_ref, xpad_ref, h_ref):
    L = z_ref.shape[0]
    c = pl.program_id(1)

    @pl.when(c == 0)
    def _():
        xpad_ref[...] = conv0_ref[...]
        h_ref[...] = h0_ref[...]

    xbc = xbc_ref[...]
    xfull = jnp.concatenate([xpad_ref[...], xbc], axis=0)
    conv = cb_ref[...] + xbc * cw_ref[B_CONV - 1:B_CONV, :]
    for j in range(B_CONV - 1):
        conv = conv + pltpu.roll(xfull, B_CONV - 1 - j, axis=0)[8:8 + L, :] * cw_ref[j:j + 1, :]
    xpad_ref[...] = xbc[L - 8:L, :]
    u = conv * jax.nn.sigmoid(conv)

    g = g_ref[...] + dtb_ref[...]
    dt = jnp.maximum(g, 0.0) + _log1p_exp_neg_abs(g)
    da = dt * (-jnp.exp(alog_ref[...]))
    row = lax.broadcasted_iota(jnp.int32, (L, L), 0)
    colm = lax.broadcasted_iota(jnp.int32, (L, L), 1)
    tri = row >= colm
    cs = _cumsum_rows(da, tri.astype(BF16))
    cs_t = cs.T

    ys = []
    gmat = [None] * B_GROUPS
    for h in range(N_HEADS):
        grp = h // (N_HEADS // B_GROUPS)
        xs = u[:, h * HD:(h + 1) * HD]
        bm = u[:, 256 + grp * B_STATE:256 + (grp + 1) * B_STATE].astype(BF16)
        cm = u[:, 384 + grp * B_STATE:384 + (grp + 1) * B_STATE].astype(BF16)
        if gmat[grp] is None:
            gmat[grp] = _dot_nt(cm, bm)
        csc = cs[:, G_DT + h:G_DT + h + 1]
        csr = cs_t[G_DT + h:G_DT + h + 1, :]
        tot = cs[L - 1:L, G_DT + h:G_DT + h + 1]
        dec = jnp.exp(jnp.where(tri, csc - csr, NEG))
        xdt = xs * dt[:, G_DT + h:G_DT + h + 1]
        hprev = h_ref[h]
        y = _dot((gmat[grp] * dec).astype(BF16), xdt.astype(BF16))
        y = y + _dot_nt(cm, hprev.astype(BF16)) * jnp.exp(csc)
        y = y + dsk_ref[0:1, h:h + 1] * xs
        ys.append(y)
        wend = (xdt * jnp.exp(tot - csc)).astype(BF16)
        h_ref[h] = jnp.exp(tot) * hprev + _dot_tn(wend, bm)
    y = jnp.concatenate(ys, axis=-1)
    z = z_ref[...]
    y = y * (z * jax.nn.sigmoid(z))
    y_ref[...] = _rms(y, ng_ref[...]).astype(y_ref.dtype)

    @pl.when(c == pl.num_programs(1) - 1)
    def _():
        hout_ref[...] = h_ref[...]


def _ssd(dst, bz, bxbc, gates, conv0, h0, cw, cb, dtb, alog, dsk, ng, row0, n_seq, seq, L):
    nc = seq // L
    blk0 = row0 // L
    rspec = lambda n: pl.BlockSpec((L, n), lambda b, c: (blk0 + b * nc + c, 0))
    return pl.pallas_call(
        _ssd_kernel, grid=(n_seq, nc),
        in_specs=[_DST_SPEC, rspec(256), rspec(512), rspec(LANE),
                  pl.BlockSpec((None, 8, 512), lambda b, c: (b, 0, 0)),
                  pl.BlockSpec((None, N_HEADS, HD, B_STATE), lambda b, c: (b, 0, 0, 0)),
                  _full_spec((B_CONV, 512)), _full_spec((1, 512)), _full_spec((1, LANE)), _full_spec((1, LANE)),
                  _full_spec((1, LANE)), _full_spec((1, 256))],
        out_specs=(rspec(256), pl.BlockSpec((None, N_HEADS, HD, B_STATE), lambda b, c: (b, 0, 0, 0))),
        input_output_aliases={0: 0},
        out_shape=(jax.ShapeDtypeStruct(dst.shape, dst.dtype),
                   jax.ShapeDtypeStruct((n_seq, N_HEADS, HD, B_STATE), F32)),
        scratch_shapes=[pltpu.VMEM((8, 512), F32), pltpu.VMEM((N_HEADS, HD, B_STATE), F32)],
        compiler_params=_cparams(("parallel", "arbitrary")), name="ssd",
    )(dst, bz, bxbc, gates, conv0, h0, cw, cb, dtb, alog, dsk, ng)


def _ssd_prompt_kernel(dst_ref, z_ref, xbc_ref, g_ref, gt_ref, cw_ref, cb_ref, dtb_ref, dtbt_ref, alog_ref, alogt_ref,
                       dsk_ref, ng_ref, y_ref, hout_ref, xpad_ref, h_ref):
    L = z_ref.shape[0]
    c = pl.program_id(1)

    @pl.when(c == 0)
    def _():
        xpad_ref[...] = jnp.zeros_like(xpad_ref)
        h_ref[...] = jnp.zeros_like(h_ref)

    xbc = xbc_ref[...]
    xfull = jnp.concatenate([xpad_ref[...], xbc], axis=0)
    conv = cb_ref[...] + xbc * cw_ref[B_CONV - 1:B_CONV, :]
    for j in range(B_CONV - 1):
        conv = conv + pltpu.roll(xfull, B_CONV - 1 - j, axis=0)[8:8 + L, :] * cw_ref[j:j + 1, :]
    xpad_ref[...] = xbc[L - 8:L, :]
    u_all = conv * jax.nn.sigmoid(conv)

    Lc = REC_CHUNK_PROMPT
    row = lax.broadcasted_iota(jnp.int32, (Lc, Lc), 0)
    colm = lax.broadcasted_iota(jnp.int32, (Lc, Lc), 1)
    s_le_l = row <= colm
    tri_col = (row >= colm).astype(BF16)
    tri_row = s_le_l.astype(BF16)
    hs = [h_ref[h] for h in range(N_HEADS)]

    for r0 in range(0, L, Lc):
        rows = slice(r0, r0 + Lc)
        u = u_all[rows, :]
        g = g_ref[rows, :] + dtb_ref[...]
        gt = gt_ref[:, rows] + dtbt_ref[...]
        dt = jnp.maximum(g, 0.0) + _log1p_exp_neg_abs(g)
        dtt = jnp.maximum(gt, 0.0) + _log1p_exp_neg_abs(gt)
        cs = _cumsum_rows(dt * (-jnp.exp(alog_ref[...])), tri_col)
        cs_t = _cumsum_lanes(dtt * (-jnp.exp(alogt_ref[...])), tri_row)

        xst = u[:, :GROUP_W].T
        yts = []
        gmat_t = [None] * B_GROUPS
        for h in range(N_HEADS):
            grp = h // (N_HEADS // B_GROUPS)
            bm = u[:, 256 + grp * B_STATE:256 + (grp + 1) * B_STATE].astype(BF16)
            cm = u[:, 384 + grp * B_STATE:384 + (grp + 1) * B_STATE].astype(BF16)
            if gmat_t[grp] is None:
                gmat_t[grp] = _dot_nt(bm, cm)
            csc = cs[:, G_DT + h:G_DT + h + 1]
            csr = cs_t[G_DT + h:G_DT + h + 1, :]
            tot = cs_t[G_DT + h:G_DT + h + 1, Lc - 1:Lc]
            xs_t = xst[h * HD:(h + 1) * HD]
            xdt_t = xs_t * dtt[G_DT + h:G_DT + h + 1, :]
            dec_t = jnp.exp(jnp.where(s_le_l, csr - csc, NEG))
            hprev = hs[h]
            y = _dot(xdt_t.astype(BF16), (gmat_t[grp] * dec_t).astype(BF16))
            y = y + _dot_nt(hprev.astype(BF16), cm) * jnp.exp(csr)
            yts.append(y + dsk_ref[0:1, h:h + 1] * xs_t)
            wend = (xdt_t * jnp.exp(tot - csr)).astype(BF16)
            hs[h] = jnp.exp(tot) * hprev + _dot(wend, bm)
        y = jnp.concatenate(yts, axis=0).T
        z = z_ref[rows, :]
        y = y * (z * jax.nn.sigmoid(z))
        y_ref[rows, :] = _rms(y, ng_ref[...]).astype(y_ref.dtype)

    for h in range(N_HEADS):
        h_ref[h] = hs[h]

    @pl.when(c == pl.num_programs(1) - 1)
    def _():
        hout_ref[...] = h_ref[...]


def _ssd_prompt(dst, bz, bxbc, gates, gt, cw, cb, dtb, dtbt, alog, alogt, dsk, ng, n_seq, seq, L):
    nc = seq // L
    rspec = lambda n: pl.BlockSpec((L, n), lambda b, c: (b * nc + c, 0))
    return pl.pallas_call(
        _ssd_prompt_kernel, grid=(n_seq, nc),
        in_specs=[_DST_SPEC, rspec(256), rspec(512), rspec(LANE),
                  pl.BlockSpec((GT_ROWS, L), lambda b, c: (0, b * nc + c)),
                  _full_spec((B_CONV, 512)), _full_spec((1, 512)), _full_spec((1, LANE)), _full_spec((GT_ROWS, 1)),
                  _full_spec((1, LANE)), _full_spec((GT_ROWS, 1)), _full_spec((1, LANE)), _full_spec((1, 256))],
        out_specs=(rspec(256), pl.BlockSpec((None, N_HEADS, HD, B_STATE), lambda b, c: (b, 0, 0, 0))),
        input_output_aliases={0: 0},
        out_shape=(jax.ShapeDtypeStruct(dst.shape, dst.dtype),
                   jax.ShapeDtypeStruct((n_seq, N_HEADS, HD, B_STATE), F32)),
        scratch_shapes=[pltpu.VMEM((8, 512), F32), pltpu.VMEM((N_HEADS, HD, B_STATE), F32)],
        compiler_params=_cparams(("parallel", "arbitrary")), name="ssd_prompt",
    )(dst, bz, bxbc, gates, gt, cw, cb, dtb, dtbt, alog, alogt, dsk, ng)


def _mlstm_kernel(dst_ref, qkv_ref, o_ref, g_ref, c0_ref, m0_ref, bias_ref, ng_ref,
                  y_ref, cout_ref, mout_ref, c_ref, m_ref):
    L = qkv_ref.shape[0]
    c = pl.program_id(1)

    @pl.when(c == 0)
    def _():
        c_ref[...] = c0_ref[...]
        m_ref[...] = m0_ref[...]

    g = g_ref[...] + bias_ref[...]
    lf = jnp.minimum(g, 0.0) - _log1p_exp_neg_abs(g)
    row = lax.broadcasted_iota(jnp.int32, (L, L), 0)
    colm = lax.broadcasted_iota(jnp.int32, (L, L), 1)
    tri = row >= colm
    bcum = _cumsum_rows(lf, tri.astype(BF16))
    bcum_t = bcum.T
    g_t = g.T
    unit = (lax.broadcasted_iota(jnp.int32, (L, HD), 1) == 0).astype(BF16)
    ng = ng_ref[...]
    og = o_ref[...]

    for h in range(N_HEADS):
        sl = slice(h * HD, (h + 1) * HD)
        q = qkv_ref[:, h * HD:(h + 1) * HD]
        k = qkv_ref[:, 256 + h * HD:256 + (h + 1) * HD]
        v = qkv_ref[:, 512 + h * HD:512 + (h + 1) * HD]
        vaug = jnp.concatenate([v, unit], axis=-1)
        bc = bcum[:, G_F + h:G_F + h + 1]
        br = bcum_t[G_F + h:G_F + h + 1, :]
        ic = g[:, G_I + h:G_I + h + 1]
        ir = g_t[G_I + h:G_I + h + 1, :]
        tot = bcum[L - 1:L, G_F + h:G_F + h + 1]
        mprev = m_ref[0:1, h:h + 1]

        dm = jnp.where(tri, bc - br + ir, NEG)
        inter = bc + mprev
        mt = jnp.maximum(inter, jnp.max(dm, axis=-1, keepdims=True))
        w_intra = jnp.exp(dm - mt)
        w_inter = jnp.exp(inter - mt)
        caug = c_ref[h]
        s = _dot_nt(q, k) * w_intra
        r = w_inter * _dot(q, caug.astype(BF16)) + _dot(s.astype(BF16), vaug)
        num = r[:, :HD]
        den = r[:, HD:HD + 1]
        hout = num / jnp.maximum(jnp.abs(den), jnp.exp(-mt))

        gend = tot - bc + ic
        mnew = jnp.maximum(tot + mprev, jnp.max(gend, axis=0, keepdims=True))
        kw = (k.astype(F32) * jnp.exp(gend - mnew)).astype(BF16)
        c_ref[h] = jnp.exp(tot + mprev - mnew) * caug + _dot_tn(kw, vaug)
        m_ref[0:1, h:h + 1] = mnew

        yh = _rms(hout, ng[:, sl]) * jax.nn.sigmoid(og[:, sl])
        y_ref[:, sl] = yh.astype(y_ref.dtype)

    @pl.when(c == pl.num_programs(1) - 1)
    def _():
        cout_ref[...] = c_ref[...]
        mout_ref[...] = m_ref[...]


def _cumsum_lanes(x, tri_bf16):
    hi = x.astype(BF16)
    r1 = x - hi.astype(F32)
    mid = r1.astype(BF16)
    lo = (r1 - mid.astype(F32)).astype(BF16)
    return _dot(hi, tri_bf16) + _dot(mid, tri_bf16) + _dot(lo, tri_bf16)


def _mlstm_prompt_kernel(dst_ref, qkv_ref, kt_ref, vt_ref, o_ref, g_ref, gt_ref, bias_ref, biast_ref, ng_ref,
                         y_ref, cout_ref, mout_ref, c_ref, m_ref):
    L = REC_CHUNK_PROMPT
    c = pl.program_id(1)

    @pl.when(c == 0)
    def _():
        c_ref[...] = jnp.zeros_like(c_ref)
        m_ref[...] = jnp.zeros_like(m_ref)

    row = lax.broadcasted_iota(jnp.int32, (L, L), 0)
    colm = lax.broadcasted_iota(jnp.int32, (L, L), 1)
    s_le_l = row <= colm
    tri_col = (row >= colm).astype(BF16)
    tri_row = s_le_l.astype(BF16)
    cts = [c_ref[h] for h in range(N_HEADS)]
    ms = [m_ref[0:1, h:h + 1] for h in range(N_HEADS)]

    for r0 in range(0, qkv_ref.shape[0], L):
        rows = slice(r0, r0 + L)
        g = g_ref[rows, :] + bias_ref[...]
        gt = gt_ref[:, rows] + biast_ref[...]
        lf = jnp.minimum(g, 0.0) - _log1p_exp_neg_abs(g)
        lft = jnp.minimum(gt, 0.0) - _log1p_exp_neg_abs(gt)
        bcum = _cumsum_rows(lf, tri_col)
        bcum_t = _cumsum_lanes(lft, tri_row)

        yts = []
        for h in range(N_HEADS):
            q = qkv_ref[rows, h * HD:(h + 1) * HD]
            k = qkv_ref[rows, 256 + h * HD:256 + (h + 1) * HD]
            vt = vt_ref[h, :, rows]
            src = g[:, G_I + h:G_I + h + 1] - bcum[:, G_F + h:G_F + h + 1]
            bc = bcum_t[G_F + h:G_F + h + 1, :]
            ir = gt[G_I + h:G_I + h + 1, :]
            tot = bcum_t[G_F + h:G_F + h + 1, L - 1:L]
            mprev = ms[h]

            dmt = jnp.where(s_le_l, bc + src, NEG)
            inter = bc + mprev
            mt = jnp.maximum(inter, jnp.max(dmt, axis=0, keepdims=True))
            w_intra = jnp.exp(dmt - mt)
            w_inter = jnp.exp(inter - mt)
            ct = cts[h]
            st = _dot_nt(k, q) * w_intra
            rt = w_inter * _dot_nt(ct.astype(BF16), q) + _dot(vt, st.astype(BF16))
            hout = rt[:HD] / jnp.maximum(jnp.abs(rt[HD:HD + 1]), jnp.exp(-mt))
            yts.append(hout * lax.rsqrt(jnp.mean(hout * hout, axis=0, keepdims=True) + EPS))

            gend = tot - bc + ir
            mnew = jnp.maximum(tot + mprev, jnp.max(gend, axis=-1, keepdims=True))
            kw = (kt_ref[h, :, rows].astype(F32) * jnp.exp(gend - mnew)).astype(BF16)
            cts[h] = jnp.exp(tot + mprev - mnew) * ct + _dot_nt(vt, kw)
            ms[h] = mnew

        y = jnp.concatenate(yts, axis=0).T
        y_ref[rows, :] = (y * ng_ref[...] * jax.nn.sigmoid(o_ref[rows, :])).astype(y_ref.dtype)

    for h in range(N_HEADS):
        c_ref[h] = cts[h]
        m_ref[0:1, h:h + 1] = ms[h]

    @pl.when(c == pl.num_programs(1) - 1)
    def _():
        cout_ref[...] = c_ref[...]
        mout_ref[...] = m_ref[...]


def _mlstm_prompt(dst, mqkv, mkt, mvt, mo, gates, gt, bias, biast, ng, n_seq, seq, L):
    nc = seq // L
    rspec = lambda n: pl.BlockSpec((L, n), lambda b, c: (b * nc + c, 0))
    cspec = lambda *lead: pl.BlockSpec((*lead, L), lambda b, c: (0,) * len(lead) + (b * nc + c,))
    stspec = pl.BlockSpec((None, N_HEADS, LANE, HD), lambda b, c: (b, 0, 0, 0))
    mspec = pl.BlockSpec((None, 1, LANE), lambda b, c: (b, 0, 0))
    return pl.pallas_call(
        _mlstm_prompt_kernel, grid=(n_seq, nc),
        in_specs=[_DST_SPEC, rspec(768), cspec(N_HEADS, HD), cspec(N_HEADS, LANE), rspec(256), rspec(LANE),
                  cspec(GT_ROWS), _full_spec((1, LANE)), _full_spec((GT_ROWS, 1)), _full_spec((1, 256))],
        out_specs=(rspec(256), stspec, mspec), input_output_aliases={0: 0},
        out_shape=(jax.ShapeDtypeStruct(dst.shape, dst.dtype),
                   jax.ShapeDtypeStruct((n_seq, N_HEADS, LANE, HD), F32),
                   jax.ShapeDtypeStruct((n_seq, 1, LANE), F32)),
        scratch_shapes=[pltpu.VMEM((N_HEADS, LANE, HD), F32), pltpu.VMEM((1, LANE), F32)],
        compiler_params=_cparams(("parallel", "arbitrary")), name="mlstm_prompt",
    )(dst, mqkv, mkt, mvt, mo, gates, gt, bias, biast, ng)


def _mlstm(dst, mqkv, mo, gates, c0, m0, bias, ng, row0, n_seq, seq, L):
    nc = seq // L
    blk0 = row0 // L
    rspec = lambda n: pl.BlockSpec((L, n), lambda b, c: (blk0 + b * nc + c, 0))
    cspec = pl.BlockSpec((None, N_HEADS, HD, LANE), lambda b, c: (b, 0, 0, 0))
    mspec = pl.BlockSpec((None, 1, LANE), lambda b, c: (b, 0, 0))
    return pl.pallas_call(
        _mlstm_kernel, grid=(n_seq, nc),
        in_specs=[_DST_SPEC, rspec(768), rspec(256), rspec(LANE), cspec, mspec, _full_spec((1, LANE)),
                  _full_spec((1, 256))],
        out_specs=(rspec(256), cspec, mspec), input_output_aliases={0: 0},
        out_shape=(jax.ShapeDtypeStruct(dst.shape, dst.dtype),
                   jax.ShapeDtypeStruct((n_seq, N_HEADS, HD, LANE), F32),
                   jax.ShapeDtypeStruct((n_seq, 1, LANE), F32)),
        scratch_shapes=[pltpu.VMEM((N_HEADS, HD, LANE), F32), pltpu.VMEM((1, LANE), F32)],
        compiler_params=_cparams(("parallel", "arbitrary")), name="mlstm",
    )(dst, mqkv, mo, gates, c0, m0, bias, ng)


def _mla_prompt_kernel(dst_ref, q_ref, kc_ref, vt_ref, o_ref, acc_ref, m_ref, s_ref):
    tq = q_ref.shape[2]
    tk = C_TILE
    i = pl.program_id(1)
    m_ref[...] = jnp.full_like(m_ref, NEG)
    acc_ref[...] = jnp.zeros_like(acc_ref)

    def scores(j, slot):
        start = pl.multiple_of(j * tk, tk)
        kt = kc_ref[pl.ds(start, tk), :]
        for h in range(N_HEADS):
            s_ref[slot, h] = _dot(kt, q_ref[h])

    def consume(j, slot, diag_offset=None):
        for h in range(N_HEADS):
            s = s_ref[slot, h]
            if diag_offset is not None:
                krow = (diag_offset + lax.broadcasted_iota(jnp.int32, (tk, tq), 0)) // CHUNK
                qcol = lax.broadcasted_iota(jnp.int32, (tk, tq), 1) // CHUNK
                s = jnp.where(krow <= qcol, s, NEG)
            m_old = m_ref[h]
            m_new = jnp.maximum(m_old, jnp.max(s, axis=0, keepdims=True))
            p = jnp.exp2(s - m_new).astype(BF16)
            acc_ref[h] = jnp.exp2(m_old - m_new) * acc_ref[h] + _dot(vt_ref[j, h], p)
            m_ref[h] = m_new

    scores(0, 0)

    def pair(j):
        scores(j + 1, 1)
        consume(j, 0)
        scores(j + 2, 0)
        consume(j + 1, 1)

    def body_main(jj, carry):
        for u in range(C_UNROLL_PAIRS):
            pair(2 * (C_UNROLL_PAIRS * jj + u))
        return carry

    def body_rest(jj, carry):
        pair(2 * (C_UNROLL_PAIRS * n_main + jj))
        return carry

    n_main = i // C_UNROLL_PAIRS
    lax.fori_loop(0, n_main, body_main, 0)
    lax.fori_loop(0, i - C_UNROLL_PAIRS * n_main, body_rest, 0)
    scores(2 * i + 1, 1)
    consume(2 * i, 0, 0)
    consume(2 * i + 1, 1, tk)

    ys = []
    for h in range(N_HEADS):
        acc = acc_ref[h]
        ys.append(acc[:HD] / acc[HD:HD + 1])
    o_ref[...] = jnp.concatenate(ys, axis=0).T.astype(o_ref.dtype)


def _mla_prompt(dst, qt, kc, vt, batch, seq):
    tq = C_QTILE
    nq = seq // tq
    return pl.pallas_call(
        _mla_prompt_kernel, grid=(batch, nq),
        in_specs=[_DST_SPEC, pl.BlockSpec((N_HEADS, 256, tq), lambda b, i: (0, 0, b * nq + i)),
                  pl.BlockSpec((seq, 256), lambda b, i: (b, 0)),
                  pl.BlockSpec((seq // C_TILE, N_HEADS, V_ROWS, C_TILE), lambda b, i: (b, 0, 0, 0))],
        out_specs=pl.BlockSpec((tq, 256), lambda b, i: (b * nq + i, 0)),
        out_shape=jax.ShapeDtypeStruct(dst.shape, dst.dtype), input_output_aliases={0: 0},
        scratch_shapes=[pltpu.VMEM((N_HEADS, V_ROWS, tq), F32), pltpu.VMEM((N_HEADS, 1, tq), F32),
                        pltpu.VMEM((2, N_HEADS, C_TILE, tq), F32)],
        compiler_params=_cparams(("parallel", "arbitrary")), name="mla_prompt",
    )(dst, qt, kc, vt)


def _mla_sample_kernel(dst_ref, q_ref, ckv_ref, kr_ref, knew_ref, wuv_ref, o_ref):
    seq = q_ref.shape[1]
    qs = q_ref[...].reshape(N_HEADS * seq, 2 * LANE)
    kcache = jnp.concatenate([ckv_ref[...]] + [kr_ref[...]] * N_HEADS, axis=-1).astype(BF16)
    kall = jnp.concatenate([kcache, knew_ref[...]], axis=0)
    s = _dot_nt(qs, kall)
    m = jnp.max(s, axis=-1, keepdims=True)
    p = jnp.exp2(s - m)
    l = jnp.sum(p, axis=-1, keepdims=True)
    o = (_dot(p.astype(BF16), kall[:, :C_KV_LORA]) / l).astype(BF16)
    for h in range(N_HEADS):
        o_ref[:, h * HD:(h + 1) * HD] = _dot(o[h * seq:(h + 1) * seq], wuv_ref[h]).astype(o_ref.dtype)


def _mla_sample(dst, qabs, kc, cache_ckv, cache_kr, wuv, layer, row0, n_seq, seq):
    blk0 = row0 // seq
    past = cache_ckv.shape[2]
    return pl.pallas_call(
        _mla_sample_kernel, grid=(n_seq,),
        in_specs=[_DST_SPEC, pl.BlockSpec((N_HEADS, seq, 256), lambda b: (0, b, 0)),
                  pl.BlockSpec((None, None, past, C_KV_LORA), lambda b: (layer, b, 0, 0)),
                  pl.BlockSpec((None, None, past, C_ROPE), lambda b: (layer, b, 0, 0)),
                  pl.BlockSpec((seq, 256), lambda b: (blk0 + b, 0)),
                  _full_spec((N_HEADS, 128, HD))],
        out_specs=pl.BlockSpec((seq, 256), lambda b: (blk0 + b, 0)),
        out_shape=jax.ShapeDtypeStruct(dst.shape, dst.dtype), input_output_aliases={0: 0},
        compiler_params=_cparams(("parallel",)), name="mla_sample",
    )(dst, qabs, cache_ckv, cache_kr, kc, wuv)


def _outmlp_kernel(xp_ref, xs_ref, ya_ref, yb_ref, yc_ref, yd_ref, wout_ref, g2_ref, w1_ref, w2_ref, fg_ref,
                   op_ref, os_ref, *, final, n_prompt_tiles):
    i = pl.program_id(0)
    y = jnp.concatenate([ya_ref[...], yb_ref[...], yc_ref[...], yd_ref[...]], axis=-1)
    x1 = jnp.where(i < n_prompt_tiles, xp_ref[...], xs_ref[...]) + _dot(y, wout_ref[...])
    hm = _rms(x1, g2_ref[...]).astype(BF16)
    acc = x1
    for c in range(D_FF // D_MODEL):
        sl = slice(c * D_MODEL, (c + 1) * D_MODEL)
        hc = jnp.square(jnp.maximum(_dot(hm, w1_ref[:, sl]), 0.0)).astype(BF16)
        acc = acc + _dot(hc, w2_ref[sl, :])
    if final:
        acc = _rms(acc, fg_ref[...])

    @pl.when(i < n_prompt_tiles)
    def _():
        op_ref[...] = acc

    @pl.when(i >= n_prompt_tiles)
    def _():
        os_ref[...] = acc


def _outmlp(xp, xs, ya, yb, yc, yd, wout, g2, w1, w2, fg, layer, final, in_place):
    t = xp.shape[0] + xs.shape[0]
    tm = ROW_TILE
    npt = xp.shape[0] // tm
    row = lambda n: pl.BlockSpec((tm, n), lambda i: (i, 0))
    xspecs = _split_rows(npt, D_MODEL)
    return pl.pallas_call(
        functools.partial(_outmlp_kernel, final=final, n_prompt_tiles=npt), grid=(t // tm,),
        in_specs=[*xspecs, row(256), row(256), row(256), row(256), _layer_spec((D_MODEL, D_MODEL), layer),
                  _full_spec((1, D_MODEL)), _layer_spec((D_MODEL, D_FF), layer), _layer_spec((D_FF, D_MODEL), layer),
                  _full_spec((1, D_MODEL))],
        out_specs=xspecs,
        out_shape=(jax.ShapeDtypeStruct(xp.shape, F32), jax.ShapeDtypeStruct(xs.shape, F32)),
        input_output_aliases={0: 0, 1: 1} if in_place else {},
        compiler_params=_cparams(("arbitrary",)), name="outmlp",
    )(xp, xs, ya, yb, yc, yd, wout, g2, w1, w2, fg)


def _rot_cols(w):
    half = C_ROPE // 2
    return jnp.concatenate([-w[..., half:], w[..., :half]], axis=-1)


def _pad_lanes(v, n=LANE, at=0):
    out = jnp.zeros(v.shape[:-1] + (n,), v.dtype)
    return out.at[..., at:at + v.shape[-1]].set(v)


def _prep_w_in(w):
    sizes = (256, 256, 256, 256, 512, 4, 256, 128, 32, 256, 256, 256, 4, 4, 256)
    cuts = [0]
    for s in sizes:
        cuts.append(cuts[-1] + s)
    (a_q, a_k, a_v, b_z, b_xbc, b_dt, c_q, c_kv, c_kr, m_q, m_k, m_v, m_i, m_f, m_o) = (
        w[..., cuts[n]:cuts[n + 1]] for n in range(len(sizes)))
    kr4 = jnp.concatenate([c_kr] * N_HEADS, axis=-1)
    krr4 = jnp.concatenate([_rot_cols(c_kr)] * N_HEADS, axis=-1)
    gates = _pad_lanes(jnp.concatenate([b_dt, m_i, m_f], axis=-1))
    out = jnp.concatenate([a_q, a_k, a_v, b_z, b_xbc, c_q, c_kv, kr4, krr4, gates, m_q, m_k, m_v, m_o], axis=-1)
    assert out.shape[-1] == N_PROJ
    return out.astype(BF16)


def _prep_w_uq(w):
    d = w.shape[0]
    w4 = w.reshape(d, 256, N_HEADS, C_NOPE + C_ROPE)
    nope = w4[..., :C_NOPE].reshape(d, 256, N_HEADS * C_NOPE)
    rope = w4[..., C_NOPE:]
    return jnp.concatenate([nope, rope.reshape(d, 256, N_HEADS * C_ROPE),
                            _rot_cols(rope).reshape(d, 256, N_HEADS * C_ROPE)], axis=-1).astype(BF16)


def _prep_w_ukv(w):
    d = w.shape[0]
    w4 = w.reshape(d, C_KV_LORA, N_HEADS, C_NOPE + HD)
    w_uk = w4[..., :C_NOPE]
    w_uv = w4[..., C_NOPE:]
    wabs = jnp.zeros((d, N_HEADS * C_NOPE, N_HEADS * C_KV_LORA), w.dtype)
    for h in range(N_HEADS):
        wabs = wabs.at[:, h * C_NOPE:(h + 1) * C_NOPE, h * C_KV_LORA:(h + 1) * C_KV_LORA].set(
            jnp.swapaxes(w_uk[:, :, h, :], 1, 2))
    wuv = jnp.transpose(w_uv, (0, 2, 1, 3))
    wuv_all = w_uv.reshape(d, C_KV_LORA, N_HEADS * HD)
    return wabs.astype(BF16), wuv.astype(BF16), wuv_all.astype(BF16)


def _rope_tables(pos):
    half = C_ROPE // 2
    inv = jnp.exp(-math.log(ROPE_BASE) * jnp.arange(half, dtype=F32) / half)
    ang = pos.astype(F32)[:, None] * inv[None, :]
    reps = LANE // half
    return jnp.tile(jnp.cos(ang), (1, reps)), jnp.tile(jnp.sin(ang), (1, reps))


def _band_bias(table, q_rows):
    cols = q_rows + A_SPAN
    k = np.arange(q_rows + cols - 1)
    u = table[:, np.clip((q_rows - 1 - k) + A_SPAN, -A_REL_CLIP, A_REL_CLIP) + A_REL_CLIP].astype(F32)
    period = q_rows + cols
    u = jnp.pad(u, ((0, 0), (0, 1)))
    flat = jnp.tile(u, (1, q_rows))[:, q_rows - 1:q_rows - 1 + q_rows * (period - 1)]
    toep = flat.reshape(table.shape[0], q_rows, period - 1)[:, :, :cols]
    r = np.arange(q_rows)[:, None]
    s = np.arange(cols)[None, :] - (r // CHUNK) * CHUNK
    inband = (s >= 0) & (s < A_SPAN + CHUNK)
    return jnp.where(jnp.asarray(inband)[None], toep, NEG)


def kernel(x_prompt, x_sample, cache_attn_k, cache_attn_v, state_ssm_conv, state_ssm, cache_mla_ckv, cache_mla_kr,
           state_mlstm_c, state_mlstm_n, state_mlstm_m, norm1_g, w_in, attn_rel_bias, ssm_conv_w, ssm_conv_b,
           ssm_dt_bias, ssm_a_log, ssm_d, ssm_norm_g, mla_q_norm_g, mla_w_uq, mla_kv_norm_g, mla_w_ukv,
           mlstm_b_i, mlstm_b_f, mlstm_norm_g, w_out, norm2_g, mlp_w1, mlp_w2, final_g):
    batch, seq, _ = x_prompt.shape
    dbatch, dseq, _ = x_sample.shape
    depth = w_in.shape[0]
    past = cache_mla_ckv.shape[2]
    tp, ts = batch * seq, dbatch * dseq
    lp = REC_CHUNK_PROMPT * REC_STEP_CHUNKS
    assert seq % lp == 0
    assert dseq == CHUNK and past % CHUNK == 0 and cache_attn_k.shape[2] == A_SPAN
    assert seq % C_QTILE == 0 and tp % ROW_TILE == 0 and ts % ROW_TILE == 0

    w_in_r = _prep_w_in(w_in)
    wuq_r = _prep_w_uq(mla_w_uq)
    wabs, wuv, wuv_all = _prep_w_ukv(mla_w_ukv)
    w_out_b, w1_b, w2_b = w_out.astype(BF16), mlp_w1.astype(BF16), mlp_w2.astype(BF16)
    pos = jnp.concatenate([jnp.tile(jnp.arange(seq), batch), jnp.tile(past + jnp.arange(dseq), dbatch)])
    cos, sin = _rope_tables(pos)
    dtb = _pad_lanes(ssm_dt_bias, at=G_DT)[:, None, :]
    alog = _pad_lanes(ssm_a_log, at=G_DT)[:, None, :]
    dsk = _pad_lanes(ssm_d)[:, None, :]
    dtb_t = jnp.swapaxes(dtb[:, :, :GT_ROWS], 1, 2)
    alog_t = jnp.swapaxes(alog[:, :, :GT_ROWS], 1, 2)
    gate_bias =(_pad_lanes(mlstm_b_i, at=G_I) + _pad_lanes(mlstm_b_f, at=G_F))[:, None, :]
    gate_bias_t = jnp.swapaxes(gate_bias[:, :, :GT_ROWS], 1, 2)
    cache_k = cache_attn_k.reshape(depth, dbatch, A_SPAN, 256)
    cache_v = cache_attn_v.reshape(depth, dbatch, A_SPAN, 256)
    conv0_s = jnp.pad(state_ssm_conv, ((0, 0), (0, 0), (8 - (B_CONV - 1), 0), (0, 0)))
    caug0_s = jnp.concatenate([state_mlstm_c, state_mlstm_n[..., None],
                               jnp.zeros(state_mlstm_c.shape[:-1] + (LANE - HD - 1,), F32)], axis=-1)
    m0_s = _pad_lanes(state_mlstm_m)[:, :, None, :]

    xp, xs = x_prompt.reshape(tp, D_MODEL), x_sample.reshape(ts, D_MODEL)
    outs = [[] for _ in range(18)]
    ya, yb, yc, yd = (jnp.zeros((tp + ts, GROUP_W), BF16) for _ in range(4))
    fg = final_g[None, :]
    for l in range(depth):
        (aq, akv, bz, bxbc, qabs, qt, ckv_p, ckv_s, kr_p, kr_s, kc, vt, mqkv, mo, gates, mkt, mvt, gt, aqt, akh,
         avt) = _inproj(
            xp, xs, norm1_g[l][None], w_in_r, mla_q_norm_g[l][None], mla_kv_norm_g[l][None], wuq_r, wabs,
            wuv_all, cos, sin, l)

        rel = attn_rel_bias[l] * math.log2(math.e)
        ya = _attn_a_prompt(ya, aqt, akh, avt, jnp.swapaxes(_band_bias(rel, A_QBLOCK), 1, 2), batch, seq)
        ya = _attn_a_sample(ya, aq, akv, cache_k, cache_v, _band_bias(rel, CHUNK), l, tp, dbatch, dseq)

        ssd_w = (ssm_conv_w[l], ssm_conv_b[l][None], dtb[l], alog[l], dsk[l], ssm_norm_g[l][None])
        yb, h_p = _ssd_prompt(yb, bz, bxbc, gates, gt, ssm_conv_w[l], ssm_conv_b[l][None], dtb[l], dtb_t[l], alog[l],
                              alog_t[l], dsk[l], ssm_norm_g[l][None], batch, seq, lp)
        yb, h_s = _ssd(yb, bz, bxbc, gates, conv0_s[l], state_ssm[l], *ssd_w, tp, dbatch, dseq, dseq)

        yc = _mla_prompt(yc, qt, kc, vt, batch, seq)
        yc = _mla_sample(yc, qabs, kc, cache_mla_ckv, cache_mla_kr, wuv[l], l, tp, dbatch, dseq)

        ml_w = (gate_bias[l], mlstm_norm_g[l][None])
        yd, ct_p, m_p = _mlstm_prompt(yd, mqkv, mkt, mvt, mo, gates, gt, gate_bias[l], gate_bias_t[l],
                                      mlstm_norm_g[l][None], batch, seq, lp)
        yd, c_s, m_s = _mlstm(yd, mqkv, mo, gates, caug0_s[l], m0_s[l], *ml_w, tp, dbatch, dseq, dseq)

        xp, xs = _outmlp(xp, xs, ya, yb, yc, yd, w_out_b, norm2_g[l][None], w1_b, w2_b, fg, l, l == depth - 1, l > 0)

        keep = min(A_SPAN, seq)
        tail = lambda a, n: jnp.stack([a[(b + 1) * seq - n:(b + 1) * seq] for b in range(batch)])
        akv_p = tail(akv, keep).reshape(batch, keep, 2, N_HEADS, HD)
        akv_s = akv[tp:].reshape(dbatch, dseq, 2, N_HEADS, HD)
        new = (akv_p[:, :, 0], akv_p[:, :, 1], tail(bxbc, B_CONV - 1), h_p,
               ckv_p.reshape(batch, seq, C_KV_LORA), kr_p.reshape(batch, seq, C_ROPE),
               jnp.swapaxes(ct_p[:, :, :HD, :], 2, 3), ct_p[:, :, HD, :], m_p[:, 0, :N_HEADS],
               akv_s[:, :, 0], akv_s[:, :, 1],
               bxbc[tp:].reshape(dbatch, dseq, B_CONV_DIM)[:, dseq - (B_CONV - 1):], h_s,
               ckv_s.reshape(dbatch, dseq, C_KV_LORA), kr_s.reshape(dbatch, dseq, C_ROPE),
               c_s[..., :HD], c_s[..., HD], m_s[:, 0, :N_HEADS])
        for dst, v in zip(outs, new):
            dst.append(v)

    y_prompt = xp.reshape(batch, seq, D_MODEL)
    y_sample = xs.reshape(dbatch, dseq, D_MODEL)
    res = [jnp.stack(v) for v in outs]
    res[9] = jnp.concatenate([cache_attn_k[:, :, dseq:], res[9]], axis=2)
    res[10] = jnp.concatenate([cache_attn_v[:, :, dseq:], res[10]], axis=2)
    return (y_prompt, y_sample) + tuple(res)
```

```python
import functools
import math

import jax
import jax.numpy as jnp
import numpy as np
from jax import lax
from jax.experimental import pallas as pl
from jax.experimental.pallas import tpu as pltpu

F32 = jnp.float32
BF16 = jnp.bfloat16
EPS = 1e-6
NEG = -1e30
ROPE_BASE = 10000.0

CHUNK = 64
D_MODEL = 1024
GROUP_W = 256
D_FF = 4096
N_HEADS = 4
HD = 64
A_BAND_CHUNKS = 8
A_SPAN = A_BAND_CHUNKS * CHUNK
A_REL_CLIP = 128
B_GROUPS = 2
B_STATE = 64
B_CONV = 4
B_CONV_DIM = 512
C_NOPE = 64
C_ROPE = 32
C_KV_LORA = 128
A_QSCALE = HD ** -0.5 * math.log2(math.e)
C_SCALE = (C_NOPE + C_ROPE) ** -0.5
C_QSCALE = C_SCALE * math.log2(math.e)
V_ROWS = 80
LANE = 128
VMEM_LIMIT = 56 * 1024 * 1024

OFF_A, OFF_BZ, OFF_XBC, OFF_CQ, OFF_CKV, OFF_KR, OFF_KRR, OFF_G, OFF_MQKV, OFF_MO, N_PROJ = (
    0, 768, 1024, 1536, 1792, 1920, 2048, 2176, 2304, 3072, 3328)
G_DT, G_I, G_F = 0, 4, 8
GT_ROWS = 16

ROW_TILE = 512
A_QBLOCK = 256
A_STEP_BLOCKS = 2
C_TILE = 256
C_QTILE = 2 * C_TILE
C_UNROLL_PAIRS = 4
REC_CHUNK_PROMPT = 256
REC_STEP_CHUNKS = 4


def _cparams(sem):
    return pltpu.CompilerParams(dimension_semantics=sem, vmem_limit_bytes=VMEM_LIMIT)


def _rms(x, g):
    return x * lax.rsqrt(jnp.mean(x * x, axis=-1, keepdims=True) + EPS) * g


def _dot(a, b):
    return jnp.dot(a, b, preferred_element_type=F32)


def _dot_nt(a, b):
    return lax.dot_general(a, b, (((1,), (1,)), ((), ())), preferred_element_type=F32)


def _dot_tn(a, b):
    return lax.dot_general(a, b, (((0,), (0,)), ((), ())), preferred_element_type=F32)


def _cumsum_rows(x, tri_bf16):
    hi = x.astype(BF16)
    r1 = x - hi.astype(F32)
    mid = r1.astype(BF16)
    lo = (r1 - mid.astype(F32)).astype(BF16)
    return _dot(tri_bf16, hi) + _dot(tri_bf16, mid) + _dot(tri_bf16, lo)


def _log1p_exp_neg_abs(x):
    return jnp.log1p(jnp.exp(-jnp.abs(x)))


def _full_spec(shape):
    n = len(shape)
    return pl.BlockSpec(shape, lambda *_: (0,) * n)


_DST_SPEC = pl.BlockSpec(memory_space=pl.ANY)


def _const_spec(shape):
    n = len(shape)
    return pl.BlockSpec(shape, lambda *_: (0,) * n, pipeline_mode=pl.Buffered(1))


def _layer_spec(shape, layer):
    n = len(shape)
    return pl.BlockSpec((None,) + tuple(shape), lambda *_: (layer,) + (0,) * n, pipeline_mode=pl.Buffered(1))


def _split_rows(n_prompt_tiles, width):
    npt = n_prompt_tiles
    return (pl.BlockSpec((ROW_TILE, width), lambda i: (jnp.minimum(i, npt - 1), 0)),
            pl.BlockSpec((ROW_TILE, width), lambda i: (jnp.maximum(i - npt, 0), 0)))


def _inproj_kernel(xp_ref, xs_ref, g1_ref, w_ref, qg_ref, kvg_ref, wuq_ref, wabs_ref, wuv_ref, cos_ref, sin_ref,
                   aq_ref, akv_ref, bz_ref, bxbc_ref, qabs_ref, qt_ref, ckvp_ref, ckvs_ref, krp_ref, krs_ref, kc_ref, vt_ref,
                   mqkv_ref, mo_ref, gates_ref, mkt_ref, mvt_ref, gt_ref, aqt_ref, akh_ref, avt_ref, *,
                   n_prompt_tiles):
    tm = xp_ref.shape[0]
    is_prompt = pl.program_id(0) < n_prompt_tiles
    x = jnp.where(is_prompt, xp_ref[...], xs_ref[...])
    hn = _rms(x, g1_ref[...]).astype(BF16)

    def seg(a, b):
        return _dot(hn, w_ref[:, a:b])

    a = seg(OFF_A, OFF_BZ)
    aq = a[:, :GROUP_W] * A_QSCALE
    aq_ref[...] = aq.astype(BF16)
    akv_ref[...] = a[:, GROUP_W:]
    aqt_ref[...] = aq.T.astype(BF16)
    avt_ref[...] = a[:, 2 * GROUP_W:].T.astype(BF16)
    for h in range(N_HEADS):
        akh_ref[h] = a[:, GROUP_W + h * HD:GROUP_W + (h + 1) * HD].astype(BF16)
    bz_ref[...] = seg(OFF_BZ, OFF_XBC)
    bxbc_ref[...] = seg(OFF_XBC, OFF_CQ)

    cos = cos_ref[...]
    sin = sin_ref[...]
    cqn = _rms(seg(OFF_CQ, OFF_CKV), qg_ref[...]).astype(BF16)
    qf = _dot(cqn, wuq_ref[...])
    qrope = (qf[:, 256:384] * cos + qf[:, 384:512] * sin) * C_QSCALE
    qlat = _dot(qf[:, :256].astype(BF16), wabs_ref[...]) * C_QSCALE
    lane_head = lax.broadcasted_iota(jnp.int32, (tm, LANE), 1) // C_ROPE
    qhs = [jnp.concatenate([qlat[:, h * LANE:(h + 1) * LANE], jnp.where(lane_head == h, qrope, 0.0)], axis=-1)
           for h in range(N_HEADS)]

    ckv_kr = seg(OFF_CKV, OFF_KRR)
    krr_g = seg(OFF_KRR, OFF_MQKV)
    g = krr_g[:, LANE:]
    ckv = _rms(ckv_kr[:, :C_KV_LORA], kvg_ref[...])
    kr4 = ckv_kr[:, C_KV_LORA:] * cos + krr_g[:, :LANE] * sin
    kc_ref[...] = jnp.concatenate([ckv, kr4], axis=-1).astype(BF16)

    m = seg(OFF_MQKV, OFF_MO)
    mk = m[:, 256:512] * HD ** -0.5
    mqkv_ref[:, 0:256] = m[:, 0:256].astype(BF16)
    mqkv_ref[:, 256:512] = mk.astype(BF16)
    mqkv_ref[:, 512:768] = m[:, 512:768].astype(BF16)
    mo_ref[...] = seg(OFF_MO, N_PROJ)
    gates_ref[...] = g

    for h in range(N_HEADS):
        qt_ref[h] = qhs[h].T.astype(BF16)
    vall = _dot(ckv.astype(BF16), wuv_ref[...])
    ones_c = (lax.broadcasted_iota(jnp.int32, (V_ROWS - HD, C_TILE), 0) == 0).astype(BF16)
    for s in range(tm // C_TILE):
        vt = vall[s * C_TILE:(s + 1) * C_TILE, :].T
        for h in range(N_HEADS):
            vt_ref[s, h, 0:HD, :] = vt[h * HD:(h + 1) * HD].astype(BF16)
            vt_ref[s, h, HD:V_ROWS, :] = ones_c
    mkt = mk.T
    mvt = m[:, 512:768].T
    ones_v = (lax.broadcasted_iota(jnp.int32, (LANE - HD, tm), 0) == 0).astype(BF16)
    for h in range(N_HEADS):
        mkt_ref[h] = mkt[h * HD:(h + 1) * HD].astype(BF16)
        mvt_ref[h, 0:HD, :] = mvt[h * HD:(h + 1) * HD].astype(BF16)
        mvt_ref[h, HD:LANE, :] = ones_v
    gt_ref[...] = g.T[0:GT_ROWS]

    @pl.when(is_prompt)
    def _():
        ckvp_ref[...] = ckv
        krp_ref[...] = kr4[:, :C_ROPE]

    @pl.when(jnp.logical_not(is_prompt))
    def _():
        ckvs_ref[...] = ckv
        krs_ref[...] = kr4[:, :C_ROPE]
        for h in range(N_HEADS):
            qabs_ref[h] = qhs[h].astype(BF16)


def _inproj(xp, xs, g1, w, qg, kvg, wuq, wabs, wuv_all, cos, sin, layer):
    tp, ts = xp.shape[0], xs.shape[0]
    t = tp + ts
    tm = ROW_TILE
    npt = tp // tm
    col = lambda *lead: pl.BlockSpec((*lead, tm), lambda i: (0,) * len(lead) + (i,))
    row = lambda n: pl.BlockSpec((tm, n), lambda i: (i, 0))
    out_shape = (
        jax.ShapeDtypeStruct((t, 256), BF16),
        jax.ShapeDtypeStruct((t, 512), F32),
        jax.ShapeDtypeStruct((t, 256), F32),
        jax.ShapeDtypeStruct((t, 512), F32),
        jax.ShapeDtypeStruct((N_HEADS, ts, 256), BF16),
        jax.ShapeDtypeStruct((N_HEADS, 256, t), BF16),
        jax.ShapeDtypeStruct((tp, 128), F32),
        jax.ShapeDtypeStruct((ts, 128), F32),
        jax.ShapeDtypeStruct((tp, C_ROPE), F32),
        jax.ShapeDtypeStruct((ts, C_ROPE), F32),
        jax.ShapeDtypeStruct((t, 256), BF16),
        jax.ShapeDtypeStruct((t // C_TILE, N_HEADS, V_ROWS, C_TILE), BF16),
        jax.ShapeDtypeStruct((t, 768), BF16),
        jax.ShapeDtypeStruct((t, 256), F32),
        jax.ShapeDtypeStruct((t, LANE), F32),
        jax.ShapeDtypeStruct((N_HEADS, HD, t), BF16),
        jax.ShapeDtypeStruct((N_HEADS, LANE, t), BF16),
        jax.ShapeDtypeStruct((GT_ROWS, t), F32),
        jax.ShapeDtypeStruct((GROUP_W, t), BF16),
        jax.ShapeDtypeStruct((N_HEADS, t, HD), BF16),
        jax.ShapeDtypeStruct((GROUP_W, t), BF16),
    )
    out_specs = (
        row(256), row(512), row(256), row(512),
        pl.BlockSpec((N_HEADS, tm, 256), lambda i: (0, jnp.maximum(i - npt, 0), 0)),
        col(N_HEADS, 256),
        *_split_rows(npt, 128), *_split_rows(npt, C_ROPE), row(256),
        pl.BlockSpec((tm // C_TILE, N_HEADS, V_ROWS, C_TILE), lambda i: (i, 0, 0, 0)),
        row(768), row(256), row(LANE),
        col(N_HEADS, HD), col(N_HEADS, LANE), col(GT_ROWS),
        col(GROUP_W), pl.BlockSpec((N_HEADS, tm, HD), lambda i: (0, i, 0)), col(GROUP_W),
    )
    in_specs = [*_split_rows(npt, D_MODEL), _full_spec((1, D_MODEL)), _layer_spec((D_MODEL, N_PROJ), layer),
                _full_spec((1, 256)), _full_spec((1, 128)), _layer_spec((256, 512), layer),
                _layer_spec((256, 512), layer), _layer_spec((C_KV_LORA, 256), layer), row(LANE), row(LANE)]
    return pl.pallas_call(
        functools.partial(_inproj_kernel, n_prompt_tiles=npt), grid=(t // tm,), in_specs=in_specs,
        out_specs=out_specs, out_shape=out_shape,
        compiler_params=_cparams(("arbitrary",)), name="inproj",
    )(xp, xs, g1, w, qg, kvg, wuq, wabs, wuv_all, cos, sin)


def _band_heads(q, kb, vb, bias_ref, valid, o_ref):
    for h in range(N_HEADS):
        sl = slice(h * HD, (h + 1) * HD)
        s = _dot_nt(q[:, sl], kb[:, sl]) + bias_ref[h]
        if valid is not None:
            s = jnp.where(valid, s, NEG)
        m = jnp.max(s, axis=-1, keepdims=True)
        p = jnp.exp2(s - m)
        l = jnp.sum(p, axis=-1, keepdims=True)
        o = _dot(p.astype(BF16), vb[:, sl]) / l
        o_ref[:, sl] = o.astype(o_ref.dtype)


def _attn_a_prompt_kernel(dst_ref, qt_ref, k0_ref, k1_ref, k2_ref, k3_ref, v0_ref, v1_ref, v2_ref, v3_ref, bias_ref,
                          o_ref, s_ref):
    i = pl.program_id(1)
    qb = A_QBLOCK
    nk = 3 * qb
    krefs = (k0_ref, k1_ref, k2_ref, k3_ref)
    vrefs = (v0_ref, v1_ref, v2_ref, v3_ref)
    vts = [jnp.concatenate([vrefs[sub + d][...] for d in range(3)], axis=1) for sub in range(A_STEP_BLOCKS)]
    ones = (lax.broadcasted_iota(jnp.int32, (V_ROWS - HD, nk), 0) == 0).astype(BF16)
    items = [(sub, h) for sub in range(A_STEP_BLOCKS) for h in range(N_HEADS)]

    def scores(n):
        sub, h = items[n]
        kh = jnp.concatenate([krefs[sub + d][h] for d in range(3)], axis=0)
        q = qt_ref[h * HD:(h + 1) * HD, sub * qb:(sub + 1) * qb]
        s_ref[n % 2] = _dot(kh, q) + bias_ref[h]

    def run(first_step):
        outs = [[] for _ in range(A_STEP_BLOCKS)]
        scores(0)
        for n, (sub, h) in enumerate(items):
            if n + 1 < len(items):
                scores(n + 1)
            s = s_ref[n % 2]
            if first_step:
                krow = lax.broadcasted_iota(jnp.int32, (nk, qb), 0)
                s = jnp.where(krow >= (2 - sub) * qb, s, NEG)
            m = jnp.max(s, axis=0, keepdims=True)
            p = jnp.exp2(s - m).astype(BF16)
            vaug = jnp.concatenate([vts[sub][h * HD:(h + 1) * HD, :], ones], axis=0)
            o = _dot(vaug, p)
            outs[sub].append(o[:HD] / o[HD:HD + 1])
        for sub in range(A_STEP_BLOCKS):
            o_ref[sub * qb:(sub + 1) * qb, :] = jnp.concatenate(outs[sub], axis=0).T.astype(o_ref.dtype)

    @pl.when(i == 0)
    def _():
        run(True)

    @pl.when(i > 0)
    def _():
        run(False)


def _attn_a_prompt(dst, aqt, akh, avt, biasm_t, batch, seq):
    qb = A_QBLOCK
    nb = seq // qb
    ns = nb // A_STEP_BLOCKS
    blk = lambda b, i, d: b * nb + jnp.maximum(A_STEP_BLOCKS * i + d, 0)
    kspec = lambda d: pl.BlockSpec((N_HEADS, qb, HD), lambda b, i: (0, blk(b, i, d), 0))
    vspec = lambda d: pl.BlockSpec((GROUP_W, qb), lambda b, i: (0, blk(b, i, d)))
    offs = (-2, -1, 0, 1)
    return pl.pallas_call(
        _attn_a_prompt_kernel, grid=(batch, ns),
        in_specs=[_DST_SPEC, pl.BlockSpec((GROUP_W, A_STEP_BLOCKS * qb), lambda b, i: (0, b * ns + i)),
                  *[kspec(d) for d in offs], *[vspec(d) for d in offs], _full_spec(biasm_t.shape)],
        out_specs=pl.BlockSpec((A_STEP_BLOCKS * qb, 256), lambda b, i: (b * ns + i, 0)),
        out_shape=jax.ShapeDtypeStruct(dst.shape, dst.dtype), input_output_aliases={0: 0},
        scratch_shapes=[pltpu.VMEM((2, 3 * qb, qb), F32)],
        compiler_params=_cparams(("parallel", "parallel")), name="attn_a_prompt",
    )(dst, aqt, *[akh] * 4, *[avt] * 4, biasm_t)


def _attn_a_sample_kernel(dst_ref, q_ref, ck_ref, cv_ref, kvn_ref, bias_ref, o_ref):
    kvn = kvn_ref[...]
    kb = jnp.concatenate([ck_ref[...], kvn[:, :GROUP_W]], axis=0).astype(BF16)
    vb = jnp.concatenate([cv_ref[...], kvn[:, GROUP_W:]], axis=0).astype(BF16)
    _band_heads(q_ref[...], kb, vb, bias_ref, None, o_ref)


def _attn_a_sample(dst, aq, akv, cache_k, cache_v, bias, layer, row0, n_seq, seq):
    blk0 = row0 // seq
    cspec = pl.BlockSpec((None, None, A_SPAN, 256), lambda b: (layer, b, 0, 0))
    return pl.pallas_call(
        _attn_a_sample_kernel, grid=(n_seq,),
        in_specs=[_DST_SPEC, pl.BlockSpec((seq, 256), lambda b: (blk0 + b, 0)), cspec, cspec,
                  pl.BlockSpec((seq, 512), lambda b: (blk0 + b, 0)), _full_spec(bias.shape)],
        out_specs=pl.BlockSpec((seq, 256), lambda b: (blk0 + b, 0)),
        out_shape=jax.ShapeDtypeStruct(dst.shape, dst.dtype), input_output_aliases={0: 0},
        compiler_params=_cparams(("parallel",)), name="attn_a_sample",
    )(dst, aq, cache_k, cache_v, akv, bias)


def _ssd_kernel(dst_ref, z_ref, xbc_ref, g_ref, conv0_ref, h0_ref, cw_ref, cb_ref, dtb_ref, alog_ref, dsk_ref, ng_ref,
                y_ref, hout_ref, xpad_ref, h_ref):
    L = z_ref.shape[0]
    c = pl.program_id(1)

    @pl.when(c == 0)
    def _():
        xpad_ref[...] = conv0_ref[...]
        h_ref[...] = h0_ref[...]

    xbc = xbc_ref[...]
    xfull = jnp.concatenate([xpad_ref[...], xbc], axis=0)
    conv = cb_ref[...] + xbc * cw_ref[B_CONV - 1:B_CONV, :]
    for j in range(B_CONV - 1):
        conv = conv + pltpu.roll(xfull, B_CONV - 1 - j, axis=0)[8:8 + L, :] * cw_ref[j:j + 1, :]
    xpad_ref[...] = xbc[L - 8:L, :]
    u = conv * jax.nn.sigmoid(conv)

    g = g_ref[...] + dtb_ref[...]
    dt = jnp.maximum(g, 0.0) + _log1p_exp_neg_abs(g)
    da = dt * (-jnp.exp(alog_ref[...]))
    row = lax.broadcasted_iota(jnp.int32, (L, L), 0)
    colm = lax.broadcasted_iota(jnp.int32, (L, L), 1)
    tri = row >= colm
    cs = _cumsum_rows(da, tri.astype(BF16))
    cs_t = cs.T

    ys = []
    gmat = [None] * B_GROUPS
    for h in range(N_HEADS):
        grp = h // (N_HEADS // B_GROUPS)
        xs = u[:, h * HD:(h + 1) * HD]
        bm = u[:, 256 + grp * B_STATE:256 + (grp + 1) * B_STATE].astype(BF16)
        cm = u[:, 384 + grp * B_STATE:384 + (grp + 1) * B_STATE].astype(BF16)
        if gmat[grp] is None:
            gmat[grp] = _dot_nt(cm, bm)
        csc = cs[:, G_DT + h:G_DT + h + 1]
        csr = cs_t[G_DT + h:G_DT + h + 1, :]
        tot = cs[L - 1:L, G_DT + h:G_DT + h + 1]
        dec = jnp.exp(jnp.where(tri, csc - csr, NEG))
        xdt = xs * dt[:, G_DT + h:G_DT + h + 1]
        hprev = h_ref[h]
        y = _dot((gmat[grp] * dec).astype(BF16), xdt.astype(BF16))
        y = y + _dot_nt(cm, hprev.astype(BF16)) * jnp.exp(csc)
        y = y + dsk_ref[0:1, h:h + 1] * xs
        ys.append(y)
        wend = (xdt * jnp.exp(tot - csc)).astype(BF16)
        h_ref[h] = jnp.exp(tot) * hprev + _dot_tn(wend, bm)
    y = jnp.concatenate(ys, axis=-1)
    z = z_ref[...]
    y = y * (z * jax.nn.sigmoid(z))
    y_ref[...] = _rms(y, ng_ref[...]).astype(y_ref.dtype)

    @pl.when(c == pl.num_programs(1) - 1)
    def _():
        hout_ref[...] = h_ref[...]


def _ssd(dst, bz, bxbc, gates, conv0, h0, cw, cb, dtb, alog, dsk, ng, row0, n_seq, seq, L):
    nc = seq // L
    blk0 = row0 // L
    rspec = lambda n: pl.BlockSpec((L, n), lambda b, c: (blk0 + b * nc + c, 0))
    return pl.pallas_call(
        _ssd_kernel, grid=(n_seq, nc),
        in_specs=[_DST_SPEC, rspec(256), rspec(512), rspec(LANE),
                  pl.BlockSpec((None, 8, 512), lambda b, c: (b, 0, 0)),
                  pl.BlockSpec((None, N_HEADS, HD, B_STATE), lambda b, c: (b, 0, 0, 0)),
                  _full_spec((B_CONV, 512)), _full_spec((1, 512)), _full_spec((1, LANE)), _full_spec((1, LANE)),
                  _full_spec((1, LANE)), _full_spec((1, 256))],
        out_specs=(rspec(256), pl.BlockSpec((None, N_HEADS, HD, B_STATE), lambda b, c: (b, 0, 0, 0))),
        input_output_aliases={0: 0},
        out_shape=(jax.ShapeDtypeStruct(dst.shape, dst.dtype),
                   jax.ShapeDtypeStruct((n_seq, N_HEADS, HD, B_STATE), F32)),
        scratch_shapes=[pltpu.VMEM((8, 512), F32), pltpu.VMEM((N_HEADS, HD, B_STATE), F32)],
        compiler_params=_cparams(("parallel", "arbitrary")), name="ssd",
    )(dst, bz, bxbc, gates, conv0, h0, cw, cb, dtb, alog, dsk, ng)


def _ssd_prompt_kernel(dst_ref, z_ref, xbc_ref, g_ref, gt_ref, cw_ref, cb_ref, dtb_ref, dtbt_ref, alog_ref, alogt_ref,
                       dsk_ref, ng_ref, y_ref, hout_ref, xpad_ref, h_ref):
    L = z_ref.shape[0]
    c = pl.program_id(1)

    @pl.when(c == 0)
    def _():
        xpad_ref[...] = jnp.zeros_like(xpad_ref)
        h_ref[...] = jnp.zeros_like(h_ref)

    xbc = xbc_ref[...]
    xfull = jnp.concatenate([xpad_ref[...], xbc], axis=0)
    conv = cb_ref[...] + xbc * cw_ref[B_CONV - 1:B_CONV, :]
    for j in range(B_CONV - 1):
        conv = conv + pltpu.roll(xfull, B_CONV - 1 - j, axis=0)[8:8 + L, :] * cw_ref[j:j + 1, :]
    xpad_ref[...] = xbc[L - 8:L, :]
    u_all = conv * jax.nn.sigmoid(conv)

    Lc = REC_CHUNK_PROMPT
    row = lax.broadcasted_iota(jnp.int32, (Lc, Lc), 0)
    colm = lax.broadcasted_iota(jnp.int32, (Lc, Lc), 1)
    s_le_l = row <= colm
    tri_col = (row >= colm).astype(BF16)
    tri_row = s_le_l.astype(BF16)
    hs = [h_ref[h] for h in range(N_HEADS)]

    for r0 in range(0, L, Lc):
        rows = slice(r0, r0 + Lc)
        u = u_all[rows, :]
        g = g_ref[rows, :] + dtb_ref[...]
        gt = gt_ref[:, rows] + dtbt_ref[...]
        dt = jnp.maximum(g, 0.0) + _log1p_exp_neg_abs(g)
        dtt = jnp.maximum(gt, 0.0) + _log1p_exp_neg_abs(gt)
        cs = _cumsum_rows(dt * (-jnp.exp(alog_ref[...])), tri_col)
        cs_t = _cumsum_lanes(dtt * (-jnp.exp(alogt_ref[...])), tri_row)

        xst = u[:, :GROUP_W].T
        yts = []
        gmat_t = [None] * B_GROUPS
        for h in range(N_HEADS):
            grp = h // (N_HEADS // B_GROUPS)
            bm = u[:, 256 + grp * B_STATE:256 + (grp + 1) * B_STATE].astype(BF16)
            cm = u[:, 384 + grp * B_STATE:384 + (grp + 1) * B_STATE].astype(BF16)
            if gmat_t[grp] is None:
                gmat_t[grp] = _dot_nt(bm, cm)
            csc = cs[:, G_DT + h:G_DT + h + 1]
            csr = cs_t[G_DT + h:G_DT + h + 1, :]
            tot = cs_t[G_DT + h:G_DT + h + 1, Lc - 1:Lc]
            xs_t = xst[h * HD:(h + 1) * HD]
            xdt_t = xs_t * dtt[G_DT + h:G_DT + h + 1, :]
            dec_t = jnp.exp(jnp.where(s_le_l, csr - csc, NEG))
            hprev = hs[h]
            y = _dot(xdt_t.astype(BF16), (gmat_t[grp] * dec_t).astype(BF16))
            y = y + _dot_nt(hprev.astype(BF16), cm) * jnp.exp(csr)
            yts.append(y + dsk_ref[0:1, h:h + 1] * xs_t)
            wend = (xdt_t * jnp.exp(tot - csr)).astype(BF16)
            hs[h] = jnp.exp(tot) * hprev + _dot(wend, bm)
        y = jnp.concatenate(yts, axis=0).T
        z = z_ref[rows, :]
        y = y * (z * jax.nn.sigmoid(z))
        y_ref[rows, :] = _rms(y, ng_ref[...]).astype(y_ref.dtype)

    for h in range(N_HEADS):
        h_ref[h] = hs[h]

    @pl.when(c == pl.num_programs(1) - 1)
    def _():
        hout_ref[...] = h_ref[...]


def _ssd_prompt(dst, bz, bxbc, gates, gt, cw, cb, dtb, dtbt, alog, alogt, dsk, ng, n_seq, seq, L):
    nc = seq // L
    rspec = lambda n: pl.BlockSpec((L, n), lambda b, c: (b * nc + c, 0))
    return pl.pallas_call(
        _ssd_prompt_kernel, grid=(n_seq, nc),
        in_specs=[_DST_SPEC, rspec(256), rspec(512), rspec(LANE),
                  pl.BlockSpec((GT_ROWS, L), lambda b, c: (0, b * nc + c)),
                  _full_spec((B_CONV, 512)), _full_spec((1, 512)), _full_spec((1, LANE)), _full_spec((GT_ROWS, 1)),
                  _full_spec((1, LANE)), _full_spec((GT_ROWS, 1)), _full_spec((1, LANE)), _full_spec((1, 256))],
        out_specs=(rspec(256), pl.BlockSpec((None, N_HEADS, HD, B_STATE), lambda b, c: (b, 0, 0, 0))),
        input_output_aliases={0: 0},
        out_shape=(jax.ShapeDtypeStruct(dst.shape, dst.dtype),
                   jax.ShapeDtypeStruct((n_seq, N_HEADS, HD, B_STATE), F32)),
        scratch_shapes=[pltpu.VMEM((8, 512), F32), pltpu.VMEM((N_HEADS, HD, B_STATE), F32)],
        compiler_params=_cparams(("parallel", "arbitrary")), name="ssd_prompt",
    )(dst, bz, bxbc, gates, gt, cw, cb, dtb, dtbt, alog, alogt, dsk, ng)


def _mlstm_kernel(dst_ref, qkv_ref, o_ref, g_ref, c0_ref, m0_ref, bias_ref, ng_ref,
                  y_ref, cout_ref, mout_ref, c_ref, m_ref):
    L = qkv_ref.shape[0]
    c = pl.program_id(1)

    @pl.when(c == 0)
    def _():
        c_ref[...] = c0_ref[...]
        m_ref[...] = m0_ref[...]

    g = g_ref[...] + bias_ref[...]
    lf = jnp.minimum(g, 0.0) - _log1p_exp_neg_abs(g)
    row = lax.broadcasted_iota(jnp.int32, (L, L), 0)
    colm = lax.broadcasted_iota(jnp.int32, (L, L), 1)
    tri = row >= colm
    bcum = _cumsum_rows(lf, tri.astype(BF16))
    bcum_t = bcum.T
    g_t = g.T
    unit = (lax.broadcasted_iota(jnp.int32, (L, HD), 1) == 0).astype(BF16)
    ng = ng_ref[...]
    og = o_ref[...]

    for h in range(N_HEADS):
        sl = slice(h * HD, (h + 1) * HD)
        q = qkv_ref[:, h * HD:(h + 1) * HD]
        k = qkv_ref[:, 256 + h * HD:256 + (h + 1) * HD]
        v = qkv_ref[:, 512 + h * HD:512 + (h + 1) * HD]
        vaug = jnp.concatenate([v, unit], axis=-1)
        bc = bcum[:, G_F + h:G_F + h + 1]
        br = bcum_t[G_F + h:G_F + h + 1, :]
        ic = g[:, G_I + h:G_I + h + 1]
        ir = g_t[G_I + h:G_I + h + 1, :]
        tot = bcum[L - 1:L, G_F + h:G_F + h + 1]
        mprev = m_ref[0:1, h:h + 1]

        dm = jnp.where(tri, bc - br + ir, NEG)
        inter = bc + mprev
        mt = jnp.maximum(inter, jnp.max(dm, axis=-1, keepdims=True))
        w_intra = jnp.exp(dm - mt)
        w_inter = jnp.exp(inter - mt)
        caug = c_ref[h]
        s = _dot_nt(q, k) * w_intra
        r = w_inter * _dot(q, caug.astype(BF16)) + _dot(s.astype(BF16), vaug)
        num = r[:, :HD]
        den = r[:, HD:HD + 1]
        hout = num / jnp.maximum(jnp.abs(den), jnp.exp(-mt))

        gend = tot - bc + ic
        mnew = jnp.maximum(tot + mprev, jnp.max(gend, axis=0, keepdims=True))
        kw = (k.astype(F32) * jnp.exp(gend - mnew)).astype(BF16)
        c_ref[h] = jnp.exp(tot + mprev - mnew) * caug + _dot_tn(kw, vaug)
        m_ref[0:1, h:h + 1] = mnew

        yh = _rms(hout, ng[:, sl]) * jax.nn.sigmoid(og[:, sl])
        y_ref[:, sl] = yh.astype(y_ref.dtype)

    @pl.when(c == pl.num_programs(1) - 1)
    def _():
        cout_ref[...] = c_ref[...]
        mout_ref[...] = m_ref[...]


def _cumsum_lanes(x, tri_bf16):
    hi = x.astype(BF16)
    r1 = x - hi.astype(F32)
    mid = r1.astype(BF16)
    lo = (r1 - mid.astype(F32)).astype(BF16)
    return _dot(hi, tri_bf16) + _dot(mid, tri_bf16) + _dot(lo, tri_bf16)


def _mlstm_prompt_kernel(dst_ref, qkv_ref, kt_ref, vt_ref, o_ref, g_ref, gt_ref, bias_ref, biast_ref, ng_ref,
                         y_ref, cout_ref, mout_ref, c_ref, m_ref):
    L = REC_CHUNK_PROMPT
    c = pl.program_id(1)

    @pl.when(c == 0)
    def _():
        c_ref[...] = jnp.zeros_like(c_ref)
        m_ref[...] = jnp.zeros_like(m_ref)

    row = lax.broadcasted_iota(jnp.int32, (L, L), 0)
    colm = lax.broadcasted_iota(jnp.int32, (L, L), 1)
    s_le_l = row <= colm
    tri_col = (row >= colm).astype(BF16)
    tri_row = s_le_l.astype(BF16)
    cts = [c_ref[h] for h in range(N_HEADS)]
    ms = [m_ref[0:1, h:h + 1] for h in range(N_HEADS)]

    for r0 in range(0, qkv_ref.shape[0], L):
        rows = slice(r0, r0 + L)
        g = g_ref[rows, :] + bias_ref[...]
        gt = gt_ref[:, rows] + biast_ref[...]
        lf = jnp.minimum(g, 0.0) - _log1p_exp_neg_abs(g)
        lft = jnp.minimum(gt, 0.0) - _log1p_exp_neg_abs(gt)
        bcum = _cumsum_rows(lf, tri_col)
        bcum_t = _cumsum_lanes(lft, tri_row)

        yts = []
        for h in range(N_HEADS):
            q = qkv_ref[rows, h * HD:(h + 1) * HD]
            k = qkv_ref[rows, 256 + h * HD:256 + (h + 1) * HD]
            vt = vt_ref[h, :, rows]
            src = g[:, G_I + h:G_I + h + 1] - bcum[:, G_F + h:G_F + h + 1]
            bc = bcum_t[G_F + h:G_F + h + 1, :]
            ir = gt[G_I + h:G_I + h + 1, :]
            tot = bcum_t[G_F + h:G_F + h + 1, L - 1:L]
            mprev = ms[h]

            dmt = jnp.where(s_le_l, bc + src, NEG)
            inter = bc + mprev
            mt = jnp.maximum(inter, jnp.max(dmt, axis=0, keepdims=True))
            w_intra = jnp.exp(dmt - mt)
            w_inter = jnp.exp(inter - mt)
            ct = cts[h]
            st = _dot_nt(k, q) * w_intra
            rt = w_inter * _dot_nt(ct.astype(BF16), q) + _dot(vt, st.astype(BF16))
            hout = rt[:HD] / jnp.maximum(jnp.abs(rt[HD:HD + 1]), jnp.exp(-mt))
            yts.append(hout * lax.rsqrt(jnp.mean(hout * hout, axis=0, keepdims=True) + EPS))

            gend = tot - bc + ir
            mnew = jnp.maximum(tot + mprev, jnp.max(gend, axis=-1, keepdims=True))
            kw = (kt_ref[h, :, rows].astype(F32) * jnp.exp(gend - mnew)).astype(BF16)
            cts[h] = jnp.exp(tot + mprev - mnew) * ct + _dot_nt(vt, kw)
            ms[h] = mnew

        y = jnp.concatenate(yts, axis=0).T
        y_ref[rows, :] = (y * ng_ref[...] * jax.nn.sigmoid(o_ref[rows, :])).astype(y_ref.dtype)

    for h in range(N_HEADS):
        c_ref[h] = cts[h]
        m_ref[0:1, h:h + 1] = ms[h]

    @pl.when(c == pl.num_programs(1) - 1)
    def _():
        cout_ref[...] = c_ref[...]
        mout_ref[...] = m_ref[...]


def _mlstm_prompt(dst, mqkv, mkt, mvt, mo, gates, gt, bias, biast, ng, n_seq, seq, L):
    nc = seq // L
    rspec = lambda n: pl.BlockSpec((L, n), lambda b, c: (b * nc + c, 0))
    cspec = lambda *lead: pl.BlockSpec((*lead, L), lambda b, c: (0,) * len(lead) + (b * nc + c,))
    stspec = pl.BlockSpec((None, N_HEADS, LANE, HD), lambda b, c: (b, 0, 0, 0))
    mspec = pl.BlockSpec((None, 1, LANE), lambda b, c: (b, 0, 0))
    return pl.pallas_call(
        _mlstm_prompt_kernel, grid=(n_seq, nc),
        in_specs=[_DST_SPEC, rspec(768), cspec(N_HEADS, HD), cspec(N_HEADS, LANE), rspec(256), rspec(LANE),
                  cspec(GT_ROWS), _full_spec((1, LANE)), _full_spec((GT_ROWS, 1)), _full_spec((1, 256))],
        out_specs=(rspec(256), stspec, mspec), input_output_aliases={0: 0},
        out_shape=(jax.ShapeDtypeStruct(dst.shape, dst.dtype),
                   jax.ShapeDtypeStruct((n_seq, N_HEADS, LANE, HD), F32),
                   jax.ShapeDtypeStruct((n_seq, 1, LANE), F32)),
        scratch_shapes=[pltpu.VMEM((N_HEADS, LANE, HD), F32), pltpu.VMEM((1, LANE), F32)],
        compiler_params=_cparams(("parallel", "arbitrary")), name="mlstm_prompt",
    )(dst, mqkv, mkt, mvt, mo, gates, gt, bias, biast, ng)


def _mlstm(dst, mqkv, mo, gates, c0, m0, bias, ng, row0, n_seq, seq, L):
    nc = seq // L
    blk0 = row0 // L
    rspec = lambda n: pl.BlockSpec((L, n), lambda b, c: (blk0 + b * nc + c, 0))
    cspec = pl.BlockSpec((None, N_HEADS, HD, LANE), lambda b, c: (b, 0, 0, 0))
    mspec = pl.BlockSpec((None, 1, LANE), lambda b, c: (b, 0, 0))
    return pl.pallas_call(
        _mlstm_kernel, grid=(n_seq, nc),
        in_specs=[_DST_SPEC, rspec(768), rspec(256), rspec(LANE), cspec, mspec, _full_spec((1, LANE)),
                  _full_spec((1, 256))],
        out_specs=(rspec(256), cspec, mspec), input_output_aliases={0: 0},
        out_shape=(jax.ShapeDtypeStruct(dst.shape, dst.dtype),
                   jax.ShapeDtypeStruct((n_seq, N_HEADS, HD, LANE), F32),
                   jax.ShapeDtypeStruct((n_seq, 1, LANE), F32)),
        scratch_shapes=[pltpu.VMEM((N_HEADS, HD, LANE), F32), pltpu.VMEM((1, LANE), F32)],
        compiler_params=_cparams(("parallel", "arbitrary")), name="mlstm",
    )(dst, mqkv, mo, gates, c0, m0, bias, ng)


def _mla_prompt_kernel(dst_ref, q_ref, kc_ref, vt_ref, o_ref, acc_ref, m_ref, s_ref):
    tq = q_ref.shape[2]
    tk = C_TILE
    i = pl.program_id(1)
    m_ref[...] = jnp.full_like(m_ref, NEG)
    acc_ref[...] = jnp.zeros_like(acc_ref)

    def scores(j, slot, c0=0):
        start = pl.multiple_of(j * tk, tk)
        kt = kc_ref[pl.ds(start, tk), :]
        for h in range(N_HEADS):
            s_ref[slot, h, :, c0:] = _dot(kt, q_ref[h, :, c0:])

    def consume(j, slot, diag_offset=None):
        c0 = 0 if diag_offset is None else diag_offset
        for h in range(N_HEADS):
            s = s_ref[slot, h, :, c0:]
            if diag_offset is not None:
                krow = lax.broadcasted_iota(jnp.int32, (tk, tq - c0), 0) // CHUNK
                qcol = lax.broadcasted_iota(jnp.int32, (tk, tq - c0), 1) // CHUNK
                s = jnp.where(krow <= qcol, s, NEG)
            m_old = m_ref[h, :, c0:]
            m_new = jnp.maximum(m_old, jnp.max(s, axis=0, keepdims=True))
            p = jnp.exp2(s - m_new).astype(BF16)
            acc_ref[h, :, c0:] = jnp.exp2(m_old - m_new) * acc_ref[h, :, c0:] + _dot(vt_ref[j, h], p)
            m_ref[h, :, c0:] = m_new

    scores(0, 0)

    def pair(j):
        scores(j + 1, 1)
        consume(j, 0)
        scores(j + 2, 0)
        consume(j + 1, 1)

    def body_main(jj, carry):
        for u in range(C_UNROLL_PAIRS):
            pair(2 * (C_UNROLL_PAIRS * jj + u))
        return carry

    def body_rest(jj, carry):
        pair(2 * (C_UNROLL_PAIRS * n_main + jj))
        return carry

    n_main = i // C_UNROLL_PAIRS
    lax.fori_loop(0, n_main, body_main, 0)
    lax.fori_loop(0, i - C_UNROLL_PAIRS * n_main, body_rest, 0)
    scores(2 * i + 1, 1, tk)
    consume(2 * i, 0, 0)
    consume(2 * i + 1, 1, tk)

    ys = []
    for h in range(N_HEADS):
        acc = acc_ref[h]
        ys.append(acc[:HD] / acc[HD:HD + 1])
    o_ref[...] = jnp.concatenate(ys, axis=0).T.astype(o_ref.dtype)


def _mla_prompt(dst, qt, kc, vt, batch, seq):
    tq = C_QTILE
    nq = seq // tq
    return pl.pallas_call(
        _mla_prompt_kernel, grid=(batch, nq),
        in_specs=[_DST_SPEC, pl.BlockSpec((N_HEADS, 256, tq), lambda b, i: (0, 0, b * nq + i)),
                  pl.BlockSpec((seq, 256), lambda b, i: (b, 0)),
                  pl.BlockSpec((seq // C_TILE, N_HEADS, V_ROWS, C_TILE), lambda b, i: (b, 0, 0, 0))],
        out_specs=pl.BlockSpec((tq, 256), lambda b, i: (b * nq + i, 0)),
        out_shape=jax.ShapeDtypeStruct(dst.shape, dst.dtype), input_output_aliases={0: 0},
        scratch_shapes=[pltpu.VMEM((N_HEADS, V_ROWS, tq), F32), pltpu.VMEM((N_HEADS, 1, tq), F32),
                        pltpu.VMEM((2, N_HEADS, C_TILE, tq), F32)],
        compiler_params=_cparams(("parallel", "arbitrary")), name="mla_prompt",
    )(dst, qt, kc, vt)


def _mla_sample_kernel(dst_ref, q_ref, ckv_ref, kr_ref, knew_ref, wuv_ref, o_ref):
    seq = q_ref.shape[1]
    qs = q_ref[...].reshape(N_HEADS * seq, 2 * LANE)
    kcache = jnp.concatenate([ckv_ref[...]] + [kr_ref[...]] * N_HEADS, axis=-1).astype(BF16)
    kall = jnp.concatenate([kcache, knew_ref[...]], axis=0)
    s = _dot_nt(qs, kall)
    m = jnp.max(s, axis=-1, keepdims=True)
    p = jnp.exp2(s - m)
    l = jnp.sum(p, axis=-1, keepdims=True)
    o = (_dot(p.astype(BF16), kall[:, :C_KV_LORA]) / l).astype(BF16)
    for h in range(N_HEADS):
        o_ref[:, h * HD:(h + 1) * HD] = _dot(o[h * seq:(h + 1) * seq], wuv_ref[h]).astype(o_ref.dtype)


def _mla_sample(dst, qabs, kc, cache_ckv, cache_kr, wuv, layer, row0, n_seq, seq):
    blk0 = row0 // seq
    past = cache_ckv.shape[2]
    return pl.pallas_call(
        _mla_sample_kernel, grid=(n_seq,),
        in_specs=[_DST_SPEC, pl.BlockSpec((N_HEADS, seq, 256), lambda b: (0, b, 0)),
                  pl.BlockSpec((None, None, past, C_KV_LORA), lambda b: (layer, b, 0, 0)),
                  pl.BlockSpec((None, None, past, C_ROPE), lambda b: (layer, b, 0, 0)),
                  pl.BlockSpec((seq, 256), lambda b: (blk0 + b, 0)),
                  _full_spec((N_HEADS, 128, HD))],
        out_specs=pl.BlockSpec((seq, 256), lambda b: (blk0 + b, 0)),
        out_shape=jax.ShapeDtypeStruct(dst.shape, dst.dtype), input_output_aliases={0: 0},
        compiler_params=_cparams(("parallel",)), name="mla_sample",
    )(dst, qabs, cache_ckv, cache_kr, kc, wuv)


def _outmlp_kernel(xp_ref, xs_ref, ya_ref, yb_ref, yc_ref, yd_ref, wout_ref, g2_ref, w1_ref, w2_ref, fg_ref,
                   op_ref, os_ref, *, final, n_prompt_tiles):
    i = pl.program_id(0)
    y = jnp.concatenate([ya_ref[...], yb_ref[...], yc_ref[...], yd_ref[...]], axis=-1)
    x1 = jnp.where(i < n_prompt_tiles, xp_ref[...], xs_ref[...]) + _dot(y, wout_ref[...])
    hm = _rms(x1, g2_ref[...]).astype(BF16)
    acc = x1
    for c in range(D_FF // D_MODEL):
        sl = slice(c * D_MODEL, (c + 1) * D_MODEL)
        hc = jnp.square(jnp.maximum(_dot(hm, w1_ref[:, sl]), 0.0)).astype(BF16)
        acc = acc + _dot(hc, w2_ref[sl, :])
    if final:
        acc = _rms(acc, fg_ref[...])

    @pl.when(i < n_prompt_tiles)
    def _():
        op_ref[...] = acc

    @pl.when(i >= n_prompt_tiles)
    def _():
        os_ref[...] = acc


def _outmlp(xp, xs, ya, yb, yc, yd, wout, g2, w1, w2, fg, layer, final, in_place):
    t = xp.shape[0] + xs.shape[0]
    tm = ROW_TILE
    npt = xp.shape[0] // tm
    row = lambda n: pl.BlockSpec((tm, n), lambda i: (i, 0))
    xspecs = _split_rows(npt, D_MODEL)
    return pl.pallas_call(
        functools.partial(_outmlp_kernel, final=final, n_prompt_tiles=npt), grid=(t // tm,),
        in_specs=[*xspecs, row(256), row(256), row(256), row(256), _layer_spec((D_MODEL, D_MODEL), layer),
                  _full_spec((1, D_MODEL)), _layer_spec((D_MODEL, D_FF), layer), _layer_spec((D_FF, D_MODEL), layer),
                  _full_spec((1, D_MODEL))],
        out_specs=xspecs,
        out_shape=(jax.ShapeDtypeStruct(xp.shape, F32), jax.ShapeDtypeStruct(xs.shape, F32)),
        input_output_aliases={0: 0, 1: 1} if in_place else {},
        compiler_params=_cparams(("arbitrary",)), name="outmlp",
    )(xp, xs, ya, yb, yc, yd, wout, g2, w1, w2, fg)


def _rot_cols(w):
    half = C_ROPE // 2
    return jnp.concatenate([-w[..., half:], w[..., :half]], axis=-1)


def _pad_lanes(v, n=LANE, at=0):
    out = jnp.zeros(v.shape[:-1] + (n,), v.dtype)
    return out.at[..., at:at + v.shape[-1]].set(v)


def _prep_w_in(w):
    sizes = (256, 256, 256, 256, 512, 4, 256, 128, 32, 256, 256, 256, 4, 4, 256)
    cuts = [0]
    for s in sizes:
        cuts.append(cuts[-1] + s)
    (a_q, a_k, a_v, b_z, b_xbc, b_dt, c_q, c_kv, c_kr, m_q, m_k, m_v, m_i, m_f, m_o) = (
        w[..., cuts[n]:cuts[n + 1]] for n in range(len(sizes)))
    kr4 = jnp.concatenate([c_kr] * N_HEADS, axis=-1)
    krr4 = jnp.concatenate([_rot_cols(c_kr)] * N_HEADS, axis=-1)
    gates = _pad_lanes(jnp.concatenate([b_dt, m_i, m_f], axis=-1))
    out = jnp.concatenate([a_q, a_k, a_v, b_z, b_xbc, c_q, c_kv, kr4, krr4, gates, m_q, m_k, m_v, m_o], axis=-1)
    assert out.shape[-1] == N_PROJ
    return out.astype(BF16)


def _prep_w_uq(w):
    d = w.shape[0]
    w4 = w.reshape(d, 256, N_HEADS, C_NOPE + C_ROPE)
    nope = w4[..., :C_NOPE].reshape(d, 256, N_HEADS * C_NOPE)
    rope = w4[..., C_NOPE:]
    return jnp.concatenate([nope, rope.reshape(d, 256, N_HEADS * C_ROPE),
                            _rot_cols(rope).reshape(d, 256, N_HEADS * C_ROPE)], axis=-1).astype(BF16)


def _prep_w_ukv(w):
    d = w.shape[0]
    w4 = w.reshape(d, C_KV_LORA, N_HEADS, C_NOPE + HD)
    w_uk = w4[..., :C_NOPE]
    w_uv = w4[..., C_NOPE:]
    wabs = jnp.zeros((d, N_HEADS * C_NOPE, N_HEADS * C_KV_LORA), w.dtype)
    for h in range(N_HEADS):
        wabs = wabs.at[:, h * C_NOPE:(h + 1) * C_NOPE, h * C_KV_LORA:(h + 1) * C_KV_LORA].set(
            jnp.swapaxes(w_uk[:, :, h, :], 1, 2))
    wuv = jnp.transpose(w_uv, (0, 2, 1, 3))
    wuv_all = w_uv.reshape(d, C_KV_LORA, N_HEADS * HD)
    return wabs.astype(BF16), wuv.astype(BF16), wuv_all.astype(BF16)


def _rope_tables(pos):
    half = C_ROPE // 2
    inv = jnp.exp(-math.log(ROPE_BASE) * jnp.arange(half, dtype=F32) / half)
    ang = pos.astype(F32)[:, None] * inv[None, :]
    reps = LANE // half
    return jnp.tile(jnp.cos(ang), (1, reps)), jnp.tile(jnp.sin(ang), (1, reps))


def _band_bias(table, q_rows):
    cols = q_rows + A_SPAN
    k = np.arange(q_rows + cols - 1)
    u = table[:, np.clip((q_rows - 1 - k) + A_SPAN, -A_REL_CLIP, A_REL_CLIP) + A_REL_CLIP].astype(F32)
    period = q_rows + cols
    u = jnp.pad(u, ((0, 0), (0, 1)))
    flat = jnp.tile(u, (1, q_rows))[:, q_rows - 1:q_rows - 1 + q_rows * (period - 1)]
    toep = flat.reshape(table.shape[0], q_rows, period - 1)[:, :, :cols]
    r = np.arange(q_rows)[:, None]
    s = np.arange(cols)[None, :] - (r // CHUNK) * CHUNK
    inband = (s >= 0) & (s < A_SPAN + CHUNK)
    return jnp.where(jnp.asarray(inband)[None], toep, NEG)


def kernel(x_prompt, x_sample, cache_attn_k, cache_attn_v, state_ssm_conv, state_ssm, cache_mla_ckv, cache_mla_kr,
           state_mlstm_c, state_mlstm_n, state_mlstm_m, norm1_g, w_in, attn_rel_bias, ssm_conv_w, ssm_conv_b,
           ssm_dt_bias, ssm_a_log, ssm_d, ssm_norm_g, mla_q_norm_g, mla_w_uq, mla_kv_norm_g, mla_w_ukv,
           mlstm_b_i, mlstm_b_f, mlstm_norm_g, w_out, norm2_g, mlp_w1, mlp_w2, final_g):
    batch, seq, _ = x_prompt.shape
    dbatch, dseq, _ = x_sample.shape
    depth = w_in.shape[0]
    past = cache_mla_ckv.shape[2]
    tp, ts = batch * seq, dbatch * dseq
    lp = REC_CHUNK_PROMPT * REC_STEP_CHUNKS
    assert seq % lp == 0
    assert dseq == CHUNK and past % CHUNK == 0 and cache_attn_k.shape[2] == A_SPAN
    assert seq % C_QTILE == 0 and tp % ROW_TILE == 0 and ts % ROW_TILE == 0

    w_in_r = _prep_w_in(w_in)
    wuq_r = _prep_w_uq(mla_w_uq)
    wabs, wuv, wuv_all = _prep_w_ukv(mla_w_ukv)
    w_out_b, w1_b, w2_b = w_out.astype(BF16), mlp_w1.astype(BF16), mlp_w2.astype(BF16)
    pos = jnp.concatenate([jnp.tile(jnp.arange(seq), batch), jnp.tile(past + jnp.arange(dseq), dbatch)])
    cos, sin = _rope_tables(pos)
    dtb = _pad_lanes(ssm_dt_bias, at=G_DT)[:, None, :]
    alog = _pad_lanes(ssm_a_log, at=G_DT)[:, None, :]
    dsk = _pad_lanes(ssm_d)[:, None, :]
    dtb_t = jnp.swapaxes(dtb[:, :, :GT_ROWS], 1, 2)
    alog_t = jnp.swapaxes(alog[:, :, :GT_ROWS], 1, 2)
    gate_bias =(_pad_lanes(mlstm_b_i, at=G_I) + _pad_lanes(mlstm_b_f, at=G_F))[:, None, :]
    gate_bias_t = jnp.swapaxes(gate_bias[:, :, :GT_ROWS], 1, 2)
    cache_k = cache_attn_k.reshape(depth, dbatch, A_SPAN, 256)
    cache_v = cache_attn_v.reshape(depth, dbatch, A_SPAN, 256)
    conv0_s = jnp.pad(state_ssm_conv, ((0, 0), (0, 0), (8 - (B_CONV - 1), 0), (0, 0)))
    caug0_s = jnp.concatenate([state_mlstm_c, state_mlstm_n[..., None],
                               jnp.zeros(state_mlstm_c.shape[:-1] + (LANE - HD - 1,), F32)], axis=-1)
    m0_s = _pad_lanes(state_mlstm_m)[:, :, None, :]

    xp, xs = x_prompt.reshape(tp, D_MODEL), x_sample.reshape(ts, D_MODEL)
    outs = [[] for _ in range(18)]
    ya, yb, yc, yd = (jnp.zeros((tp + ts, GROUP_W), BF16) for _ in range(4))
    fg = final_g[None, :]
    for l in range(depth):
        (aq, akv, bz, bxbc, qabs, qt, ckv_p, ckv_s, kr_p, kr_s, kc, vt, mqkv, mo, gates, mkt, mvt, gt, aqt, akh,
         avt) = _inproj(
            xp, xs, norm1_g[l][None], w_in_r, mla_q_norm_g[l][None], mla_kv_norm_g[l][None], wuq_r, wabs,
            wuv_all, cos, sin, l)

        rel = attn_rel_bias[l] * math.log2(math.e)
        ya = _attn_a_prompt(ya, aqt, akh, avt, jnp.swapaxes(_band_bias(rel, A_QBLOCK), 1, 2), batch, seq)
        ya = _attn_a_sample(ya, aq, akv, cache_k, cache_v, _band_bias(rel, CHUNK), l, tp, dbatch, dseq)

        ssd_w = (ssm_conv_w[l], ssm_conv_b[l][None], dtb[l], alog[l], dsk[l], ssm_norm_g[l][None])
        yb, h_p = _ssd_prompt(yb, bz, bxbc, gates, gt, ssm_conv_w[l], ssm_conv_b[l][None], dtb[l], dtb_t[l], alog[l],
                              alog_t[l], dsk[l], ssm_norm_g[l][None], batch, seq, lp)
        yb, h_s = _ssd(yb, bz, bxbc, gates, conv0_s[l], state_ssm[l], *ssd_w, tp, dbatch, dseq, dseq)

        yc = _mla_prompt(yc, qt, kc, vt, batch, seq)
        yc = _mla_sample(yc, qabs, kc, cache_mla_ckv, cache_mla_kr, wuv[l], l, tp, dbatch, dseq)

        ml_w = (gate_bias[l], mlstm_norm_g[l][None])
        yd, ct_p, m_p = _mlstm_prompt(yd, mqkv, mkt, mvt, mo, gates, gt, gate_bias[l], gate_bias_t[l],
                                      mlstm_norm_g[l][None], batch, seq, lp)
        yd, c_s, m_s = _mlstm(yd, mqkv, mo, gates, caug0_s[l], m0_s[l], *ml_w, tp, dbatch, dseq, dseq)

        xp, xs = _outmlp(xp, xs, ya, yb, yc, yd, w_out_b, norm2_g[l][None], w1_b, w2_b, fg, l, l == depth - 1, l > 0)

        keep = min(A_SPAN, seq)
        tail = lambda a, n: jnp.stack([a[(b + 1) * seq - n:(b + 1) * seq] for b in range(batch)])
        akv_p = tail(akv, keep).reshape(batch, keep, 2, N_HEADS, HD)
        akv_s = akv[tp:].reshape(dbatch, dseq, 2, N_HEADS, HD)
        new = (akv_p[:, :, 0], akv_p[:, :, 1], tail(bxbc, B_CONV - 1), h_p,
               ckv_p.reshape(batch, seq, C_KV_LORA), kr_p.reshape(batch, seq, C_ROPE),
               jnp.swapaxes(ct_p[:, :, :HD, :], 2, 3), ct_p[:, :, HD, :], m_p[:, 0, :N_HEADS],
               akv_s[:, :, 0], akv_s[:, :, 1],
               bxbc[tp:].reshape(dbatch, dseq, B_CONV_DIM)[:, dseq - (B_CONV - 1):], h_s,
               ckv_s.reshape(dbatch, dseq, C_KV_LORA), kr_s.reshape(dbatch, dseq, C_ROPE),
               c_s[..., :HD], c_s[..., HD], m_s[:, 0, :N_HEADS])
        for dst, v in zip(outs, new):
            dst.append(v)

    y_prompt = xp.reshape(batch, seq, D_MODEL)
    y_sample = xs.reshape(dbatch, dseq, D_MODEL)
    res = [jnp.stack(v) for v in outs]
    res[9] = jnp.concatenate([cache_attn_k[:, :, dseq:], res[9]], axis=2)
    res[10] = jnp.concatenate([cache_attn_v[:, :, dseq:], res[10]], axis=2)
    return (y_prompt, y_sample) + tuple(res)
```

```python
import functools
import math

import jax
import jax.numpy as jnp
import numpy as np
from jax import lax
from jax.experimental import pallas as pl
from jax.experimental.pallas import tpu as pltpu

F32 = jnp.float32
BF16 = jnp.bfloat16
EPS = 1e-6
NEG = -1e30
ROPE_BASE = 10000.0

CHUNK = 64
D_MODEL = 1024
GROUP_W = 256
D_FF = 4096
N_HEADS = 4
HD = 64
A_BAND_CHUNKS = 8
A_SPAN = A_BAND_CHUNKS * CHUNK
A_REL_CLIP = 128
B_GROUPS = 2
B_STATE = 64
B_CONV = 4
B_CONV_DIM = 512
C_NOPE = 64
C_ROPE = 32
C_KV_LORA = 128
A_QSCALE = HD ** -0.5 * math.log2(math.e)
C_SCALE = (C_NOPE + C_ROPE) ** -0.5
C_QSCALE = C_SCALE * math.log2(math.e)
V_ROWS = 80
LANE = 128
VMEM_LIMIT = 56 * 1024 * 1024

OFF_A, OFF_BZ, OFF_XBC, OFF_CQ, OFF_CKV, OFF_KR, OFF_KRR, OFF_G, OFF_MQKV, OFF_MO, N_PROJ = (
    0, 768, 1024, 1536, 1792, 1920, 2048, 2176, 2304, 3072, 3328)
G_DT, G_I, G_F = 0, 4, 8
GT_ROWS = 16

ROW_TILE = 512
A_QBLOCK = 256
A_STEP_BLOCKS = 4
C_TILE = 256
C_QTILE = 2 * C_TILE
C_UNROLL_PAIRS = 4
SAMPLE_GROUP = 4
REC_CHUNK_PROMPT = 256
REC_STEP_CHUNKS = 4


def _cparams(sem):
    return pltpu.CompilerParams(dimension_semantics=sem, vmem_limit_bytes=VMEM_LIMIT)


def _rms(x, g):
    return x * lax.rsqrt(jnp.mean(x * x, axis=-1, keepdims=True) + EPS) * g


def _dot(a, b):
    return jnp.dot(a, b, preferred_element_type=F32)


def _dot_nt(a, b):
    return lax.dot_general(a, b, (((1,), (1,)), ((), ())), preferred_element_type=F32)


def _dot_tn(a, b):
    return lax.dot_general(a, b, (((0,), (0,)), ((), ())), preferred_element_type=F32)


def _cumsum_rows(x, tri_bf16):
    hi = x.astype(BF16)
    r1 = x - hi.astype(F32)
    mid = r1.astype(BF16)
    lo = (r1 - mid.astype(F32)).astype(BF16)
    return _dot(tri_bf16, hi) + _dot(tri_bf16, mid) + _dot(tri_bf16, lo)


def _log1p_exp_neg_abs(x):
    return jnp.log1p(jnp.exp(-jnp.abs(x)))


def _full_spec(shape):
    n = len(shape)
    return pl.BlockSpec(shape, lambda *_: (0,) * n)


_DST_SPEC = pl.BlockSpec(memory_space=pl.ANY)


def _const_spec(shape):
    n = len(shape)
    return pl.BlockSpec(shape, lambda *_: (0,) * n, pipeline_mode=pl.Buffered(1))


def _layer_spec(shape, layer):
    n = len(shape)
    return pl.BlockSpec((None,) + tuple(shape), lambda *_: (layer,) + (0,) * n, pipeline_mode=pl.Buffered(1))


def _split_rows(n_prompt_tiles, width):
    npt = n_prompt_tiles
    return (pl.BlockSpec((ROW_TILE, width), lambda i: (jnp.minimum(i, npt - 1), 0)),
            pl.BlockSpec((ROW_TILE, width), lambda i: (jnp.maximum(i - npt, 0), 0)))


def _inproj_kernel(xp_ref, xs_ref, g1_ref, w_ref, qg_ref, kvg_ref, wuq_ref, wabs_ref, wuv_ref, cos_ref, sin_ref,
                   aq_ref, akv_ref, bz_ref, bxbc_ref, qabs_ref, qt_ref, ckvp_ref, ckvs_ref, krp_ref, krs_ref, kc_ref, vt_ref,
                   mqkv_ref, mo_ref, gates_ref, mkt_ref, mvt_ref, gt_ref, aqt_ref, akh_ref, avt_ref, *,
                   n_prompt_tiles):
    tm = xp_ref.shape[0]
    is_prompt = pl.program_id(0) < n_prompt_tiles
    x = jnp.where(is_prompt, xp_ref[...], xs_ref[...])
    hn = _rms(x, g1_ref[...]).astype(BF16)

    def seg(a, b):
        return _dot(hn, w_ref[:, a:b])

    a = seg(OFF_A, OFF_BZ)
    aq = a[:, :GROUP_W] * A_QSCALE
    aq_ref[...] = aq.astype(BF16)
    akv_ref[...] = a[:, GROUP_W:]
    aqt_ref[...] = aq.T.astype(BF16)
    avt_ref[...] = a[:, 2 * GROUP_W:].T.astype(BF16)
    for h in range(N_HEADS):
        akh_ref[h] = a[:, GROUP_W + h * HD:GROUP_W + (h + 1) * HD].astype(BF16)
    bz_ref[...] = seg(OFF_BZ, OFF_XBC)
    bxbc_ref[...] = seg(OFF_XBC, OFF_CQ)

    cos = cos_ref[...]
    sin = sin_ref[...]
    cqn = _rms(seg(OFF_CQ, OFF_CKV), qg_ref[...]).astype(BF16)
    qf = _dot(cqn, wuq_ref[...])
    qrope = (qf[:, 256:384] * cos + qf[:, 384:512] * sin) * C_QSCALE
    qlat = _dot(qf[:, :256].astype(BF16), wabs_ref[...]) * C_QSCALE
    lane_head = lax.broadcasted_iota(jnp.int32, (tm, LANE), 1) // C_ROPE
    qhs = [jnp.concatenate([qlat[:, h * LANE:(h + 1) * LANE], jnp.where(lane_head == h, qrope, 0.0)], axis=-1)
           for h in range(N_HEADS)]

    ckv_kr = seg(OFF_CKV, OFF_KRR)
    krr_g = seg(OFF_KRR, OFF_MQKV)
    g = krr_g[:, LANE:]
    ckv = _rms(ckv_kr[:, :C_KV_LORA], kvg_ref[...])
    kr4 = ckv_kr[:, C_KV_LORA:] * cos + krr_g[:, :LANE] * sin
    kc_ref[...] = jnp.concatenate([ckv, kr4], axis=-1).astype(BF16)

    m = seg(OFF_MQKV, OFF_MO)
    mk = m[:, 256:512] * HD ** -0.5
    mqkv_ref[:, 0:256] = m[:, 0:256].astype(BF16)
    mqkv_ref[:, 256:512] = mk.astype(BF16)
    mqkv_ref[:, 512:768] = m[:, 512:768].astype(BF16)
    mo_ref[...] = seg(OFF_MO, N_PROJ)
    gates_ref[...] = g

    for h in range(N_HEADS):
        qt_ref[h] = qhs[h].T.astype(BF16)
    vall = _dot(ckv.astype(BF16), wuv_ref[...])
    ones_c = (lax.broadcasted_iota(jnp.int32, (V_ROWS - HD, C_TILE), 0) == 0).astype(BF16)
    for s in range(tm // C_TILE):
        vt = vall[s * C_TILE:(s + 1) * C_TILE, :].T
        for h in range(N_HEADS):
            vt_ref[s, h, 0:HD, :] = vt[h * HD:(h + 1) * HD].astype(BF16)
            vt_ref[s, h, HD:V_ROWS, :] = ones_c
    mkt = mk.T
    mvt = m[:, 512:768].T
    ones_v = (lax.broadcasted_iota(jnp.int32, (LANE - HD, tm), 0) == 0).astype(BF16)
    for h in range(N_HEADS):
        mkt_ref[h] = mkt[h * HD:(h + 1) * HD].astype(BF16)
        mvt_ref[h, 0:HD, :] = mvt[h * HD:(h + 1) * HD].astype(BF16)
        mvt_ref[h, HD:LANE, :] = ones_v
    gt_ref[...] = g.T[0:GT_ROWS]

    @pl.when(is_prompt)
    def _():
        ckvp_ref[...] = ckv
        krp_ref[...] = kr4[:, :C_ROPE]

    @pl.when(jnp.logical_not(is_prompt))
    def _():
        ckvs_ref[...] = ckv
        krs_ref[...] = kr4[:, :C_ROPE]
        for h in range(N_HEADS):
            qabs_ref[h] = qhs[h].astype(BF16)


def _inproj(xp, xs, g1, w, qg, kvg, wuq, wabs, wuv_all, cos, sin, layer):
    tp, ts = xp.shape[0], xs.shape[0]
    t = tp + ts
    tm = ROW_TILE
    npt = tp // tm
    col = lambda *lead: pl.BlockSpec((*lead, tm), lambda i: (0,) * len(lead) + (i,))
    row = lambda n: pl.BlockSpec((tm, n), lambda i: (i, 0))
    out_shape = (
        jax.ShapeDtypeStruct((t, 256), BF16),
        jax.ShapeDtypeStruct((t, 512), F32),
        jax.ShapeDtypeStruct((t, 256), F32),
        jax.ShapeDtypeStruct((t, 512), F32),
        jax.ShapeDtypeStruct((N_HEADS, ts, 256), BF16),
        jax.ShapeDtypeStruct((N_HEADS, 256, t), BF16),
        jax.ShapeDtypeStruct((tp, 128), F32),
        jax.ShapeDtypeStruct((ts, 128), F32),
        jax.ShapeDtypeStruct((tp, C_ROPE), F32),
        jax.ShapeDtypeStruct((ts, C_ROPE), F32),
        jax.ShapeDtypeStruct((t, 256), BF16),
        jax.ShapeDtypeStruct((t // C_TILE, N_HEADS, V_ROWS, C_TILE), BF16),
        jax.ShapeDtypeStruct((t, 768), BF16),
        jax.ShapeDtypeStruct((t, 256), F32),
        jax.ShapeDtypeStruct((t, LANE), F32),
        jax.ShapeDtypeStruct((N_HEADS, HD, t), BF16),
        jax.ShapeDtypeStruct((N_HEADS, LANE, t), BF16),
        jax.ShapeDtypeStruct((GT_ROWS, t), F32),
        jax.ShapeDtypeStruct((GROUP_W, t), BF16),
        jax.ShapeDtypeStruct((N_HEADS, t, HD), BF16),
        jax.ShapeDtypeStruct((GROUP_W, t), BF16),
    )
    out_specs = (
        row(256), row(512), row(256), row(512),
        pl.BlockSpec((N_HEADS, tm, 256), lambda i: (0, jnp.maximum(i - npt, 0), 0)),
        col(N_HEADS, 256),
        *_split_rows(npt, 128), *_split_rows(npt, C_ROPE), row(256),
        pl.BlockSpec((tm // C_TILE, N_HEADS, V_ROWS, C_TILE), lambda i: (i, 0, 0, 0)),
        row(768), row(256), row(LANE),
        col(N_HEADS, HD), col(N_HEADS, LANE), col(GT_ROWS),
        col(GROUP_W), pl.BlockSpec((N_HEADS, tm, HD), lambda i: (0, i, 0)), col(GROUP_W),
    )
    in_specs = [*_split_rows(npt, D_MODEL), _full_spec((1, D_MODEL)), _layer_spec((D_MODEL, N_PROJ), layer),
                _full_spec((1, 256)), _full_spec((1, 128)), _layer_spec((256, 512), layer),
                _layer_spec((256, 512), layer), _layer_spec((C_KV_LORA, 256), layer), row(LANE), row(LANE)]
    return pl.pallas_call(
        functools.partial(_inproj_kernel, n_prompt_tiles=npt), grid=(t // tm,), in_specs=in_specs,
        out_specs=out_specs, out_shape=out_shape,
        compiler_params=_cparams(("arbitrary",)), name="inproj",
    )(xp, xs, g1, w, qg, kvg, wuq, wabs, wuv_all, cos, sin)


def _band_heads(q, kb, vb, bias_ref, valid, o_ref, r0=0):
    for h in range(N_HEADS):
        sl = slice(h * HD, (h + 1) * HD)
        s = _dot_nt(q[:, sl], kb[:, sl]) + bias_ref[h]
        if valid is not None:
            s = jnp.where(valid, s, NEG)
        m = jnp.max(s, axis=-1, keepdims=True)
        p = jnp.exp2(s - m)
        l = jnp.sum(p, axis=-1, keepdims=True)
        o = _dot(p.astype(BF16), vb[:, sl]) / l
        o_ref[r0:r0 + q.shape[0], sl] = o.astype(o_ref.dtype)


def _attn_a_prompt_kernel(dst_ref, qt_ref, *refs):
    n_kb = A_STEP_BLOCKS + 2
    krefs, vrefs = refs[:n_kb], refs[n_kb:2 * n_kb]
    bias_ref, o_ref, s_ref = refs[2 * n_kb:]
    i = pl.program_id(1)
    qb = A_QBLOCK
    nk = 3 * qb
    vts = [jnp.concatenate([vrefs[sub + d][...] for d in range(3)], axis=1) for sub in range(A_STEP_BLOCKS)]
    ones = (lax.broadcasted_iota(jnp.int32, (V_ROWS - HD, nk), 0) == 0).astype(BF16)
    items = [(sub, h) for sub in range(A_STEP_BLOCKS) for h in range(N_HEADS)]

    def scores(n):
        sub, h = items[n]
        kh = jnp.concatenate([krefs[sub + d][h] for d in range(3)], axis=0)
        q = qt_ref[h * HD:(h + 1) * HD, sub * qb:(sub + 1) * qb]
        s_ref[n % 2] = _dot(kh, q) + bias_ref[h]

    def run(first_step):
        outs = [[] for _ in range(A_STEP_BLOCKS)]
        scores(0)
        for n, (sub, h) in enumerate(items):
            if n + 1 < len(items):
                scores(n + 1)
            s = s_ref[n % 2]
            if first_step and sub < 2:
                krow = lax.broadcasted_iota(jnp.int32, (nk, qb), 0)
                s = jnp.where(krow >= (2 - sub) * qb, s, NEG)
            m = jnp.max(s, axis=0, keepdims=True)
            p = jnp.exp2(s - m).astype(BF16)
            vaug = jnp.concatenate([vts[sub][h * HD:(h + 1) * HD, :], ones], axis=0)
            o = _dot(vaug, p)
            outs[sub].append(o[:HD] / o[HD:HD + 1])
        for sub in range(A_STEP_BLOCKS):
            o_ref[sub * qb:(sub + 1) * qb, :] = jnp.concatenate(outs[sub], axis=0).T.astype(o_ref.dtype)

    @pl.when(i == 0)
    def _():
        run(True)

    @pl.when(i > 0)
    def _():
        run(False)


def _attn_a_prompt(dst, aqt, akh, avt, biasm_t, batch, seq):
    qb = A_QBLOCK
    nb = seq // qb
    ns = nb // A_STEP_BLOCKS
    blk = lambda b, i, d: b * nb + jnp.maximum(A_STEP_BLOCKS * i + d, 0)
    kspec = lambda d: pl.BlockSpec((N_HEADS, qb, HD), lambda b, i: (0, blk(b, i, d), 0))
    vspec = lambda d: pl.BlockSpec((GROUP_W, qb), lambda b, i: (0, blk(b, i, d)))
    offs = tuple(range(-2, A_STEP_BLOCKS))
    return pl.pallas_call(
        _attn_a_prompt_kernel, grid=(batch, ns),
        in_specs=[_DST_SPEC, pl.BlockSpec((GROUP_W, A_STEP_BLOCKS * qb), lambda b, i: (0, b * ns + i)),
                  *[kspec(d) for d in offs], *[vspec(d) for d in offs], _full_spec(biasm_t.shape)],
        out_specs=pl.BlockSpec((A_STEP_BLOCKS * qb, 256), lambda b, i: (b * ns + i, 0)),
        out_shape=jax.ShapeDtypeStruct(dst.shape, dst.dtype), input_output_aliases={0: 0},
        scratch_shapes=[pltpu.VMEM((2, 3 * qb, qb), F32)],
        compiler_params=_cparams(("parallel", "parallel")), name="attn_a_prompt",
    )(dst, aqt, *[akh] * len(offs), *[avt] * len(offs), biasm_t)


def _attn_a_sample_kernel(dst_ref, q_ref, ck_ref, cv_ref, kvn_ref, bias_ref, o_ref):
    group = ck_ref.shape[0]
    seq = q_ref.shape[0] // group
    for n in range(group):
        rows = slice(n * seq, (n + 1) * seq)
        kvn = kvn_ref[rows, :]
        kb = jnp.concatenate([ck_ref[n], kvn[:, :GROUP_W]], axis=0).astype(BF16)
        vb = jnp.concatenate([cv_ref[n], kvn[:, GROUP_W:]], axis=0).astype(BF16)
        _band_heads(q_ref[rows, :], kb, vb, bias_ref, None, o_ref, n * seq)


def _attn_a_sample(dst, aq, akv, cache_k, cache_v, bias, layer, row0, n_seq, seq):
    grp = SAMPLE_GROUP
    rows = grp * seq
    assert n_seq % grp == 0 and row0 % rows == 0
    blk0 = row0 // rows
    cspec = pl.BlockSpec((None, grp, A_SPAN, 256), lambda b: (layer, b, 0, 0))
    return pl.pallas_call(
        _attn_a_sample_kernel, grid=(n_seq // grp,),
        in_specs=[_DST_SPEC, pl.BlockSpec((rows, 256), lambda b: (blk0 + b, 0)), cspec, cspec,
                  pl.BlockSpec((rows, 512), lambda b: (blk0 + b, 0)), _full_spec(bias.shape)],
        out_specs=pl.BlockSpec((rows, 256), lambda b: (blk0 + b, 0)),
        out_shape=jax.ShapeDtypeStruct(dst.shape, dst.dtype), input_output_aliases={0: 0},
        compiler_params=_cparams(("parallel",)), name="attn_a_sample",
    )(dst, aq, cache_k, cache_v, akv, bias)


def _ssd_kernel(dst_ref, z_ref, xbc_ref, g_ref, conv0_ref, h0_ref, cw_ref, cb_ref, dtb_ref, alog_ref, dsk_ref, ng_ref,
                y_ref, hout_ref, xpad_ref, h_ref):
    L = z_ref.shape[0]
    c = pl.program_id(1)

    @pl.when(c == 0)
    def _():
        xpad_ref[...] = conv0_ref[...]
        h_ref[...] = h0_ref[...]

    xbc = xbc_ref[...]
    xfull = jnp.concatenate([xpad_ref[...], xbc], axis=0)
    conv = cb_ref[...] + xbc * cw_ref[B_CONV - 1:B_CONV, :]
    for j in range(B_CONV - 1):
        conv = conv + pltpu.roll(xfull, B_CONV - 1 - j, axis=0)[8:8 + L, :] * cw_ref[j:j + 1, :]
    xpad_ref[...] = xbc[L - 8:L, :]
    u = conv * jax.nn.sigmoid(conv)

    g = g_ref[...] + dtb_ref[...]
    dt = jnp.maximum(g, 0.0) + _log1p_exp_neg_abs(g)
    da = dt * (-jnp.exp(alog_ref[...]))
    row = lax.broadcasted_iota(jnp.int32, (L, L), 0)
    colm = lax.broadcasted_iota(jnp.int32, (L, L), 1)
    tri = row >= colm
    cs = _cumsum_rows(da, tri.astype(BF16))
    cs_t = cs.T

    ys = []
    gmat = [None] * B_GROUPS
    for h in range(N_HEADS):
        grp = h // (N_HEADS // B_GROUPS)
        xs = u[:, h * HD:(h + 1) * HD]
        bm = u[:, 256 + grp * B_STATE:256 + (grp + 1) * B_STATE].astype(BF16)
        cm = u[:, 384 + grp * B_STATE:384 + (grp + 1) * B_STATE].astype(BF16)
        if gmat[grp] is None:
            gmat[grp] = _dot_nt(cm, bm)
        csc = cs[:, G_DT + h:G_DT + h + 1]
        csr = cs_t[G_DT + h:G_DT + h + 1, :]
        tot = cs[L - 1:L, G_DT + h:G_DT + h + 1]
        dec = jnp.exp(jnp.where(tri, csc - csr, NEG))
        xdt = xs * dt[:, G_DT + h:G_DT + h + 1]
        hprev = h_ref[h]
        y = _dot((gmat[grp] * dec).astype(BF16), xdt.astype(BF16))
        y = y + _dot_nt(cm, hprev.astype(BF16)) * jnp.exp(csc)
        y = y + dsk_ref[0:1, h:h + 1] * xs
        ys.append(y)
        wend = (xdt * jnp.exp(tot - csc)).astype(BF16)
        h_ref[h] = jnp.exp(tot) * hprev + _dot_tn(wend, bm)
    y = jnp.concatenate(ys, axis=-1)
    z = z_ref[...]
    y = y * (z * jax.nn.sigmoid(z))
    y_ref[...] = _rms(y, ng_ref[...]).astype(y_ref.dtype)

    @pl.when(c == pl.num_programs(1) - 1)
    def _():
        hout_ref[...] = h_ref[...]


def _ssd(dst, bz, bxbc, gates, conv0, h0, cw, cb, dtb, alog, dsk, ng, row0, n_seq, seq, L):
    nc = seq // L
    blk0 = row0 // L
    rspec = lambda n: pl.BlockSpec((L, n), lambda b, c: (blk0 + b * nc + c, 0))
    return pl.pallas_call(
        _ssd_kernel, grid=(n_seq, nc),
        in_specs=[_DST_SPEC, rspec(256), rspec(512), rspec(LANE),
                  pl.BlockSpec((None, 8, 512), lambda b, c: (b, 0, 0)),
                  pl.BlockSpec((None, N_HEADS, HD, B_STATE), lambda b, c: (b, 0, 0, 0)),
                  _full_spec((B_CONV, 512)), _full_spec((1, 512)), _full_spec((1, LANE)), _full_spec((1, LANE)),
                  _full_spec((1, LANE)), _full_spec((1, 256))],
        out_specs=(rspec(256), pl.BlockSpec((None, N_HEADS, HD, B_STATE), lambda b, c: (b, 0, 0, 0))),
        input_output_aliases={0: 0},
        out_shape=(jax.ShapeDtypeStruct(dst.shape, dst.dtype),
                   jax.ShapeDtypeStruct((n_seq, N_HEADS, HD, B_STATE), F32)),
        scratch_shapes=[pltpu.VMEM((8, 512), F32), pltpu.VMEM((N_HEADS, HD, B_STATE), F32)],
        compiler_params=_cparams(("parallel", "arbitrary")), name="ssd",
    )(dst, bz, bxbc, gates, conv0, h0, cw, cb, dtb, alog, dsk, ng)


def _ssd_prompt_kernel(dst_ref, z_ref, xbc_ref, g_ref, gt_ref, cw_ref, cb_ref, dtb_ref, dtbt_ref, alog_ref, alogt_ref,
                       dsk_ref, ng_ref, y_ref, hout_ref, xpad_ref, h_ref):
    L = z_ref.shape[0]
    c = pl.program_id(1)

    @pl.when(c == 0)
    def _():
        xpad_ref[...] = jnp.zeros_like(xpad_ref)
        h_ref[...] = jnp.zeros_like(h_ref)

    xbc = xbc_ref[...]
    xfull = jnp.concatenate([xpad_ref[...], xbc], axis=0)
    conv = cb_ref[...] + xbc * cw_ref[B_CONV - 1:B_CONV, :]
    for j in range(B_CONV - 1):
        conv = conv + pltpu.roll(xfull, B_CONV - 1 - j, axis=0)[8:8 + L, :] * cw_ref[j:j + 1, :]
    xpad_ref[...] = xbc[L - 8:L, :]
    u_all = conv * jax.nn.sigmoid(conv)

    Lc = REC_CHUNK_PROMPT
    row = lax.broadcasted_iota(jnp.int32, (Lc, Lc), 0)
    colm = lax.broadcasted_iota(jnp.int32, (Lc, Lc), 1)
    s_le_l = row <= colm
    tri_col = (row >= colm).astype(BF16)
    tri_row = s_le_l.astype(BF16)
    hs = [h_ref[h] for h in range(N_HEADS)]

    for r0 in range(0, L, Lc):
        rows = slice(r0, r0 + Lc)
        u = u_all[rows, :]
        g = g_ref[rows, :] + dtb_ref[...]
        gt = gt_ref[:, rows] + dtbt_ref[...]
        dt = jnp.maximum(g, 0.0) + _log1p_exp_neg_abs(g)
        dtt = jnp.maximum(gt, 0.0) + _log1p_exp_neg_abs(gt)
        cs = _cumsum_rows(dt * (-jnp.exp(alog_ref[...])), tri_col)
        cs_t = _cumsum_lanes(dtt * (-jnp.exp(alogt_ref[...])), tri_row)

        xst = u[:, :GROUP_W].T
        yts = []
        gmat_t = [None] * B_GROUPS
        for h in range(N_HEADS):
            grp = h // (N_HEADS // B_GROUPS)
            bm = u[:, 256 + grp * B_STATE:256 + (grp + 1) * B_STATE].astype(BF16)
            cm = u[:, 384 + grp * B_STATE:384 + (grp + 1) * B_STATE].astype(BF16)
            if gmat_t[grp] is None:
                gmat_t[grp] = _dot_nt(bm, cm)
            csc = cs[:, G_DT + h:G_DT + h + 1]
            csr = cs_t[G_DT + h:G_DT + h + 1, :]
            tot = cs_t[G_DT + h:G_DT + h + 1, Lc - 1:Lc]
            xs_t = xst[h * HD:(h + 1) * HD]
            xdt_t = xs_t * dtt[G_DT + h:G_DT + h + 1, :]
            dec_t = jnp.exp(jnp.where(s_le_l, csr - csc, NEG))
            hprev = hs[h]
            y = _dot(xdt_t.astype(BF16), (gmat_t[grp] * dec_t).astype(BF16))
            y = y + _dot_nt(hprev.astype(BF16), cm) * jnp.exp(csr)
            yts.append(y + dsk_ref[0:1, h:h + 1] * xs_t)
            wend = (xdt_t * jnp.exp(tot - csr)).astype(BF16)
            hs[h] = jnp.exp(tot) * hprev + _dot(wend, bm)
        y = jnp.concatenate(yts, axis=0).T
        z = z_ref[rows, :]
        y = y * (z * jax.nn.sigmoid(z))
        y_ref[rows, :] = _rms(y, ng_ref[...]).astype(y_ref.dtype)

    for h in range(N_HEADS):
        h_ref[h] = hs[h]

    @pl.when(c == pl.num_programs(1) - 1)
    def _():
        hout_ref[...] = h_ref[...]


def _ssd_prompt(dst, bz, bxbc, gates, gt, cw, cb, dtb, dtbt, alog, alogt, dsk, ng, n_seq, seq, L):
    nc = seq // L
    rspec = lambda n: pl.BlockSpec((L, n), lambda b, c: (b * nc + c, 0))
    return pl.pallas_call(
        _ssd_prompt_kernel, grid=(n_seq, nc),
        in_specs=[_DST_SPEC, rspec(256), rspec(512), rspec(LANE),
                  pl.BlockSpec((GT_ROWS, L), lambda b, c: (0, b * nc + c)),
                  _full_spec((B_CONV, 512)), _full_spec((1, 512)), _full_spec((1, LANE)), _full_spec((GT_ROWS, 1)),
                  _full_spec((1, LANE)), _full_spec((GT_ROWS, 1)), _full_spec((1, LANE)), _full_spec((1, 256))],
        out_specs=(rspec(256), pl.BlockSpec((None, N_HEADS, HD, B_STATE), lambda b, c: (b, 0, 0, 0))),
        input_output_aliases={0: 0},
        out_shape=(jax.ShapeDtypeStruct(dst.shape, dst.dtype),
                   jax.ShapeDtypeStruct((n_seq, N_HEADS, HD, B_STATE), F32)),
        scratch_shapes=[pltpu.VMEM((8, 512), F32), pltpu.VMEM((N_HEADS, HD, B_STATE), F32)],
        compiler_params=_cparams(("parallel", "arbitrary")), name="ssd_prompt",
    )(dst, bz, bxbc, gates, gt, cw, cb, dtb, dtbt, alog, alogt, dsk, ng)


def _mlstm_kernel(dst_ref, qkv_ref, o_ref, g_ref, c0_ref, m0_ref, bias_ref, ng_ref,
                  y_ref, cout_ref, mout_ref, c_ref, m_ref):
    L = qkv_ref.shape[0]
    c = pl.program_id(1)

    @pl.when(c == 0)
    def _():
        c_ref[...] = c0_ref[...]
        m_ref[...] = m0_ref[...]

    g = g_ref[...] + bias_ref[...]
    lf = jnp.minimum(g, 0.0) - _log1p_exp_neg_abs(g)
    row = lax.broadcasted_iota(jnp.int32, (L, L), 0)
    colm = lax.broadcasted_iota(jnp.int32, (L, L), 1)
    tri = row >= colm
    bcum = _cumsum_rows(lf, tri.astype(BF16))
    bcum_t = bcum.T
    g_t = g.T
    unit = (lax.broadcasted_iota(jnp.int32, (L, HD), 1) == 0).astype(BF16)
    ng = ng_ref[...]
    og = o_ref[...]

    for h in range(N_HEADS):
        sl = slice(h * HD, (h + 1) * HD)
        q = qkv_ref[:, h * HD:(h + 1) * HD]
        k = qkv_ref[:, 256 + h * HD:256 + (h + 1) * HD]
        v = qkv_ref[:, 512 + h * HD:512 + (h + 1) * HD]
        vaug = jnp.concatenate([v, unit], axis=-1)
        bc = bcum[:, G_F + h:G_F + h + 1]
        br = bcum_t[G_F + h:G_F + h + 1, :]
        ic = g[:, G_I + h:G_I + h + 1]
        ir = g_t[G_I + h:G_I + h + 1, :]
        tot = bcum[L - 1:L, G_F + h:G_F + h + 1]
        mprev = m_ref[0:1, h:h + 1]

        dm = jnp.where(tri, bc - br + ir, NEG)
        inter = bc + mprev
        mt = jnp.maximum(inter, jnp.max(dm, axis=-1, keepdims=True))
        w_intra = jnp.exp(dm - mt)
        w_inter = jnp.exp(inter - mt)
        caug = c_ref[h]
        s = _dot_nt(q, k) * w_intra
        r = w_inter * _dot(q, caug.astype(BF16)) + _dot(s.astype(BF16), vaug)
        num = r[:, :HD]
        den = r[:, HD:HD + 1]
        hout = num / jnp.maximum(jnp.abs(den), jnp.exp(-mt))

        gend = tot - bc + ic
        mnew = jnp.maximum(tot + mprev, jnp.max(gend, axis=0, keepdims=True))
        kw = (k.astype(F32) * jnp.exp(gend - mnew)).astype(BF16)
        c_ref[h] = jnp.exp(tot + mprev - mnew) * caug + _dot_tn(kw, vaug)
        m_ref[0:1, h:h + 1] = mnew

        yh = _rms(hout, ng[:, sl]) * jax.nn.sigmoid(og[:, sl])
        y_ref[:, sl] = yh.astype(y_ref.dtype)

    @pl.when(c == pl.num_programs(1) - 1)
    def _():
        cout_ref[...] = c_ref[...]
        mout_ref[...] = m_ref[...]


def _cumsum_lanes(x, tri_bf16):
    hi = x.astype(BF16)
    r1 = x - hi.astype(F32)
    mid = r1.astype(BF16)
    lo = (r1 - mid.astype(F32)).astype(BF16)
    return _dot(hi, tri_bf16) + _dot(mid, tri_bf16) + _dot(lo, tri_bf16)


def _mlstm_prompt_kernel(dst_ref, qkv_ref, kt_ref, vt_ref, o_ref, g_ref, gt_ref, bias_ref, biast_ref, ng_ref,
                         y_ref, cout_ref, mout_ref, c_ref, m_ref):
    L = REC_CHUNK_PROMPT
    c = pl.program_id(1)

    @pl.when(c == 0)
    def _():
        c_ref[...] = jnp.zeros_like(c_ref)
        m_ref[...] = jnp.zeros_like(m_ref)

    row = lax.broadcasted_iota(jnp.int32, (L, L), 0)
    colm = lax.broadcasted_iota(jnp.int32, (L, L), 1)
    s_le_l = row <= colm
    tri_col = (row >= colm).astype(BF16)
    tri_row = s_le_l.astype(BF16)
    cts = [c_ref[h] for h in range(N_HEADS)]
    ms = [m_ref[0:1, h:h + 1] for h in range(N_HEADS)]

    for r0 in range(0, qkv_ref.shape[0], L):
        rows = slice(r0, r0 + L)
        g = g_ref[rows, :] + bias_ref[...]
        gt = gt_ref[:, rows] + biast_ref[...]
        lf = jnp.minimum(g, 0.0) - _log1p_exp_neg_abs(g)
        lft = jnp.minimum(gt, 0.0) - _log1p_exp_neg_abs(gt)
        bcum = _cumsum_rows(lf, tri_col)
        bcum_t = _cumsum_lanes(lft, tri_row)

        yts = []
        for h in range(N_HEADS):
            q = qkv_ref[rows, h * HD:(h + 1) * HD]
            k = qkv_ref[rows, 256 + h * HD:256 + (h + 1) * HD]
            vt = vt_ref[h, :, rows]
            src = g[:, G_I + h:G_I + h + 1] - bcum[:, G_F + h:G_F + h + 1]
            bc = bcum_t[G_F + h:G_F + h + 1, :]
            ir = gt[G_I + h:G_I + h + 1, :]
            tot = bcum_t[G_F + h:G_F + h + 1, L - 1:L]
            mprev = ms[h]

            dmt = jnp.where(s_le_l, bc + src, NEG)
            inter = bc + mprev
            mt = jnp.maximum(inter, jnp.max(dmt, axis=0, keepdims=True))
            w_intra = jnp.exp(dmt - mt)
            w_inter = jnp.exp(inter - mt)
            ct = cts[h]
            st = _dot_nt(k, q) * w_intra
            rt = w_inter * _dot_nt(ct.astype(BF16), q) + _dot(vt, st.astype(BF16))
            hout = rt[:HD] / jnp.maximum(jnp.abs(rt[HD:HD + 1]), jnp.exp(-mt))
            yts.append(hout * lax.rsqrt(jnp.mean(hout * hout, axis=0, keepdims=True) + EPS))

            gend = tot - bc + ir
            mnew = jnp.maximum(tot + mprev, jnp.max(gend, axis=-1, keepdims=True))
            kw = (kt_ref[h, :, rows].astype(F32) * jnp.exp(gend - mnew)).astype(BF16)
            cts[h] = jnp.exp(tot + mprev - mnew) * ct + _dot_nt(vt, kw)
            ms[h] = mnew

        y = jnp.concatenate(yts, axis=0).T
        y_ref[rows, :] = (y * ng_ref[...] * jax.nn.sigmoid(o_ref[rows, :])).astype(y_ref.dtype)

    for h in range(N_HEADS):
        c_ref[h] = cts[h]
        m_ref[0:1, h:h + 1] = ms[h]

    @pl.when(c == pl.num_programs(1) - 1)
    def _():
        cout_ref[...] = c_ref[...]
        mout_ref[...] = m_ref[...]


def _mlstm_prompt(dst, mqkv, mkt, mvt, mo, gates, gt, bias, biast, ng, n_seq, seq, L):
    nc = seq // L
    rspec = lambda n: pl.BlockSpec((L, n), lambda b, c: (b * nc + c, 0))
    cspec = lambda *lead: pl.BlockSpec((*lead, L), lambda b, c: (0,) * len(lead) + (b * nc + c,))
    stspec = pl.BlockSpec((None, N_HEADS, LANE, HD), lambda b, c: (b, 0, 0, 0))
    mspec = pl.BlockSpec((None, 1, LANE), lambda b, c: (b, 0, 0))
    return pl.pallas_call(
        _mlstm_prompt_kernel, grid=(n_seq, nc),
        in_specs=[_DST_SPEC, rspec(768), cspec(N_HEADS, HD), cspec(N_HEADS, LANE), rspec(256), rspec(LANE),
                  cspec(GT_ROWS), _full_spec((1, LANE)), _full_spec((GT_ROWS, 1)), _full_spec((1, 256))],
        out_specs=(rspec(256), stspec, mspec), input_output_aliases={0: 0},
        out_shape=(jax.ShapeDtypeStruct(dst.shape, dst.dtype),
                   jax.ShapeDtypeStruct((n_seq, N_HEADS, LANE, HD), F32),
                   jax.ShapeDtypeStruct((n_seq, 1, LANE), F32)),
        scratch_shapes=[pltpu.VMEM((N_HEADS, LANE, HD), F32), pltpu.VMEM((1, LANE), F32)],
        compiler_params=_cparams(("parallel", "arbitrary")), name="mlstm_prompt",
    )(dst, mqkv, mkt, mvt, mo, gates, gt, bias, biast, ng)


def _mlstm(dst, mqkv, mo, gates, c0, m0, bias, ng, row0, n_seq, seq, L):
    nc = seq // L
    blk0 = row0 // L
    rspec = lambda n: pl.BlockSpec((L, n), lambda b, c: (blk0 + b * nc + c, 0))
    cspec = pl.BlockSpec((None, N_HEADS, HD, LANE), lambda b, c: (b, 0, 0, 0))
    mspec = pl.BlockSpec((None, 1, LANE), lambda b, c: (b, 0, 0))
    return pl.pallas_call(
        _mlstm_kernel, grid=(n_seq, nc),
        in_specs=[_DST_SPEC, rspec(768), rspec(256), rspec(LANE), cspec, mspec, _full_spec((1, LANE)),
                  _full_spec((1, 256))],
        out_specs=(rspec(256), cspec, mspec), input_output_aliases={0: 0},
        out_shape=(jax.ShapeDtypeStruct(dst.shape, dst.dtype),
                   jax.ShapeDtypeStruct((n_seq, N_HEADS, HD, LANE), F32),
                   jax.ShapeDtypeStruct((n_seq, 1, LANE), F32)),
        scratch_shapes=[pltpu.VMEM((N_HEADS, HD, LANE), F32), pltpu.VMEM((1, LANE), F32)],
        compiler_params=_cparams(("parallel", "arbitrary")), name="mlstm",
    )(dst, mqkv, mo, gates, c0, m0, bias, ng)


def _mla_prompt_kernel(dst_ref, q_ref, kc_ref, vt_ref, o_ref, acc_ref, m_ref, s_ref):
    tq = q_ref.shape[2]
    tk = C_TILE
    i = pl.program_id(1)
    m_ref[...] = jnp.full_like(m_ref, NEG)
    acc_ref[...] = jnp.zeros_like(acc_ref)

    def scores(j, slot, c0=0):
        start = pl.multiple_of(j * tk, tk)
        kt = kc_ref[pl.ds(start, tk), :]
        for h in range(N_HEADS):
            s_ref[slot, h, :, c0:] = _dot(kt, q_ref[h, :, c0:])

    def consume(j, slot, diag_offset=None):
        c0 = 0 if diag_offset is None else diag_offset
        for h in range(N_HEADS):
            s = s_ref[slot, h, :, c0:]
            if diag_offset is not None:
                krow = lax.broadcasted_iota(jnp.int32, (tk, tq - c0), 0) // CHUNK
                qcol = lax.broadcasted_iota(jnp.int32, (tk, tq - c0), 1) // CHUNK
                s = jnp.where(krow <= qcol, s, NEG)
            m_old = m_ref[h, :, c0:]
            m_new = jnp.maximum(m_old, jnp.max(s, axis=0, keepdims=True))
            p = jnp.exp2(s - m_new).astype(BF16)
            acc_ref[h, :, c0:] = jnp.exp2(m_old - m_new) * acc_ref[h, :, c0:] + _dot(vt_ref[j, h], p)
            m_ref[h, :, c0:] = m_new

    scores(0, 0)

    def pair(j):
        scores(j + 1, 1)
        consume(j, 0)
        scores(j + 2, 0)
        consume(j + 1, 1)

    def body_main(jj, carry):
        for u in range(C_UNROLL_PAIRS):
            pair(2 * (C_UNROLL_PAIRS * jj + u))
        return carry

    def body_rest(jj, carry):
        pair(2 * (C_UNROLL_PAIRS * n_main + jj))
        return carry

    n_main = i // C_UNROLL_PAIRS
    lax.fori_loop(0, n_main, body_main, 0)
    lax.fori_loop(0, i - C_UNROLL_PAIRS * n_main, body_rest, 0)
    scores(2 * i + 1, 1, tk)
    consume(2 * i, 0, 0)
    consume(2 * i + 1, 1, tk)

    ys = []
    for h in range(N_HEADS):
        acc = acc_ref[h]
        ys.append(acc[:HD] / acc[HD:HD + 1])
    o_ref[...] = jnp.concatenate(ys, axis=0).T.astype(o_ref.dtype)


def _mla_prompt(dst, qt, kc, vt, batch, seq):
    tq = C_QTILE
    nq = seq // tq
    return pl.pallas_call(
        _mla_prompt_kernel, grid=(batch, nq),
        in_specs=[_DST_SPEC, pl.BlockSpec((N_HEADS, 256, tq), lambda b, i: (0, 0, b * nq + i)),
                  pl.BlockSpec((seq, 256), lambda b, i: (b, 0)),
                  pl.BlockSpec((seq // C_TILE, N_HEADS, V_ROWS, C_TILE), lambda b, i: (b, 0, 0, 0))],
        out_specs=pl.BlockSpec((tq, 256), lambda b, i: (b * nq + i, 0)),
        out_shape=jax.ShapeDtypeStruct(dst.shape, dst.dtype), input_output_aliases={0: 0},
        scratch_shapes=[pltpu.VMEM((N_HEADS, V_ROWS, tq), F32), pltpu.VMEM((N_HEADS, 1, tq), F32),
                        pltpu.VMEM((2, N_HEADS, C_TILE, tq), F32)],
        compiler_params=_cparams(("parallel", "arbitrary")), name="mla_prompt",
    )(dst, qt, kc, vt)


def _mla_sample_kernel(dst_ref, q_ref, ckv_ref, kr_ref, knew_ref, wuv_ref, o_ref):
    group = ckv_ref.shape[0]
    seq = q_ref.shape[1] // group
    for n in range(group):
        rows = slice(n * seq, (n + 1) * seq)
        qs = q_ref[:, rows, :].reshape(N_HEADS * seq, 2 * LANE)
        kcache = jnp.concatenate([ckv_ref[n]] + [kr_ref[n]] * N_HEADS, axis=-1).astype(BF16)
        kall = jnp.concatenate([kcache, knew_ref[rows, :]], axis=0)
        s = _dot_nt(qs, kall)
        m = jnp.max(s, axis=-1, keepdims=True)
        p = jnp.exp2(s - m)
        l = jnp.sum(p, axis=-1, keepdims=True)
        o = (_dot(p.astype(BF16), kall[:, :C_KV_LORA]) / l).astype(BF16)
        for h in range(N_HEADS):
            o_ref[rows, h * HD:(h + 1) * HD] = _dot(o[h * seq:(h + 1) * seq], wuv_ref[h]).astype(o_ref.dtype)


def _mla_sample(dst, qabs, kc, cache_ckv, cache_kr, wuv, layer, row0, n_seq, seq):
    grp = SAMPLE_GROUP
    rows = grp * seq
    assert n_seq % grp == 0 and row0 % rows == 0
    blk0 = row0 // rows
    past = cache_ckv.shape[2]
    return pl.pallas_call(
        _mla_sample_kernel, grid=(n_seq // grp,),
        in_specs=[_DST_SPEC, pl.BlockSpec((N_HEADS, rows, 256), lambda b: (0, b, 0)),
                  pl.BlockSpec((None, grp, past, C_KV_LORA), lambda b: (layer, b, 0, 0)),
                  pl.BlockSpec((None, grp, past, C_ROPE), lambda b: (layer, b, 0, 0)),
                  pl.BlockSpec((rows, 256), lambda b: (blk0 + b, 0)),
                  _full_spec((N_HEADS, 128, HD))],
        out_specs=pl.BlockSpec((rows, 256), lambda b: (blk0 + b, 0)),
        out_shape=jax.ShapeDtypeStruct(dst.shape, dst.dtype), input_output_aliases={0: 0},
        compiler_params=_cparams(("parallel",)), name="mla_sample",
    )(dst, qabs, cache_ckv, cache_kr, kc, wuv)


def _outmlp_kernel(xp_ref, xs_ref, ya_ref, yb_ref, yc_ref, yd_ref, wout_ref, g2_ref, w1_ref, w2_ref, fg_ref,
                   op_ref, os_ref, *, final, n_prompt_tiles):
    i = pl.program_id(0)
    y = jnp.concatenate([ya_ref[...], yb_ref[...], yc_ref[...], yd_ref[...]], axis=-1)
    x1 = jnp.where(i < n_prompt_tiles, xp_ref[...], xs_ref[...]) + _dot(y, wout_ref[...])
    hm = _rms(x1, g2_ref[...]).astype(BF16)
    acc = x1
    for c in range(D_FF // D_MODEL):
        sl = slice(c * D_MODEL, (c + 1) * D_MODEL)
        hc = jnp.square(jnp.maximum(_dot(hm, w1_ref[:, sl]), 0.0)).astype(BF16)
        acc = acc + _dot(hc, w2_ref[sl, :])
    if final:
        acc = _rms(acc, fg_ref[...])

    @pl.when(i < n_prompt_tiles)
    def _():
        op_ref[...] = acc

    @pl.when(i >= n_prompt_tiles)
    def _():
        os_ref[...] = acc


def _outmlp(xp, xs, ya, yb, yc, yd, wout, g2, w1, w2, fg, layer, final, in_place):
    t = xp.shape[0] + xs.shape[0]
    tm = ROW_TILE
    npt = xp.shape[0] // tm
    row = lambda n: pl.BlockSpec((tm, n), lambda i: (i, 0))
    xspecs = _split_rows(npt, D_MODEL)
    return pl.pallas_call(
        functools.partial(_outmlp_kernel, final=final, n_prompt_tiles=npt), grid=(t // tm,),
        in_specs=[*xspecs, row(256), row(256), row(256), row(256), _layer_spec((D_MODEL, D_MODEL), layer),
                  _full_spec((1, D_MODEL)), _layer_spec((D_MODEL, D_FF), layer), _layer_spec((D_FF, D_MODEL), layer),
                  _full_spec((1, D_MODEL))],
        out_specs=xspecs,
        out_shape=(jax.ShapeDtypeStruct(xp.shape, F32), jax.ShapeDtypeStruct(xs.shape, F32)),
        input_output_aliases={0: 0, 1: 1} if in_place else {},
        compiler_params=_cparams(("arbitrary",)), name="outmlp",
    )(xp, xs, ya, yb, yc, yd, wout, g2, w1, w2, fg)


def _rot_cols(w):
    half = C_ROPE // 2
    return jnp.concatenate([-w[..., half:], w[..., :half]], axis=-1)


def _pad_lanes(v, n=LANE, at=0):
    out = jnp.zeros(v.shape[:-1] + (n,), v.dtype)
    return out.at[..., at:at + v.shape[-1]].set(v)


def _prep_w_in(w):
    sizes = (256, 256, 256, 256, 512, 4, 256, 128, 32, 256, 256, 256, 4, 4, 256)
    cuts = [0]
    for s in sizes:
        cuts.append(cuts[-1] + s)
    (a_q, a_k, a_v, b_z, b_xbc, b_dt, c_q, c_kv, c_kr, m_q, m_k, m_v, m_i, m_f, m_o) = (
        w[..., cuts[n]:cuts[n + 1]] for n in range(len(sizes)))
    kr4 = jnp.concatenate([c_kr] * N_HEADS, axis=-1)
    krr4 = jnp.concatenate([_rot_cols(c_kr)] * N_HEADS, axis=-1)
    gates = _pad_lanes(jnp.concatenate([b_dt, m_i, m_f], axis=-1))
    out = jnp.concatenate([a_q, a_k, a_v, b_z, b_xbc, c_q, c_kv, kr4, krr4, gates, m_q, m_k, m_v, m_o], axis=-1)
    assert out.shape[-1] == N_PROJ
    return out.astype(BF16)


def _prep_w_uq(w):
    d = w.shape[0]
    w4 = w.reshape(d, 256, N_HEADS, C_NOPE + C_ROPE)
    nope = w4[..., :C_NOPE].reshape(d, 256, N_HEADS * C_NOPE)
    rope = w4[..., C_NOPE:]
    return jnp.concatenate([nope, rope.reshape(d, 256, N_HEADS * C_ROPE),
                            _rot_cols(rope).reshape(d, 256, N_HEADS * C_ROPE)], axis=-1).astype(BF16)


def _prep_w_ukv(w):
    d = w.shape[0]
    w4 = w.reshape(d, C_KV_LORA, N_HEADS, C_NOPE + HD)
    w_uk = w4[..., :C_NOPE]
    w_uv = w4[..., C_NOPE:]
    wabs = jnp.zeros((d, N_HEADS * C_NOPE, N_HEADS * C_KV_LORA), w.dtype)
    for h in range(N_HEADS):
        wabs = wabs.at[:, h * C_NOPE:(h + 1) * C_NOPE, h * C_KV_LORA:(h + 1) * C_KV_LORA].set(
            jnp.swapaxes(w_uk[:, :, h, :], 1, 2))
    wuv = jnp.transpose(w_uv, (0, 2, 1, 3))
    wuv_all = w_uv.reshape(d, C_KV_LORA, N_HEADS * HD)
    return wabs.astype(BF16), wuv.astype(BF16), wuv_all.astype(BF16)


def _rope_tables(pos):
    half = C_ROPE // 2
    inv = jnp.exp(-math.log(ROPE_BASE) * jnp.arange(half, dtype=F32) / half)
    ang = pos.astype(F32)[:, None] * inv[None, :]
    reps = LANE // half
    return jnp.tile(jnp.cos(ang), (1, reps)), jnp.tile(jnp.sin(ang), (1, reps))


def _band_bias(table, q_rows):
    cols = q_rows + A_SPAN
    k = np.arange(q_rows + cols - 1)
    u = table[:, np.clip((q_rows - 1 - k) + A_SPAN, -A_REL_CLIP, A_REL_CLIP) + A_REL_CLIP].astype(F32)
    period = q_rows + cols
    u = jnp.pad(u, ((0, 0), (0, 1)))
    flat = jnp.tile(u, (1, q_rows))[:, q_rows - 1:q_rows - 1 + q_rows * (period - 1)]
    toep = flat.reshape(table.shape[0], q_rows, period - 1)[:, :, :cols]
    r = np.arange(q_rows)[:, None]
    s = np.arange(cols)[None, :] - (r // CHUNK) * CHUNK
    inband = (s >= 0) & (s < A_SPAN + CHUNK)
    return jnp.where(jnp.asarray(inband)[None], toep, NEG)


def kernel(x_prompt, x_sample, cache_attn_k, cache_attn_v, state_ssm_conv, state_ssm, cache_mla_ckv, cache_mla_kr,
           state_mlstm_c, state_mlstm_n, state_mlstm_m, norm1_g, w_in, attn_rel_bias, ssm_conv_w, ssm_conv_b,
           ssm_dt_bias, ssm_a_log, ssm_d, ssm_norm_g, mla_q_norm_g, mla_w_uq, mla_kv_norm_g, mla_w_ukv,
           mlstm_b_i, mlstm_b_f, mlstm_norm_g, w_out, norm2_g, mlp_w1, mlp_w2, final_g):
    batch, seq, _ = x_prompt.shape
    dbatch, dseq, _ = x_sample.shape
    depth = w_in.shape[0]
    past = cache_mla_ckv.shape[2]
    tp, ts = batch * seq, dbatch * dseq
    lp = REC_CHUNK_PROMPT * REC_STEP_CHUNKS
    assert seq % lp == 0
    assert dseq == CHUNK and past % CHUNK == 0 and cache_attn_k.shape[2] == A_SPAN
    assert seq % C_QTILE == 0 and tp % ROW_TILE == 0 and ts % ROW_TILE == 0

    w_in_r = _prep_w_in(w_in)
    wuq_r = _prep_w_uq(mla_w_uq)
    wabs, wuv, wuv_all = _prep_w_ukv(mla_w_ukv)
    w_out_b, w1_b, w2_b = w_out.astype(BF16), mlp_w1.astype(BF16), mlp_w2.astype(BF16)
    pos = jnp.concatenate([jnp.tile(jnp.arange(seq), batch), jnp.tile(past + jnp.arange(dseq), dbatch)])
    cos, sin = _rope_tables(pos)
    dtb = _pad_lanes(ssm_dt_bias, at=G_DT)[:, None, :]
    alog = _pad_lanes(ssm_a_log, at=G_DT)[:, None, :]
    dsk = _pad_lanes(ssm_d)[:, None, :]
    dtb_t = jnp.swapaxes(dtb[:, :, :GT_ROWS], 1, 2)
    alog_t = jnp.swapaxes(alog[:, :, :GT_ROWS], 1, 2)
    gate_bias =(_pad_lanes(mlstm_b_i, at=G_I) + _pad_lanes(mlstm_b_f, at=G_F))[:, None, :]
    gate_bias_t = jnp.swapaxes(gate_bias[:, :, :GT_ROWS], 1, 2)
    cache_k = cache_attn_k.reshape(depth, dbatch, A_SPAN, 256)
    cache_v = cache_attn_v.reshape(depth, dbatch, A_SPAN, 256)
    conv0_s = jnp.pad(state_ssm_conv, ((0, 0), (0, 0), (8 - (B_CONV - 1), 0), (0, 0)))
    caug0_s = jnp.concatenate([state_mlstm_c, state_mlstm_n[..., None],
                               jnp.zeros(state_mlstm_c.shape[:-1] + (LANE - HD - 1,), F32)], axis=-1)
    m0_s = _pad_lanes(state_mlstm_m)[:, :, None, :]

    xp, xs = x_prompt.reshape(tp, D_MODEL), x_sample.reshape(ts, D_MODEL)
    outs = [[] for _ in range(18)]
    ya, yb, yc, yd = (jnp.zeros((tp + ts, GROUP_W), BF16) for _ in range(4))
    fg = final_g[None, :]
    for l in range(depth):
        (aq, akv, bz, bxbc, qabs, qt, ckv_p, ckv_s, kr_p, kr_s, kc, vt, mqkv, mo, gates, mkt, mvt, gt, aqt, akh,
         avt) = _inproj(
            xp, xs, norm1_g[l][None], w_in_r, mla_q_norm_g[l][None], mla_kv_norm_g[l][None], wuq_r, wabs,
            wuv_all, cos, sin, l)

        rel = attn_rel_bias[l] * math.log2(math.e)
        ya = _attn_a_prompt(ya, aqt, akh, avt, jnp.swapaxes(_band_bias(rel, A_QBLOCK), 1, 2), batch, seq)
        ya = _attn_a_sample(ya, aq, akv, cache_k, cache_v, _band_bias(rel, CHUNK), l, tp, dbatch, dseq)

        ssd_w = (ssm_conv_w[l], ssm_conv_b[l][None], dtb[l], alog[l], dsk[l], ssm_norm_g[l][None])
        yb, h_p = _ssd_prompt(yb, bz, bxbc, gates, gt, ssm_conv_w[l], ssm_conv_b[l][None], dtb[l], dtb_t[l], alog[l],
                              alog_t[l], dsk[l], ssm_norm_g[l][None], batch, seq, lp)
        yb, h_s = _ssd(yb, bz, bxbc, gates, conv0_s[l], state_ssm[l], *ssd_w, tp, dbatch, dseq, dseq)

        yc = _mla_prompt(yc, qt, kc, vt, batch, seq)
        yc = _mla_sample(yc, qabs, kc, cache_mla_ckv, cache_mla_kr, wuv[l], l, tp, dbatch, dseq)

        ml_w = (gate_bias[l], mlstm_norm_g[l][None])
        yd, ct_p, m_p = _mlstm_prompt(yd, mqkv, mkt, mvt, mo, gates, gt, gate_bias[l], gate_bias_t[l],
                                      mlstm_norm_g[l][None], batch, seq, lp)
        yd, c_s, m_s = _mlstm(yd, mqkv, mo, gates, caug0_s[l], m0_s[l], *ml_w, tp, dbatch, dseq, dseq)

        xp, xs = _outmlp(xp, xs, ya, yb, yc, yd, w_out_b, norm2_g[l][None], w1_b, w2_b, fg, l, l == depth - 1, l > 0)

        keep = min(A_SPAN, seq)
        tail = lambda a, n: jnp.stack([a[(b + 1) * seq - n:(b + 1) * seq] for b in range(batch)])
        akv_p = tail(akv, keep).reshape(batch, keep, 2, N_HEADS, HD)
        akv_s = akv[tp:].reshape(dbatch, dseq, 2, N_HEADS, HD)
        new = (akv_p[:, :, 0], akv_p[:, :, 1], tail(bxbc, B_CONV - 1), h_p,
               ckv_p.reshape(batch, seq, C_KV_LORA), kr_p.reshape(batch, seq, C_ROPE),
               jnp.swapaxes(ct_p[:, :, :HD, :], 2, 3), ct_p[:, :, HD, :], m_p[:, 0, :N_HEADS],
               akv_s[:, :, 0], akv_s[:, :, 1],
               bxbc[tp:].reshape(dbatch, dseq, B_CONV_DIM)[:, dseq - (B_CONV - 1):], h_s,
               ckv_s.reshape(dbatch, dseq, C_KV_LORA), kr_s.reshape(dbatch, dseq, C_ROPE),
               c_s[..., :HD], c_s[..., HD], m_s[:, 0, :N_HEADS])
        for dst, v in zip(outs, new):
            dst.append(v)

    y_prompt = xp.reshape(batch, seq, D_MODEL)
    y_sample = xs.reshape(dbatch, dseq, D_MODEL)
    res = [jnp.stack(v) for v in outs]
    res[9] = jnp.concatenate([cache_attn_k[:, :, dseq:], res[9]], axis=2)
    res[10] = jnp.concatenate([cache_attn_v[:, :, dseq:], res[10]], axis=2)
    return (y_prompt, y_sample) + tuple(res)
```
